```python
import jax, jax.numpy as jnp
from jax import lax
import numpy as np

D_MODEL = 2048
BATCH = 8
SEQ = 2048
DEPTH = 2

N_MIXERS = 2
N_FOX = (DEPTH + 1) // 2
N_SWA = DEPTH // 2
Q_BLOCK = 128

FOX_HEADS = 16
FOX_HEAD_DIM = D_MODEL // FOX_HEADS
FOX_WIDTH = FOX_HEADS * FOX_HEAD_DIM

SWA_HEAD_DIM = 64
SWA_Q_HEADS = D_MODEL // SWA_HEAD_DIM
SWA_KV_HEADS = SWA_Q_HEADS // 8
SWA_GROUP = SWA_Q_HEADS // SWA_KV_HEADS
SWA_WINDOW = 128
ROPE_THETA = 500000.0
ROPE_DIM = SWA_HEAD_DIM // 4

D_FF = 5632
CONV_WIDTH = 3

DEEPNORM_ALPHA = (2.0 * DEPTH) ** 0.25
DEEPNORM_BETA = (8.0 * DEPTH) ** -0.25
LN_EPS = 1e-5
ADA_SCALE = 0.2
MAX_POS_OFFSET = 4096

kernel_name = "hybrid_fox_swa_sink_convffn_deepnorm_adaln"


def layer_norm(x, g, b):
    xf = x.astype(jnp.float32)
    mu = jnp.mean(xf, axis=-1, keepdims=True)
    var = jnp.mean(jnp.square(xf - mu), axis=-1, keepdims=True)
    y = (xf - mu) * lax.rsqrt(var + LN_EPS)
    return (y * g.astype(jnp.float32) + b.astype(jnp.float32)).astype(x.dtype)


def rope_partial(t, pos):
    inv_freq = ROPE_THETA ** (-jnp.arange(0, ROPE_DIM, 2, dtype=jnp.float32) / ROPE_DIM)
    ang = pos.astype(jnp.float32)[..., None] * inv_freq
    cos = jnp.cos(ang)[:, :, None, :]
    sin = jnp.sin(ang)[:, :, None, :]
    tr = t[..., :ROPE_DIM].astype(jnp.float32)
    t1, t2 = tr[..., :ROPE_DIM // 2], tr[..., ROPE_DIM // 2:]
    rot = jnp.concatenate([t1 * cos - t2 * sin, t2 * cos + t1 * sin], axis=-1)
    return jnp.concatenate([rot.astype(t.dtype), t[..., ROPE_DIM:]], axis=-1)


def fox_attention(h, w_in, b_f, w_o):
    B, S, _ = h.shape
    H, dh = FOX_HEADS, FOX_HEAD_DIM
    proj = h @ w_in
    q = proj[..., :FOX_WIDTH].reshape(B, S, H, dh)
    k = proj[..., FOX_WIDTH:2 * FOX_WIDTH].reshape(B, S, H, dh)
    v = proj[..., 2 * FOX_WIDTH:3 * FOX_WIDTH].reshape(B, S, H, dh)
    f_logit = proj[..., 3 * FOX_WIDTH:] + b_f
    log_f = jax.nn.log_sigmoid(f_logit.astype(jnp.float32))
    cum = jnp.cumsum(log_f, axis=1).transpose(0, 2, 1)
    nb = S // Q_BLOCK
    q_blocks = q.reshape(B, nb, Q_BLOCK, H, dh).transpose(1, 0, 2, 3, 4)
    cq_blocks = cum.reshape(B, H, nb, Q_BLOCK).transpose(2, 0, 1, 3)
    key_pos = jnp.arange(S)
    scale = FOX_HEAD_DIM ** -0.5

    def one_block(args):
        qb, cqb, bi = args
        s = jnp.einsum('bqhd,bkhd->bhqk', qb, k).astype(jnp.float32) * scale
        s = s + cqb[..., None] - cum[:, :, None, :]
        q_pos = bi * Q_BLOCK + jnp.arange(Q_BLOCK)
        causal = key_pos[None, :] <= q_pos[:, None]
        s = jnp.where(causal, s, -jnp.inf)
        p = jax.nn.softmax(s, axis=-1).astype(v.dtype)
        return jnp.einsum('bhqk,bkhd->bqhd', p, v)

    o = lax.map(one_block, (q_blocks, cq_blocks, jnp.arange(nb)))
    o = o.transpose(1, 0, 2, 3, 4).reshape(B, S, FOX_WIDTH)
    return o @ w_o


def swa_attention(h, pos, w_in, sinks, w_o):
    B, S, _ = h.shape
    Hq, Hk, G, dh = SWA_Q_HEADS, SWA_KV_HEADS, SWA_GROUP, SWA_HEAD_DIM
    proj = h @ w_in
    q = proj[..., :Hq * dh].reshape(B, S, Hq, dh)
    k = proj[..., Hq * dh:(Hq + Hk) * dh].reshape(B, S, Hk, dh)
    v = proj[..., (Hq + Hk) * dh:].reshape(B, S, Hk, dh)
    q = rope_partial(q, pos)
    k = rope_partial(k, pos)
    nb = S // Q_BLOCK
    qb = q.reshape(B, nb, Q_BLOCK, Hk, G, dh)

    def band(t):
        tb = t.reshape(B, nb, Q_BLOCK, Hk, dh)
        prev = jnp.concatenate([jnp.zeros_like(tb[:, :1]), tb[:, :-1]], axis=1)
        return jnp.concatenate([prev, tb], axis=2)

    kb, vb = band(k), band(v)
    s = jnp.einsum('bnqhgd,bnkhd->bnhgqk', qb, kb).astype(jnp.float32) * (dh ** -0.5)
    qi = jnp.arange(Q_BLOCK)[:, None]
    kj = jnp.arange(2 * Q_BLOCK)[None, :]
    rel = qi + Q_BLOCK - kj
    key_abs = (jnp.arange(nb) * Q_BLOCK)[:, None] - Q_BLOCK + kj
    mask = (rel >= 0)[None] & (rel < SWA_WINDOW)[None] & (key_abs[:, None, :] >= 0)
    s = jnp.where(mask[None, :, None, None], s, -jnp.inf)
    sink = jnp.broadcast_to(sinks.astype(jnp.float32).reshape(1, 1, Hk, G, 1, 1), s.shape[:-1] + (1,))
    p = jax.nn.softmax(jnp.concatenate([s, sink], axis=-1), axis=-1)[..., :-1].astype(v.dtype)
    o = jnp.einsum('bnhgqk,bnkhd->bnqhgd', p, vb).reshape(B, S, Hq * dh)
    return o @ w_o


def conv_ffn(h, w_up, conv_w, conv_b, w_down):
    S = h.shape[1]
    u = h @ w_up
    up = jnp.pad(u, ((0, 0), (CONV_WIDTH - 1, 0), (0, 0)))
    u = sum(up[:, j:j + S] * conv_w[j] for j in range(CONV_WIDTH)) + conv_b
    g, val = u[..., :D_FF], u[..., D_FF:]
    return (jax.nn.silu(g) * val) @ w_down


def setup_inputs(seed: int = 0) -> dict:
    key = jax.random.key(seed)
    ks = jax.random.split(key, 20)
    f32 = jnp.float32
    n = lambda k, shape, s: (jax.random.normal(k, shape, f32) * s)
    D = D_MODEL
    x = n(ks[0], (BATCH, SEQ, D), 1.0)
    c = n(ks[1], (BATCH, D), 1.0)
    offset = jax.random.randint(ks[2], (BATCH, 1), 0, MAX_POS_OFFSET, dtype=jnp.int32)
    positions = (offset + jnp.arange(SEQ, dtype=jnp.int32)[None, :]).astype(jnp.int32)
    fox_w_in = n(ks[3], (N_FOX, D, 3 * FOX_WIDTH + FOX_HEADS), D ** -0.5)
    fox_b_f = n(ks[4], (N_FOX, FOX_HEADS), 0.1)
    fox_w_o = n(ks[5], (N_FOX, FOX_WIDTH, D), FOX_WIDTH ** -0.5 * DEEPNORM_BETA)
    swa_w_in = n(ks[6], (N_SWA, D, (SWA_Q_HEADS + 2 * SWA_KV_HEADS) * SWA_HEAD_DIM), D ** -0.5)
    swa_sinks = n(ks[7], (N_SWA, SWA_Q_HEADS), 0.5)
    swa_w_o = n(ks[8], (N_SWA, SWA_Q_HEADS * SWA_HEAD_DIM, D), (SWA_Q_HEADS * SWA_HEAD_DIM) ** -0.5 * DEEPNORM_BETA)
    ada_w = n(ks[9], (DEPTH, D, 6 * D), ADA_SCALE * D ** -0.5)
    ada_b = n(ks[10], (DEPTH, 6 * D), 0.02)
    ffn_w_up = n(ks[11], (DEPTH, D, 2 * D_FF), D ** -0.5)
    ffn_conv_w = n(ks[12], (DEPTH, CONV_WIDTH, 2 * D_FF), CONV_WIDTH ** -0.5)
    ffn_conv_b = n(ks[13], (DEPTH, 2 * D_FF), 0.02)
    ffn_w_down = n(ks[14], (DEPTH, D_FF, D), D_FF ** -0.5 * DEEPNORM_BETA)
    ln_mix_g = 1.0 + n(ks[15], (DEPTH, D), 0.02)
    ln_mix_b = n(ks[16], (DEPTH, D), 0.02)
    ln_ffn_g = 1.0 + n(ks[17], (DEPTH, D), 0.02)
    ln_ffn_b = n(ks[18], (DEPTH, D), 0.02)
    return {"x": x, "c": c, "positions": positions,
            "fox_w_in": fox_w_in, "fox_b_f": fox_b_f, "fox_w_o": fox_w_o,
            "swa_w_in": swa_w_in, "swa_sinks": swa_sinks, "swa_w_o": swa_w_o,
            "ada_w": ada_w, "ada_b": ada_b,
            "ffn_w_up": ffn_w_up, "ffn_conv_w": ffn_conv_w, "ffn_conv_b": ffn_conv_b, "ffn_w_down": ffn_w_down,
            "ln_mix_g": ln_mix_g, "ln_mix_b": ln_mix_b, "ln_ffn_g": ln_ffn_g, "ln_ffn_b": ln_ffn_b}


def reference(x, c, positions, fox_w_in, fox_b_f, fox_w_o, swa_w_in, swa_sinks, swa_w_o,
              ada_w, ada_b, ffn_w_up, ffn_conv_w, ffn_conv_b, ffn_w_down,
              ln_mix_g, ln_mix_b, ln_ffn_g, ln_ffn_b):
    c_act = jax.nn.silu(c)
    for i in range(DEPTH):
        mod = c_act @ ada_w[i] + ada_b[i]
        sh1, sc1, g1, sh2, sc2, g2 = jnp.split(mod[:, None, :], 6, axis=-1)
        h = x * (1.0 + sc1) + sh1
        j = i // N_MIXERS
        if i % N_MIXERS == 0:
            y = fox_attention(h, fox_w_in[j], fox_b_f[j], fox_w_o[j])
        else:
            y = swa_attention(h, positions, swa_w_in[j], swa_sinks[j], swa_w_o[j])
        x = layer_norm(DEEPNORM_ALPHA * x + (1.0 + g1) * y, ln_mix_g[i], ln_mix_b[i])
        h = x * (1.0 + sc2) + sh2
        y = conv_ffn(h, ffn_w_up[i], ffn_conv_w[i], ffn_conv_b[i], ffn_w_down[i])
        x = layer_norm(DEEPNORM_ALPHA * x + (1.0 + g2) * y, ln_ffn_g[i], ln_ffn_b[i])
    return x
```

```python
import functools

import jax
import jax.numpy as jnp
from jax import lax
from jax.experimental import pallas as pl
from jax.experimental.pallas import tpu as pltpu

F32 = jnp.float32
BF16 = jnp.bfloat16

LANES = 128
SUBLANES = 8
VMEM_LIMIT_BYTES = 56 * 1024 * 1024

LN_EPS = 1e-5
ROPE_THETA = 500000.0
ROPE_DIM = 16
ROPE_HALF = ROPE_DIM // 2
SWA_HEAD_DIM = 64
SWA_WINDOW = 128
CONV_WIDTH = 3
MASK_VALUE = -1e30

SH1, SC1, G1, SH2, SC2, G2 = range(6)


def _params(semantics):
    return pltpu.CompilerParams(dimension_semantics=semantics, vmem_limit_bytes=VMEM_LIMIT_BYTES)


def _dot(a, b):
    return jnp.dot(a, b, preferred_element_type=F32)


def _dot_nt(a, b):
    return lax.dot_general(a, b, (((1,), (1,)), ((), ())), preferred_element_type=F32)


def _ada_kernel(c_ref, w_ref, b_ref, o_ref):
    c = c_ref[...]
    c_act = (c * jax.nn.sigmoid(c)).astype(BF16)
    o_ref[...] = _dot(c_act, w_ref[...].astype(BF16)) + b_ref[...]


def _ada(c, ada_w, ada_b):
    depth, d, six_d = ada_w.shape
    batch = c.shape[0]
    tn = min(1024, d)
    nt = d // tn
    return pl.pallas_call(
        _ada_kernel,
        grid=(depth, 6, nt),
        in_specs=[
            pl.BlockSpec((batch, d), lambda i, k, j: (0, 0)),
            pl.BlockSpec((None, d, tn), lambda i, k, j: (i, 0, k * nt + j)),
            pl.BlockSpec((None, 1, tn), lambda i, k, j: (i, 0, k * nt + j)),
        ],
        out_specs=pl.BlockSpec((None, None, batch, tn), lambda i, k, j: (i, k, 0, j)),
        out_shape=jax.ShapeDtypeStruct((depth, 6, batch, d), F32),
        compiler_params=_params(("parallel", "parallel", "parallel")),
        name="ada_mod",
    )(c, ada_w, ada_b.reshape(depth, 1, six_d))


def _inproj_fox_kernel(*refs, modulate, tiles_per_seq, q_tiles, q_scale):
    if modulate:
        x_ref, sc_ref, sh_ref, w_ref, wf_ref, bf_ref, qkv_ref, fl_ref, h_scr = refs
    else:
        x_ref, w_ref, wf_ref, bf_ref, qkv_ref, fl_ref = refs
    i = pl.program_id(0)
    j = pl.program_id(1)

    if modulate:
        @pl.when(j == 0)
        def _():
            b = i // tiles_per_seq
            sc = sc_ref[pl.ds(b, 1), :]
            sh = sh_ref[pl.ds(b, 1), :]
            h_scr[...] = (x_ref[...] * (1.0 + sc) + sh).astype(BF16)
        h_ref = h_scr
    else:
        h_ref = x_ref

    @pl.when(j == 0)
    def _():
        fl_ref[...] = _dot(h_ref[...], wf_ref[...]) + bf_ref[...]

    acc = _dot(h_ref[...], w_ref[...])
    acc = acc * jnp.where(j < q_tiles, q_scale, 1.0)
    for hh in range(acc.shape[1] // LANES):
        qkv_ref[hh] = acc[:, hh * LANES:(hh + 1) * LANES].astype(BF16)


def _inproj_fox(x_or_h, sc, sh, w_qkv, w_f, b_f, *, head_dim):
    batch, seq, d = x_or_h.shape
    modulate = sc is not None
    n_out = w_qkv.shape[1]
    assert head_dim == LANES and n_out == 3 * d
    tn = min(512, d)
    tm = min(1024 if modulate else 2048, seq)
    nm = seq // tm
    slots = n_out // LANES
    kern = functools.partial(_inproj_fox_kernel, modulate=modulate, tiles_per_seq=nm,
                             q_tiles=d // tn, q_scale=float(head_dim) ** -0.5)
    in_specs = [pl.BlockSpec((None, tm, d), lambda i, j: (i // nm, i % nm, 0))]
    args = [x_or_h]
    if modulate:
        in_specs += [pl.BlockSpec((batch, d), lambda i, j: (0, 0))] * 2
        args += [sc, sh]
    in_specs += [
        pl.BlockSpec((d, tn), lambda i, j: (0, j)),
        pl.BlockSpec((d, LANES), lambda i, j: (0, 0)),
        pl.BlockSpec((1, LANES), lambda i, j: (0, 0)),
    ]
    args += [w_qkv, w_f, b_f]
    return pl.pallas_call(
        kern,
        grid=(batch * nm, n_out // tn),
        in_specs=in_specs,
        out_specs=[
            pl.BlockSpec((None, tn // LANES, tm, LANES), lambda i, j: (i // nm, j, i % nm, 0)),
            pl.BlockSpec((None, tm, LANES), lambda i, j: (i // nm, i % nm, 0)),
        ],
        out_shape=[
            jax.ShapeDtypeStruct((batch, slots, seq, LANES), BF16),
            jax.ShapeDtypeStruct((batch, seq, LANES), F32),
        ],
        scratch_shapes=[pltpu.VMEM((tm, d), BF16)] if modulate else [],
        compiler_params=_params(("parallel", "arbitrary")),
        name="fox_inproj",
    )(*args)


def _fox_cum_kernel(fl_ref, col_ref, row_ref, *, heads):
    seq = fl_ref.shape[0]
    x = fl_ref[...]
    log_f = jnp.minimum(x, 0.0) - jnp.log1p(jnp.exp(-jnp.abs(x)))
    r = lax.broadcasted_iota(jnp.int32, (LANES, LANES), 0)
    c = lax.broadcasted_iota(jnp.int32, (LANES, LANES), 1)
    tri = (c <= r).astype(BF16)
    carry = jnp.zeros((1, LANES), F32)
    for blk in range(seq // LANES):
        xb = log_f[blk * LANES:(blk + 1) * LANES]
        hi = xb.astype(BF16)
        rem = xb - hi.astype(F32)
        mid = rem.astype(BF16)
        lo = (rem - mid.astype(F32)).astype(BF16)
        cb = _dot(tri, hi) + _dot(tri, mid) + _dot(tri, lo) + carry
        col_ref[blk * LANES:(blk + 1) * LANES, :] = cb
        carry = cb[LANES - 1:LANES, :]
    row_ref[...] = col_ref[...].T[:heads]


def _fox_cum(fl, heads):
    batch, seq, _ = fl.shape
    return pl.pallas_call(
        functools.partial(_fox_cum_kernel, heads=heads),
        grid=(batch,),
        in_specs=[pl.BlockSpec((None, seq, LANES), lambda b: (b, 0, 0))],
        out_specs=[
            pl.BlockSpec((None, seq, LANES), lambda b: (b, 0, 0)),
            pl.BlockSpec((None, heads, seq), lambda b: (b, 0, 0)),
        ],
        out_shape=[
            jax.ShapeDtypeStruct((batch, seq, LANES), F32),
            jax.ShapeDtypeStruct((batch, heads, seq), F32),
        ],
        compiler_params=_params(("parallel",)),
        name="fox_cum",
    )(fl)


def _fox_attn_kernel(q_ref, k_ref, v_ref, crow_ref, ccol_ref, o_ref, *, heads_per_step, blk):
    seq = q_ref.shape[1]
    nblk = seq // blk
    hb = pl.program_id(1)
    lane = lax.broadcasted_iota(jnp.int32, (blk, LANES), 1)
    rows = lax.broadcasted_iota(jnp.int32, (blk, blk), 0)
    cols = lax.broadcasted_iota(jnp.int32, (blk, blk), 1)
    causal = cols <= rows

    def block_update(hh, q, cq, k0, carry, masked):
        m_old, l_old, acc = carry
        h = hb * heads_per_step + hh
        k = k_ref[hh, pl.ds(k0, blk), :]
        v = v_ref[hh, pl.ds(k0, blk), :]
        ck = crow_ref[pl.ds(h, 1), pl.ds(k0, blk)]
        s = _dot_nt(q, k) - ck
        if masked:
            s = jnp.where(causal, s, MASK_VALUE)
        m_new = jnp.maximum(m_old, jnp.max(s, axis=-1, keepdims=True) + cq)
        p = jnp.exp(s + (cq - m_new))
        alpha = jnp.exp(m_old - m_new)
        l_new = alpha * l_old + jnp.sum(p, axis=-1, keepdims=True)
        acc_new = alpha * acc + _dot(p.astype(BF16), v)
        return m_new, l_new, acc_new

    def q_block(qi, _):
        q0 = pl.multiple_of(qi * blk, blk)
        qs, cqs = [], []
        for hh in range(heads_per_step):
            h = hb * heads_per_step + hh
            qs.append(q_ref[hh, pl.ds(q0, blk), :])
            col = ccol_ref[pl.ds(q0, blk), :]
            cqs.append(jnp.sum(jnp.where(lane == h, col, 0.0), axis=-1, keepdims=True))
        init = tuple((jnp.full((blk, 1), MASK_VALUE, F32), jnp.zeros((blk, 1), F32),
                      jnp.zeros((blk, LANES), F32)) for _ in range(heads_per_step))

        def k_block(kj, carries):
            k0 = pl.multiple_of(kj * blk, blk)
            return tuple(block_update(hh, qs[hh], cqs[hh], k0, carries[hh], False)
                         for hh in range(heads_per_step))

        carries = lax.fori_loop(0, qi, k_block, init)
        for hh in range(heads_per_step):
            _, l_fin, acc = block_update(hh, qs[hh], cqs[hh], q0, carries[hh], True)
            o_ref[pl.ds(q0, blk), hh * LANES:(hh + 1) * LANES] = (acc / l_fin).astype(BF16)
        return 0

    lax.fori_loop(0, nblk, q_block, 0)


def _fox_attn(qkv, crow, ccol, *, heads):
    batch, slots, seq, dh = qkv.shape
    assert slots == 3 * heads and dh == LANES
    hps = 2 if heads % 2 == 0 else 1
    groups = heads // hps
    blk = min(256, seq)
    kern = functools.partial(_fox_attn_kernel, heads_per_step=hps, blk=blk)
    head_spec = lambda off: pl.BlockSpec((None, hps, seq, dh), lambda b, g: (b, off + g, 0, 0))
    return pl.pallas_call(
        kern,
        grid=(batch, groups),
        in_specs=[
            head_spec(0), head_spec(groups), head_spec(2 * groups),
            pl.BlockSpec((None, heads, seq), lambda b, g: (b, 0, 0)),
            pl.BlockSpec((None, seq, LANES), lambda b, g: (b, 0, 0)),
        ],
        out_specs=pl.BlockSpec((None, seq, hps * dh), lambda b, g: (b, 0, g)),
        out_shape=jax.ShapeDtypeStruct((batch, seq, heads * dh), BF16),
        compiler_params=_params(("parallel", "parallel")),
        name="fox_attn",
    )(qkv, qkv, qkv, crow, ccol)


def _rope_table_kernel(pos_ref, inv_ref, o_ref):
    pos = pos_ref[...].astype(F32)
    ang = pos * inv_ref[...]
    cos = jnp.cos(ang)
    sin = jnp.sin(ang)
    idx = lax.broadcasted_iota(jnp.int32, ang.shape, 1) % SWA_HEAD_DIM
    o_ref[0] = jnp.where(idx < ROPE_DIM, cos, 1.0)
    o_ref[1] = jnp.where((idx >= ROPE_HALF) & (idx < ROPE_DIM), sin, 0.0)
    o_ref[2] = jnp.where(idx < ROPE_HALF, -sin, 0.0)


def _rope_tables(positions):
    batch, seq = positions.shape
    inv_freq = ROPE_THETA ** (-jnp.arange(0, ROPE_DIM, 2, dtype=F32) / ROPE_DIM)
    idx = jnp.arange(LANES) % SWA_HEAD_DIM
    inv_lane = jnp.where(idx < ROPE_DIM, inv_freq[idx % ROPE_HALF], 0.0).reshape(1, LANES)
    return pl.pallas_call(
        _rope_table_kernel,
        grid=(batch,),
        in_specs=[
            pl.BlockSpec((None, seq, 1), lambda b: (b, 0, 0)),
            pl.BlockSpec((1, LANES), lambda b: (0, 0)),
        ],
        out_specs=pl.BlockSpec((3, None, seq, LANES), lambda b: (0, b, 0, 0)),
        out_shape=jax.ShapeDtypeStruct((3, batch, seq, LANES), F32),
        compiler_params=_params(("parallel",)),
        name="rope_tables",
    )(positions.reshape(batch, seq, 1), inv_lane)


def _inproj_swa_kernel(h_ref, wq_ref, wkv_ref, rt_ref, q_ref, kv_ref, *, q_tiles, q_scale, k_width):
    j = pl.program_id(1)
    cos = rt_ref[0]
    sin_lo = rt_ref[1]
    sin_hi = rt_ref[2]

    def rope(a):
        return (a * cos + pltpu.roll(a, ROPE_HALF, 1) * sin_lo
                + pltpu.roll(a, LANES - ROPE_HALF, 1) * sin_hi)

    @pl.when(j < q_tiles)
    def _():
        acc = _dot(h_ref[...], wq_ref[...])
        for cc in range(acc.shape[1] // LANES):
            sl = slice(cc * LANES, (cc + 1) * LANES)
            q_ref[:, sl] = (rope(acc[:, sl]) * q_scale).astype(BF16)

    @pl.when(j == q_tiles)
    def _():
        acc = _dot(h_ref[...], wkv_ref[...])
        for cc in range(acc.shape[1] // LANES):
            sl = slice(cc * LANES, (cc + 1) * LANES)
            a = acc[:, sl]
            kv_ref[:, sl] = (rope(a) if cc * LANES < k_width else a).astype(BF16)


def _inproj_swa(h, w_q, w_kv, rope_tables):
    batch, seq, d = h.shape
    nq = w_q.shape[1]
    kvw = w_kv.shape[1]
    assert (kvw // 2) % LANES == 0
    tn = min(512, nq)
    q_tiles = nq // tn
    kern = functools.partial(_inproj_swa_kernel, q_tiles=q_tiles,
                             q_scale=float(SWA_HEAD_DIM) ** -0.5, k_width=kvw // 2)
    return pl.pallas_call(
        kern,
        grid=(batch, q_tiles + 1),
        in_specs=[
            pl.BlockSpec((None, seq, d), lambda b, j: (b, 0, 0)),
            pl.BlockSpec((d, tn), lambda b, j: (0, jnp.minimum(j, q_tiles - 1))),
            pl.BlockSpec((d, kvw), lambda b, j: (0, 0)),
            pl.BlockSpec((3, None, seq, LANES), lambda b, j: (0, b, 0, 0)),
        ],
        out_specs=[
            pl.BlockSpec((None, seq, tn), lambda b, j: (b, 0, jnp.minimum(j, q_tiles - 1))),
            pl.BlockSpec((None, seq, kvw), lambda b, j: (b, 0, 0)),
        ],
        out_shape=[
            jax.ShapeDtypeStruct((batch, seq, nq), BF16),
            jax.ShapeDtypeStruct((batch, seq, kvw), BF16),
        ],
        compiler_params=_params(("parallel", "arbitrary")),
        name="swa_inproj",
    )(h, w_q, w_kv, rope_tables)


def _swa_attn_kernel(sink_ref, q_ref, k_ref, v_ref, o_ref, k2_scr, vlo_scr, vhi_scr, *, group):
    seq = q_ref.shape[0]
    blk = SWA_WINDOW
    g = pl.program_id(1)
    lane_s = lax.broadcasted_iota(jnp.int32, (seq, LANES), 1)
    hi_half = lane_s >= SWA_HEAD_DIM
    own_half = (lane_s // SWA_HEAD_DIM) == (g % 2)
    kp = k_ref[...].astype(F32)
    vp = v_ref[...].astype(F32)
    k2_scr[...] = jnp.where(own_half, kp, pltpu.roll(kp, SWA_HEAD_DIM, 1)).astype(BF16)
    v2 = jnp.where(own_half, vp, pltpu.roll(vp, SWA_HEAD_DIM, 1))
    vlo_scr[...] = jnp.where(hi_half, 0.0, v2).astype(BF16)
    vhi_scr[...] = jnp.where(hi_half, v2, 0.0).astype(BF16)

    lane_q = lax.broadcasted_iota(jnp.int32, (blk, LANES), 1) >= SWA_HEAD_DIM
    r = lax.broadcasted_iota(jnp.int32, (blk, 2 * blk), 0)
    c = lax.broadcasted_iota(jnp.int32, (blk, 2 * blk), 1)

    def q_block(n, _):
        q0 = pl.multiple_of(n * blk, blk)
        ks = pl.multiple_of(jnp.maximum(n - 1, 0) * blk, blk)
        rel = (q0 + r) - (ks + c)
        valid = (rel >= 0) & (rel < SWA_WINDOW)
        k2 = k2_scr[pl.ds(ks, 2 * blk), :]
        vlo = vlo_scr[pl.ds(ks, 2 * blk), :]
        vhi = vhi_scr[pl.ds(ks, 2 * blk), :]
        for pair in range(group // 2):
            qp = q_ref[pl.ds(q0, blk), pair * LANES:(pair + 1) * LANES].astype(F32)
            out = jnp.zeros((blk, LANES), F32)
            for e in range(2):
                sink = sink_ref[g * group + 2 * pair + e]
                qm = jnp.where(lane_q if e else jnp.logical_not(lane_q), qp, 0.0).astype(BF16)
                s = jnp.where(valid, _dot_nt(qm, k2), MASK_VALUE)
                m = jnp.maximum(jnp.max(s, axis=-1, keepdims=True), sink)
                p = jnp.exp(s - m)
                denom = jnp.sum(p, axis=-1, keepdims=True) + jnp.exp(sink - m)
                out = out + _dot(p.astype(BF16), vhi if e else vlo) / denom
            o_ref[pl.ds(q0, blk), pair * LANES:(pair + 1) * LANES] = out.astype(BF16)
        return 0

    lax.fori_loop(0, seq // blk, q_block, 0)


def _swa_attn(q, kv, sinks):
    batch, seq, nq = q.shape
    kvw = kv.shape[2]
    kv_heads = kvw // (2 * SWA_HEAD_DIM)
    q_heads = nq // SWA_HEAD_DIM
    group = q_heads // kv_heads
    gw = group * SWA_HEAD_DIM
    assert group % 2 == 0 and gw % LANES == 0 and kv_heads % 2 == 0
    v_off = kv_heads // 2
    return pl.pallas_call(
        functools.partial(_swa_attn_kernel, group=group),
        grid=(batch, kv_heads),
        in_specs=[
            pl.BlockSpec(memory_space=pltpu.SMEM),
            pl.BlockSpec((None, seq, gw), lambda b, g: (b, 0, g)),
            pl.BlockSpec((None, seq, LANES), lambda b, g: (b, 0, g // 2)),
            pl.BlockSpec((None, seq, LANES), lambda b, g: (b, 0, v_off + g // 2)),
        ],
        out_specs=pl.BlockSpec((None, seq, gw), lambda b, g: (b, 0, g)),
        out_shape=jax.ShapeDtypeStruct((batch, seq, nq), BF16),
        scratch_shapes=[pltpu.VMEM((seq, LANES), BF16)] * 3,
        compiler_params=_params(("parallel", "arbitrary")),
        name="swa_attn",
    )(sinks.astype(F32), q, kv, kv)


def _proj_ln_kernel(*refs, alpha, rows_per_batch, emit_next, sub):
    if emit_next:
        a_ref, w_ref, x_ref, g_ref, lg_ref, lb_ref, sc_ref, sh_ref, xo_ref, ho_ref = refs
    else:
        a_ref, w_ref, x_ref, g_ref, lg_ref, lb_ref, xo_ref = refs
    tm = a_ref.shape[0]
    b = (pl.program_id(0) * tm) // rows_per_batch
    gate = 1.0 + g_ref[pl.ds(b, 1), :]
    ln_g = lg_ref[...]
    ln_b = lb_ref[...]
    if emit_next:
        nsc = 1.0 + sc_ref[pl.ds(b, 1), :]
        nsh = sh_ref[pl.ds(b, 1), :]
    for c in range(tm // sub):
        sl = slice(c * sub, (c + 1) * sub)
        y = _dot(a_ref[sl, :], w_ref[...])
        z = alpha * x_ref[sl, :] + gate * y
        mu = jnp.mean(z, axis=-1, keepdims=True)
        zc = z - mu
        var = jnp.mean(zc * zc, axis=-1, keepdims=True)
        xn = zc * lax.rsqrt(var + LN_EPS) * ln_g + ln_b
        xo_ref[sl, :] = xn
        if emit_next:
            ho_ref[sl, :] = (xn * nsc + nsh).astype(BF16)


def _proj_ln(a, w, x, gate, ln_g, ln_b, next_sc, next_sh, *, alpha, tm):
    batch, seq, d = x.shape
    k = a.shape[-1]
    rows = batch * seq
    tm = min(tm, seq)
    emit_next = next_sc is not None
    kern = functools.partial(_proj_ln_kernel, alpha=alpha, rows_per_batch=seq, emit_next=emit_next,
                             sub=min(128, tm))
    row_spec = lambda width: pl.BlockSpec((tm, width), lambda i: (i, 0))
    const = lambda shape: pl.BlockSpec(shape, lambda i: (0,) * len(shape), pipeline_mode=pl.Buffered(1))
    in_specs = [row_spec(k), const((k, d)), row_spec(d), const((batch, d)), const((1, d)), const((1, d))]
    args = [a.reshape(rows, k), w, x.reshape(rows, d), gate, ln_g.reshape(1, d), ln_b.reshape(1, d)]
    out_specs = [row_spec(d)]
    out_shape = [jax.ShapeDtypeStruct((rows, d), F32)]
    if emit_next:
        in_specs += [const((batch, d))] * 2
        args += [next_sc, next_sh]
        out_specs.append(row_spec(d))
        out_shape.append(jax.ShapeDtypeStruct((rows, d), BF16))
    outs = pl.pallas_call(
        kern,
        grid=(rows // tm,),
        in_specs=in_specs,
        out_specs=out_specs,
        out_shape=out_shape,
        compiler_params=_params(("parallel",)),
        name="proj_ln",
    )(*args)
    x_new = outs[0].reshape(batch, seq, d)
    return x_new, (outs[1].reshape(batch, seq, d) if emit_next else None)


def _ffn_up_kernel(h_ref, wg_ref, wv_ref, cwg_ref, cwv_ref, cbg_ref, cbv_ref, o_ref, *, chunk):
    seq = h_ref.shape[0]
    tn = wg_ref.shape[1]
    wg = wg_ref[...]
    wv = wv_ref[...]

    def conv(u, halo, cw_ref, cb_ref):
        ext = jnp.concatenate([halo, u], axis=0)
        u1 = pltpu.roll(ext, 1, 0)[SUBLANES:]
        u2 = pltpu.roll(ext, 2, 0)[SUBLANES:]
        return u2 * cw_ref[0:1, :] + u1 * cw_ref[1:2, :] + u * cw_ref[2:3, :] + cb_ref[...]

    halo_g = jnp.zeros((SUBLANES, tn), F32)
    halo_v = jnp.zeros((SUBLANES, tn), F32)
    for c in range(seq // chunk):
        sl = slice(c * chunk, (c + 1) * chunk)
        hc = h_ref[sl, :]
        ug = _dot(hc, wg)
        uv = _dot(hc, wv)
        cg = conv(ug, halo_g, cwg_ref, cbg_ref)
        cv = conv(uv, halo_v, cwv_ref, cbv_ref)
        halo_g = ug[chunk - SUBLANES:]
        halo_v = uv[chunk - SUBLANES:]
        o_ref[sl, :] = (cg * jax.nn.sigmoid(cg) * cv).astype(BF16)


def _ffn_up(h, w_up, conv_w, conv_b):
    batch, seq, d = h.shape
    d_ff = w_up.shape[1] // 2
    tn = 512 if d_ff % 512 == 0 else LANES
    nt = d_ff // tn
    chunk = min(512, seq)
    lo = lambda b, j: (0, j)
    hi = lambda b, j: (0, nt + j)
    return pl.pallas_call(
        functools.partial(_ffn_up_kernel, chunk=chunk),
        grid=(batch, nt),
        in_specs=[
            pl.BlockSpec((None, seq, d), lambda b, j: (b, 0, 0)),
            pl.BlockSpec((d, tn), lo), pl.BlockSpec((d, tn), hi),
            pl.BlockSpec((CONV_WIDTH, tn), lo), pl.BlockSpec((CONV_WIDTH, tn), hi),
            pl.BlockSpec((1, tn), lo), pl.BlockSpec((1, tn), hi),
        ],
        out_specs=pl.BlockSpec((None, seq, tn), lambda b, j: (b, 0, j)),
        out_shape=jax.ShapeDtypeStruct((batch, seq, d_ff), BF16),
        compiler_params=_params(("parallel", "arbitrary")),
        name="ffn_up",
    )(h, w_up, w_up, conv_w, conv_w, conv_b.reshape(1, -1), conv_b.reshape(1, -1))


def kernel(x, c, positions, fox_w_in, fox_b_f, fox_w_o, swa_w_in, swa_sinks, swa_w_o, ada_w, ada_b,
           ffn_w_up, ffn_conv_w, ffn_conv_b, ffn_w_down, ln_mix_g, ln_mix_b, ln_ffn_g, ln_ffn_b):
    batch, seq, d = x.shape
    depth = ada_w.shape[0]
    alpha = (2.0 * depth) ** 0.25
    fox_heads = fox_b_f.shape[1]
    fox_dh = d // fox_heads

    mod = _ada(c, ada_w, ada_b)
    rope_tables = _rope_tables(positions) if depth > 1 else None

    h = None
    for i in range(depth):
        m = mod[i]
        j = i // 2
        if i % 2 == 0:
            w_in = fox_w_in[j]
            w_qkv = w_in[:, :3 * d].astype(BF16)
            w_f = jnp.pad(w_in[:, 3 * d:], ((0, 0), (0, LANES - fox_heads))).astype(BF16)
            b_f = jnp.pad(fox_b_f[j], (0, LANES - fox_heads)).reshape(1, LANES)
            if h is None:
                qkv, fl = _inproj_fox(x, m[SC1], m[SH1], w_qkv, w_f, b_f, head_dim=fox_dh)
            else:
                qkv, fl = _inproj_fox(h, None, None, w_qkv, w_f, b_f, head_dim=fox_dh)
            ccol, crow = _fox_cum(fl, fox_heads)
            o = _fox_attn(qkv, crow, ccol, heads=fox_heads)
            w_o = fox_w_o[j].astype(BF16)
        else:
            if h is None:
                raise NotImplementedError("SWA as the first layer")
            w_in = swa_w_in[j]
            nq = swa_sinks.shape[1] * SWA_HEAD_DIM
            q, kv = _inproj_swa(h, w_in[:, :nq].astype(BF16), w_in[:, nq:].astype(BF16), rope_tables)
            o = _swa_attn(q, kv, swa_sinks[j])
            w_o = swa_w_o[j].astype(BF16)
        x, h2 = _proj_ln(o, w_o, x, m[G1], ln_mix_g[i], ln_mix_b[i], m[SC2], m[SH2], alpha=alpha, tm=512)
        hmid = _ffn_up(h2, ffn_w_up[i].astype(BF16), ffn_conv_w[i], ffn_conv_b[i])
        if i + 1 < depth:
            nxt = mod[i + 1]
            x, h = _proj_ln(hmid, ffn_w_down[i].astype(BF16), x, m[G2], ln_ffn_g[i], ln_ffn_b[i],
                            nxt[SC1], nxt[SH1], alpha=alpha, tm=256)
        else:
            x, h = _proj_ln(hmid, ffn_w_down[i].astype(BF16), x, m[G2], ln_ffn_g[i], ln_ffn_b[i],
                            None, None, alpha=alpha, tm=256)
    return x
```

```python
import functools

import jax
import jax.numpy as jnp
from jax import lax
from jax.experimental import pallas as pl
from jax.experimental.pallas import tpu as pltpu

F32 = jnp.float32
BF16 = jnp.bfloat16

LANES = 128
SUBLANES = 8
VMEM_LIMIT_BYTES = 56 * 1024 * 1024

LN_EPS = 1e-5
ROPE_THETA = 500000.0
ROPE_DIM = 16
ROPE_HALF = ROPE_DIM // 2
SWA_HEAD_DIM = 64
SWA_WINDOW = 128
CONV_WIDTH = 3
MASK_VALUE = -1e30

SH1, SC1, G1, SH2, SC2, G2 = range(6)


def _params(semantics):
    return pltpu.CompilerParams(dimension_semantics=semantics, vmem_limit_bytes=VMEM_LIMIT_BYTES)


def _dot(a, b):
    return jnp.dot(a, b, preferred_element_type=F32)


def _dot_nt(a, b):
    return lax.dot_general(a, b, (((1,), (1,)), ((), ())), preferred_element_type=F32)


def _resident(block_shape, index_map):
    return pl.BlockSpec(block_shape, index_map, pipeline_mode=pl.Buffered(1))


def _mod_spec(mod, layer, chunk):
    _, _, batch, d = mod.shape
    return _resident((None, None, batch, d), lambda *_: (layer, chunk, 0, 0))


def _ada_kernel(c_ref, w_ref, b_ref, o_ref):
    c = c_ref[...]
    c_act = (c * jax.nn.sigmoid(c)).astype(BF16)
    o_ref[...] = _dot(c_act, w_ref[...].astype(BF16)) + b_ref[...]


def _ada(c, ada_w, ada_b):
    depth, d, six_d = ada_w.shape
    batch = c.shape[0]
    tn = min(1024, d)
    nt = d // tn
    return pl.pallas_call(
        _ada_kernel,
        grid=(depth, 6, nt),
        in_specs=[
            pl.BlockSpec((batch, d), lambda i, k, j: (0, 0)),
            pl.BlockSpec((None, d, tn), lambda i, k, j: (i, 0, k * nt + j)),
            pl.BlockSpec((None, 1, tn), lambda i, k, j: (i, 0, k * nt + j)),
        ],
        out_specs=pl.BlockSpec((None, None, batch, tn), lambda i, k, j: (i, k, 0, j)),
        out_shape=jax.ShapeDtypeStruct((depth, 6, batch, d), F32),
        compiler_params=_params(("parallel", "parallel", "parallel")),
        name="ada_mod",
    )(c, ada_w, ada_b.reshape(depth, 1, six_d))


def _inproj_fox_kernel(*refs, modulate, tiles_per_seq, q_tiles, q_scale):
    if modulate:
        x_ref, sc_ref, sh_ref, w_ref, wf_ref, bf_ref, qkv_ref, fl_ref, h_scr = refs
    else:
        x_ref, w_ref, wf_ref, bf_ref, qkv_ref, fl_ref = refs
    i = pl.program_id(0)
    j = pl.program_id(1)

    if modulate:
        @pl.when(j == 0)
        def _():
            b = i // tiles_per_seq
            sc = sc_ref[pl.ds(b, 1), :]
            sh = sh_ref[pl.ds(b, 1), :]
            h_scr[...] = (x_ref[...] * (1.0 + sc) + sh).astype(BF16)
        h_ref = h_scr
    else:
        h_ref = x_ref

    @pl.when(j == 0)
    def _():
        fl_ref[...] = _dot(h_ref[...], wf_ref[...]) + bf_ref[...]

    acc = _dot(h_ref[...], w_ref[...].astype(BF16))
    acc = acc * jnp.where(j < q_tiles, q_scale, 1.0)
    for hh in range(acc.shape[1] // LANES):
        qkv_ref[hh] = acc[:, hh * LANES:(hh + 1) * LANES].astype(BF16)


def _inproj_fox(x_or_h, mod, layer, fox_w_in, fox_layer, w_f, b_f, *, head_dim):
    batch, seq, d = x_or_h.shape
    modulate = mod is not None
    n_out = 3 * d
    assert head_dim == LANES
    tn = min(512, d)
    tm = min(1024 if modulate else 2048, seq)
    nm = seq // tm
    slots = n_out // LANES
    kern = functools.partial(_inproj_fox_kernel, modulate=modulate, tiles_per_seq=nm,
                             q_tiles=d // tn, q_scale=float(head_dim) ** -0.5)
    in_specs = [pl.BlockSpec((None, tm, d), lambda i, j: (i // nm, i % nm, 0))]
    args = [x_or_h]
    if modulate:
        in_specs += [_mod_spec(mod, layer, SC1), _mod_spec(mod, layer, SH1)]
        args += [mod, mod]
    in_specs += [
        pl.BlockSpec((None, d, tn), lambda i, j: (fox_layer, 0, j)),
        _resident((d, LANES), lambda i, j: (0, 0)),
        _resident((1, LANES), lambda i, j: (0, 0)),
    ]
    args += [fox_w_in, w_f, b_f]
    return pl.pallas_call(
        kern,
        grid=(batch * nm, n_out // tn),
        in_specs=in_specs,
        out_specs=[
            pl.BlockSpec((None, tn // LANES, tm, LANES), lambda i, j: (i // nm, j, i % nm, 0)),
            pl.BlockSpec((None, tm, LANES), lambda i, j: (i // nm, i % nm, 0)),
        ],
        out_shape=[
            jax.ShapeDtypeStruct((batch, slots, seq, LANES), BF16),
            jax.ShapeDtypeStruct((batch, seq, LANES), F32),
        ],
        scratch_shapes=[pltpu.VMEM((tm, d), BF16)] if modulate else [],
        compiler_params=_params(("parallel", "arbitrary")),
        name="fox_inproj",
    )(*args)


def _fox_cum_kernel(fl_ref, col_ref, row_ref, *, heads):
    seq = fl_ref.shape[0]
    x = fl_ref[...]
    log_f = jnp.minimum(x, 0.0) - jnp.log1p(jnp.exp(-jnp.abs(x)))
    r = lax.broadcasted_iota(jnp.int32, (LANES, LANES), 0)
    c = lax.broadcasted_iota(jnp.int32, (LANES, LANES), 1)
    tri = (c <= r).astype(BF16)
    carry = jnp.zeros((1, LANES), F32)
    for blk in range(seq // LANES):
        xb = log_f[blk * LANES:(blk + 1) * LANES]
        hi = xb.astype(BF16)
        rem = xb - hi.astype(F32)
        mid = rem.astype(BF16)
        lo = (rem - mid.astype(F32)).astype(BF16)
        cb = _dot(tri, hi) + _dot(tri, mid) + _dot(tri, lo) + carry
        col_ref[blk * LANES:(blk + 1) * LANES, :] = cb
        carry = cb[LANES - 1:LANES, :]
    row_ref[...] = col_ref[...].T[:heads]


def _fox_cum(fl, heads):
    batch, seq, _ = fl.shape
    return pl.pallas_call(
        functools.partial(_fox_cum_kernel, heads=heads),
        grid=(batch,),
        in_specs=[pl.BlockSpec((None, seq, LANES), lambda b: (b, 0, 0))],
        out_specs=[
            pl.BlockSpec((None, seq, LANES), lambda b: (b, 0, 0)),
            pl.BlockSpec((None, heads, seq), lambda b: (b, 0, 0)),
        ],
        out_shape=[
            jax.ShapeDtypeStruct((batch, seq, LANES), F32),
            jax.ShapeDtypeStruct((batch, heads, seq), F32),
        ],
        compiler_params=_params(("parallel",)),
        name="fox_cum",
    )(fl)


def _fox_attn_kernel(q_ref, k_ref, v_ref, crow_ref, ccol_ref, o_ref, *, heads_per_step, blk):
    seq = q_ref.shape[1]
    hb = pl.program_id(1)
    lane = lax.broadcasted_iota(jnp.int32, (blk, LANES), 1)
    rows = lax.broadcasted_iota(jnp.int32, (blk, blk), 0)
    cols = lax.broadcasted_iota(jnp.int32, (blk, blk), 1)
    causal = cols <= rows

    for hh in range(heads_per_step):
        h = hb * heads_per_step + hh
        for qi in range(seq // blk):
            q0 = qi * blk
            lk = q0 + blk
            q = q_ref[hh, q0:lk, :]
            s = _dot_nt(q, k_ref[hh, :lk, :]) - crow_ref[pl.ds(h, 1), :lk]
            diag = jnp.where(causal, s[:, q0:], MASK_VALUE)
            s = diag if qi == 0 else jnp.concatenate([s[:, :q0], diag], axis=1)
            cq = jnp.sum(jnp.where(lane == h, ccol_ref[q0:lk, :], 0.0), axis=-1, keepdims=True)
            m = jnp.max(s, axis=-1, keepdims=True) + cq
            p = jnp.exp(s + (cq - m))
            l = jnp.sum(p, axis=-1, keepdims=True)
            acc = _dot(p.astype(BF16), v_ref[hh, :lk, :])
            o_ref[q0:lk, hh * LANES:(hh + 1) * LANES] = (acc / l).astype(BF16)


def _fox_attn(qkv, crow, ccol, *, heads):
    batch, slots, seq, dh = qkv.shape
    assert slots == 3 * heads and dh == LANES
    hps = 2 if heads % 2 == 0 else 1
    groups = heads // hps
    blk = min(256, seq)
    kern = functools.partial(_fox_attn_kernel, heads_per_step=hps, blk=blk)
    head_spec = lambda off: pl.BlockSpec((None, hps, seq, dh), lambda b, g: (b, off + g, 0, 0))
    return pl.pallas_call(
        kern,
        grid=(batch, groups),
        in_specs=[
            head_spec(0), head_spec(groups), head_spec(2 * groups),
            pl.BlockSpec((None, heads, seq), lambda b, g: (b, 0, 0)),
            pl.BlockSpec((None, seq, LANES), lambda b, g: (b, 0, 0)),
        ],
        out_specs=pl.BlockSpec((None, seq, hps * dh), lambda b, g: (b, 0, g)),
        out_shape=jax.ShapeDtypeStruct((batch, seq, heads * dh), BF16),
        compiler_params=_params(("parallel", "parallel")),
        name="fox_attn",
    )(qkv, qkv, qkv, crow, ccol)


def _rope_table_kernel(pos_ref, inv_ref, o_ref):
    pos = pos_ref[...].astype(F32)
    ang = pos * inv_ref[...]
    cos = jnp.cos(ang)
    sin = jnp.sin(ang)
    idx = lax.broadcasted_iota(jnp.int32, ang.shape, 1) % SWA_HEAD_DIM
    o_ref[0] = jnp.where(idx < ROPE_DIM, cos, 1.0)
    o_ref[1] = jnp.where((idx >= ROPE_HALF) & (idx < ROPE_DIM), sin, 0.0)
    o_ref[2] = jnp.where(idx < ROPE_HALF, -sin, 0.0)


def _rope_tables(positions):
    batch, seq = positions.shape
    inv_freq = ROPE_THETA ** (-jnp.arange(0, ROPE_DIM, 2, dtype=F32) / ROPE_DIM)
    idx = jnp.arange(LANES) % SWA_HEAD_DIM
    inv_lane = jnp.where(idx < ROPE_DIM, inv_freq[idx % ROPE_HALF], 0.0).reshape(1, LANES)
    return pl.pallas_call(
        _rope_table_kernel,
        grid=(batch,),
        in_specs=[
            pl.BlockSpec((None, seq, 1), lambda b: (b, 0, 0)),
            pl.BlockSpec((1, LANES), lambda b: (0, 0)),
        ],
        out_specs=pl.BlockSpec((3, None, seq, LANES), lambda b: (0, b, 0, 0)),
        out_shape=jax.ShapeDtypeStruct((3, batch, seq, LANES), F32),
        compiler_params=_params(("parallel",)),
        name="rope_tables",
    )(positions.reshape(batch, seq, 1), inv_lane)


def _inproj_swa_kernel(h_ref, wq_ref, wkv_ref, rt_ref, q_ref, kv_ref, *, q_tiles, q_scale, k_width):
    j = pl.program_id(1)
    cos = rt_ref[0]
    sin_lo = rt_ref[1]
    sin_hi = rt_ref[2]

    def rope(a):
        return (a * cos + pltpu.roll(a, ROPE_HALF, 1) * sin_lo
                + pltpu.roll(a, LANES - ROPE_HALF, 1) * sin_hi)

    @pl.when(j < q_tiles)
    def _():
        acc = _dot(h_ref[...], wq_ref[...].astype(BF16))
        for cc in range(acc.shape[1] // LANES):
            sl = slice(cc * LANES, (cc + 1) * LANES)
            q_ref[:, sl] = (rope(acc[:, sl]) * q_scale).astype(BF16)

    @pl.when(j == q_tiles)
    def _():
        acc = _dot(h_ref[...], wkv_ref[...].astype(BF16))
        for cc in range(acc.shape[1] // LANES):
            sl = slice(cc * LANES, (cc + 1) * LANES)
            a = acc[:, sl]
            kv_ref[:, sl] = (rope(a) if cc * LANES < k_width else a).astype(BF16)


def _inproj_swa(h, swa_w_in, swa_layer, rope_tables, *, q_heads):
    batch, seq, d = h.shape
    nq = q_heads * SWA_HEAD_DIM
    kvw = swa_w_in.shape[2] - nq
    assert (kvw // 2) % LANES == 0
    tn = min(512, nq)
    assert nq % tn == 0 and nq % kvw == 0
    q_tiles = nq // tn
    kern = functools.partial(_inproj_swa_kernel, q_tiles=q_tiles,
                             q_scale=float(SWA_HEAD_DIM) ** -0.5, k_width=kvw // 2)
    return pl.pallas_call(
        kern,
        grid=(batch, q_tiles + 1),
        in_specs=[
            pl.BlockSpec((None, seq, d), lambda b, j: (b, 0, 0)),
            pl.BlockSpec((None, d, tn), lambda b, j: (swa_layer, 0, jnp.minimum(j, q_tiles - 1))),
            _resident((None, d, kvw), lambda b, j: (swa_layer, 0, nq // kvw)),
            pl.BlockSpec((3, None, seq, LANES), lambda b, j: (0, b, 0, 0)),
        ],
        out_specs=[
            pl.BlockSpec((None, seq, tn), lambda b, j: (b, 0, jnp.minimum(j, q_tiles - 1))),
            pl.BlockSpec((None, seq, kvw), lambda b, j: (b, 0, 0)),
        ],
        out_shape=[
            jax.ShapeDtypeStruct((batch, seq, nq), BF16),
            jax.ShapeDtypeStruct((batch, seq, kvw), BF16),
        ],
        compiler_params=_params(("parallel", "arbitrary")),
        name="swa_inproj",
    )(h, swa_w_in, swa_w_in, rope_tables)


def _swa_attn_kernel(sink_ref, q_ref, k_ref, v_ref, o_ref, k2_scr, vlo_scr, vhi_scr, *, group, layer):
    seq = q_ref.shape[0]
    blk = SWA_WINDOW
    g = pl.program_id(1)
    lane_s = lax.broadcasted_iota(jnp.int32, (seq, LANES), 1)
    hi_half = lane_s >= SWA_HEAD_DIM
    own_half = (lane_s // SWA_HEAD_DIM) == (g % 2)
    kp = k_ref[...].astype(F32)
    vp = v_ref[...].astype(F32)
    k2_scr[...] = jnp.where(own_half, kp, pltpu.roll(kp, SWA_HEAD_DIM, 1)).astype(BF16)
    v2 = jnp.where(own_half, vp, pltpu.roll(vp, SWA_HEAD_DIM, 1))
    vlo_scr[...] = jnp.where(hi_half, 0.0, v2).astype(BF16)
    vhi_scr[...] = jnp.where(hi_half, v2, 0.0).astype(BF16)

    lane_q = lax.broadcasted_iota(jnp.int32, (blk, LANES), 1) >= SWA_HEAD_DIM
    r = lax.broadcasted_iota(jnp.int32, (blk, 2 * blk), 0)
    c = lax.broadcasted_iota(jnp.int32, (blk, 2 * blk), 1)

    def q_block(n, _):
        q0 = pl.multiple_of(n * blk, blk)
        ks = pl.multiple_of(jnp.maximum(n - 1, 0) * blk, blk)
        rel = (q0 + r) - (ks + c)
        valid = (rel >= 0) & (rel < SWA_WINDOW)
        k2 = k2_scr[pl.ds(ks, 2 * blk), :]
        vlo = vlo_scr[pl.ds(ks, 2 * blk), :]
        vhi = vhi_scr[pl.ds(ks, 2 * blk), :]
        for pair in range(group // 2):
            qp = q_ref[pl.ds(q0, blk), pair * LANES:(pair + 1) * LANES].astype(F32)
            out = jnp.zeros((blk, LANES), F32)
            for e in range(2):
                sink = sink_ref[layer, g * group + 2 * pair + e]
                qm = jnp.where(lane_q if e else jnp.logical_not(lane_q), qp, 0.0).astype(BF16)
                s = jnp.where(valid, _dot_nt(qm, k2), MASK_VALUE)
                m = jnp.maximum(jnp.max(s, axis=-1, keepdims=True), sink)
                p = jnp.exp(s - m)
                denom = jnp.sum(p, axis=-1, keepdims=True) + jnp.exp(sink - m)
                out = out + _dot(p.astype(BF16), vhi if e else vlo) / denom
            o_ref[pl.ds(q0, blk), pair * LANES:(pair + 1) * LANES] = out.astype(BF16)
        return 0

    lax.fori_loop(0, seq // blk, q_block, 0)


def _swa_attn(q, kv, swa_sinks, swa_layer):
    batch, seq, nq = q.shape
    kvw = kv.shape[2]
    kv_heads = kvw // (2 * SWA_HEAD_DIM)
    q_heads = nq // SWA_HEAD_DIM
    group = q_heads // kv_heads
    gw = group * SWA_HEAD_DIM
    assert group % 2 == 0 and gw % LANES == 0 and kv_heads % 2 == 0
    v_off = kv_heads // 2
    return pl.pallas_call(
        functools.partial(_swa_attn_kernel, group=group, layer=swa_layer),
        grid=(batch, kv_heads),
        in_specs=[
            pl.BlockSpec(memory_space=pltpu.SMEM),
            pl.BlockSpec((None, seq, gw), lambda b, g: (b, 0, g)),
            pl.BlockSpec((None, seq, LANES), lambda b, g: (b, 0, g // 2)),
            pl.BlockSpec((None, seq, LANES), lambda b, g: (b, 0, v_off + g // 2)),
        ],
        out_specs=pl.BlockSpec((None, seq, gw), lambda b, g: (b, 0, g)),
        out_shape=jax.ShapeDtypeStruct((batch, seq, nq), BF16),
        scratch_shapes=[pltpu.VMEM((seq, LANES), BF16)] * 3,
        compiler_params=_params(("parallel", "arbitrary")),
        name="swa_attn",
    )(swa_sinks, q, kv, kv)


def _proj_ln_kernel(*refs, alpha, rows_per_batch, emit_next, cast_w, sub):
    refs = list(refs)
    w_scr = refs.pop() if cast_w else None
    if emit_next:
        a_ref, w_ref, x_ref, g_ref, lg_ref, lb_ref, sc_ref, sh_ref, xo_ref, ho_ref = refs
    else:
        a_ref, w_ref, x_ref, g_ref, lg_ref, lb_ref, xo_ref = refs
    if cast_w:
        @pl.when(pl.program_id(0) == 0)
        def _():
            w_scr[...] = w_ref[...].astype(BF16)
        w_ref = w_scr
    tm = a_ref.shape[0]
    b = (pl.program_id(0) * tm) // rows_per_batch
    gate = 1.0 + g_ref[pl.ds(b, 1), :]
    ln_g = lg_ref[...]
    ln_b = lb_ref[...]
    if emit_next:
        nsc = 1.0 + sc_ref[pl.ds(b, 1), :]
        nsh = sh_ref[pl.ds(b, 1), :]
    for c in range(tm // sub):
        sl = slice(c * sub, (c + 1) * sub)
        y = _dot(a_ref[sl, :], w_ref[...])
        z = alpha * x_ref[sl, :] + gate * y
        mu = jnp.mean(z, axis=-1, keepdims=True)
        zc = z - mu
        var = jnp.mean(zc * zc, axis=-1, keepdims=True)
        xn = zc * lax.rsqrt(var + LN_EPS) * ln_g + ln_b
        xo_ref[sl, :] = xn
        if emit_next:
            ho_ref[sl, :] = (xn * nsc + nsh).astype(BF16)


def _proj_ln(a, w, w_layer, x, mod, layer, gate_chunk, ln_g, ln_b, next_mod, *, alpha, tm):
    batch, seq, d = x.shape
    k = a.shape[-1]
    rows = batch * seq
    tm = min(tm, seq)
    emit_next = next_mod is not None
    cast_w = w.dtype != BF16
    kern = functools.partial(_proj_ln_kernel, alpha=alpha, rows_per_batch=seq, emit_next=emit_next,
                             cast_w=cast_w, sub=min(128, tm))
    row_spec = lambda width: pl.BlockSpec((tm, width), lambda i: (i, 0))
    vec_spec = _resident((None, 1, d), lambda i: (layer, 0, 0))
    in_specs = [row_spec(k), _resident((None, k, d), lambda i: (w_layer, 0, 0)), row_spec(d),
                _mod_spec(mod, layer, gate_chunk), vec_spec, vec_spec]
    depth = ln_g.shape[0]
    args = [a.reshape(rows, k), w, x.reshape(rows, d), mod, ln_g.reshape(depth, 1, d), ln_b.reshape(depth, 1, d)]
    out_specs = [row_spec(d)]
    out_shape = [jax.ShapeDtypeStruct((rows, d), F32)]
    if emit_next:
        nl, nsc, nsh = next_mod
        in_specs += [_mod_spec(mod, nl, nsc), _mod_spec(mod, nl, nsh)]
        args += [mod, mod]
        out_specs.append(row_spec(d))
        out_shape.append(jax.ShapeDtypeStruct((rows, d), BF16))
    outs = pl.pallas_call(
        kern,
        grid=(rows // tm,),
        in_specs=in_specs,
        out_specs=out_specs,
        out_shape=out_shape,
        scratch_shapes=[pltpu.VMEM((k, d), BF16)] if cast_w else [],
        compiler_params=_params(("arbitrary",)),
        name="proj_ln",
    )(*args)
    x_new = outs[0].reshape(batch, seq, d)
    return x_new, (outs[1].reshape(batch, seq, d) if emit_next else None)


def _ffn_up_kernel(h_ref, wg_ref, wv_ref, cwg_ref, cwv_ref, cbg_ref, cbv_ref, o_ref, *, chunk):
    seq = h_ref.shape[0]
    tn = wg_ref.shape[1]
    wg = wg_ref[...].astype(BF16)
    wv = wv_ref[...].astype(BF16)

    def conv(u, halo, cw_ref, cb_ref):
        ext = jnp.concatenate([halo, u], axis=0)
        u1 = pltpu.roll(ext, 1, 0)[SUBLANES:]
        u2 = pltpu.roll(ext, 2, 0)[SUBLANES:]
        return u2 * cw_ref[0:1, :] + u1 * cw_ref[1:2, :] + u * cw_ref[2:3, :] + cb_ref[...]

    halo_g = jnp.zeros((SUBLANES, tn), F32)
    halo_v = jnp.zeros((SUBLANES, tn), F32)
    for c in range(seq // chunk):
        sl = slice(c * chunk, (c + 1) * chunk)
        hc = h_ref[sl, :]
        ug = _dot(hc, wg)
        uv = _dot(hc, wv)
        cg = conv(ug, halo_g, cwg_ref, cbg_ref)
        cv = conv(uv, halo_v, cwv_ref, cbv_ref)
        halo_g = ug[chunk - SUBLANES:]
        halo_v = uv[chunk - SUBLANES:]
        o_ref[sl, :] = (cg * jax.nn.sigmoid(cg) * cv).astype(BF16)


def _ffn_up(h, ffn_w_up, conv_w, conv_b, layer):
    batch, seq, d = h.shape
    depth, _, two_f = ffn_w_up.shape
    d_ff = two_f // 2
    tn = 512 if d_ff % 512 == 0 else LANES
    nt = d_ff // tn
    chunk = min(512, seq)
    lo = lambda b, j: (layer, 0, j)
    hi = lambda b, j: (layer, 0, nt + j)
    conv_b = conv_b.reshape(depth, 1, two_f)
    return pl.pallas_call(
        functools.partial(_ffn_up_kernel, chunk=chunk),
        grid=(batch, nt),
        in_specs=[
            pl.BlockSpec((None, seq, d), lambda b, j: (b, 0, 0)),
            pl.BlockSpec((None, d, tn), lo), pl.BlockSpec((None, d, tn), hi),
            pl.BlockSpec((None, CONV_WIDTH, tn), lo), pl.BlockSpec((None, CONV_WIDTH, tn), hi),
            pl.BlockSpec((None, 1, tn), lo), pl.BlockSpec((None, 1, tn), hi),
        ],
        out_specs=pl.BlockSpec((None, seq, tn), lambda b, j: (b, 0, j)),
        out_shape=jax.ShapeDtypeStruct((batch, seq, d_ff), BF16),
        compiler_params=_params(("parallel", "arbitrary")),
        name="ffn_up",
    )(h, ffn_w_up, ffn_w_up, conv_w, conv_w, conv_b, conv_b)


def kernel(x, c, positions, fox_w_in, fox_b_f, fox_w_o, swa_w_in, swa_sinks, swa_w_o, ada_w, ada_b,
           ffn_w_up, ffn_conv_w, ffn_conv_b, ffn_w_down, ln_mix_g, ln_mix_b, ln_ffn_g, ln_ffn_b):
    batch, seq, d = x.shape
    depth = ada_w.shape[0]
    alpha = (2.0 * depth) ** 0.25
    fox_heads = fox_b_f.shape[1]
    fox_dh = d // fox_heads

    mod = _ada(c, ada_w, ada_b)
    rope_tables = _rope_tables(positions) if depth > 1 else None
    w_down = ffn_w_down.astype(BF16)

    h = None
    for i in range(depth):
        j = i // 2
        if i % 2 == 0:
            w_f = jnp.pad(fox_w_in[j, :, 3 * d:], ((0, 0), (0, LANES - fox_heads))).astype(BF16)
            b_f = jnp.pad(fox_b_f[j], (0, LANES - fox_heads)).reshape(1, LANES)
            if h is None:
                qkv, fl = _inproj_fox(x, mod, i, fox_w_in, j, w_f, b_f, head_dim=fox_dh)
            else:
                qkv, fl = _inproj_fox(h, None, i, fox_w_in, j, w_f, b_f, head_dim=fox_dh)
            ccol, crow = _fox_cum(fl, fox_heads)
            o = _fox_attn(qkv, crow, ccol, heads=fox_heads)
            w_o = fox_w_o
        else:
            if h is None:
                raise NotImplementedError("SWA as the first layer")
            q, kv = _inproj_swa(h, swa_w_in, j, rope_tables, q_heads=swa_sinks.shape[1])
            o = _swa_attn(q, kv, swa_sinks, j)
            w_o = swa_w_o
        x, h2 = _proj_ln(o, w_o, j, x, mod, i, G1, ln_mix_g, ln_mix_b, (i, SC2, SH2), alpha=alpha, tm=512)
        hmid = _ffn_up(h2, ffn_w_up, ffn_conv_w, ffn_conv_b, i)
        next_mod = (i + 1, SC1, SH1) if i + 1 < depth else None
        x, h = _proj_ln(hmid, w_down, i, x, mod, i, G2, ln_ffn_g, ln_ffn_b, next_mod, alpha=alpha, tm=256)
    return x
```

```python
import functools

import jax
import jax.numpy as jnp
from jax import lax
from jax.experimental import pallas as pl
from jax.experimental.pallas import tpu as pltpu

F32 = jnp.float32
BF16 = jnp.bfloat16

LANES = 128
SUBLANES = 8
VMEM_LIMIT_BYTES = 56 * 1024 * 1024

LN_EPS = 1e-5
ROPE_THETA = 500000.0
ROPE_DIM = 16
ROPE_HALF = ROPE_DIM // 2
SWA_HEAD_DIM = 64
SWA_WINDOW = 128
CONV_WIDTH = 3
MASK_VALUE = -1e30

SH1, SC1, G1, SH2, SC2, G2 = range(6)


def _params(semantics):
    return pltpu.CompilerParams(dimension_semantics=semantics, vmem_limit_bytes=VMEM_LIMIT_BYTES)


def _dot(a, b):
    return jnp.dot(a, b, preferred_element_type=F32)


def _dot_nt(a, b):
    return lax.dot_general(a, b, (((1,), (1,)), ((), ())), preferred_element_type=F32)


def _resident(block_shape, index_map):
    return pl.BlockSpec(block_shape, index_map, pipeline_mode=pl.Buffered(1))


def _mod_spec(mod, layer, chunk):
    _, _, batch, d = mod.shape
    return _resident((None, None, batch, d), lambda *_: (layer, chunk, 0, 0))


def _ada_kernel(c_ref, w_ref, b_ref, o_ref):
    c = c_ref[...]
    c_act = (c * jax.nn.sigmoid(c)).astype(BF16)
    o_ref[...] = _dot(c_act, w_ref[...].astype(BF16)) + b_ref[...]


def _ada(c, ada_w, ada_b):
    depth, d, six_d = ada_w.shape
    batch = c.shape[0]
    tn = min(1024, d)
    nt = d // tn
    return pl.pallas_call(
        _ada_kernel,
        grid=(depth, 6, nt),
        in_specs=[
            pl.BlockSpec((batch, d), lambda i, k, j: (0, 0)),
            pl.BlockSpec((None, d, tn), lambda i, k, j: (i, 0, k * nt + j)),
            pl.BlockSpec((None, 1, tn), lambda i, k, j: (i, 0, k * nt + j)),
        ],
        out_specs=pl.BlockSpec((None, None, batch, tn), lambda i, k, j: (i, k, 0, j)),
        out_shape=jax.ShapeDtypeStruct((depth, 6, batch, d), F32),
        compiler_params=_params(("parallel", "parallel", "parallel")),
        name="ada_mod",
    )(c, ada_w, ada_b.reshape(depth, 1, six_d))


def _inproj_fox_kernel(*refs, modulate, tiles_per_seq, q_tiles, q_scale):
    if modulate:
        x_ref, sc_ref, sh_ref, w_ref, wf_ref, bf_ref, qkv_ref, fl_ref, h_scr = refs
    else:
        x_ref, w_ref, wf_ref, bf_ref, qkv_ref, fl_ref = refs
    i = pl.program_id(0)
    j = pl.program_id(1)

    if modulate:
        @pl.when(j == 0)
        def _():
            b = i // tiles_per_seq
            sc = sc_ref[pl.ds(b, 1), :]
            sh = sh_ref[pl.ds(b, 1), :]
            h_scr[...] = (x_ref[...] * (1.0 + sc) + sh).astype(BF16)
        h_ref = h_scr
    else:
        h_ref = x_ref

    @pl.when(j == 0)
    def _():
        fl_ref[...] = _dot(h_ref[...], wf_ref[...]) + bf_ref[...]

    acc = _dot(h_ref[...], w_ref[...].astype(BF16))
    acc = acc * jnp.where(j < q_tiles, q_scale, 1.0)
    for hh in range(acc.shape[1] // LANES):
        qkv_ref[hh] = acc[:, hh * LANES:(hh + 1) * LANES].astype(BF16)


def _inproj_fox(x_or_h, mod, layer, fox_w_in, fox_layer, w_f, b_f, *, head_dim):
    batch, seq, d = x_or_h.shape
    modulate = mod is not None
    n_out = 3 * d
    assert head_dim == LANES
    tn = min(1024, d)
    tm = min(1024 if modulate else 2048, seq)
    nm = seq // tm
    slots = n_out // LANES
    kern = functools.partial(_inproj_fox_kernel, modulate=modulate, tiles_per_seq=nm,
                             q_tiles=d // tn, q_scale=float(head_dim) ** -0.5)
    in_specs = [pl.BlockSpec((None, tm, d), lambda i, j: (i // nm, i % nm, 0))]
    args = [x_or_h]
    if modulate:
        in_specs += [_mod_spec(mod, layer, SC1), _mod_spec(mod, layer, SH1)]
        args += [mod, mod]
    in_specs += [
        pl.BlockSpec((None, d, tn), lambda i, j: (fox_layer, 0, j)),
        _resident((d, LANES), lambda i, j: (0, 0)),
        _resident((1, LANES), lambda i, j: (0, 0)),
    ]
    args += [fox_w_in, w_f, b_f]
    return pl.pallas_call(
        kern,
        grid=(batch * nm, n_out // tn),
        in_specs=in_specs,
        out_specs=[
            pl.BlockSpec((None, tn // LANES, tm, LANES), lambda i, j: (i // nm, j, i % nm, 0)),
            pl.BlockSpec((None, tm, LANES), lambda i, j: (i // nm, i % nm, 0)),
        ],
        out_shape=[
            jax.ShapeDtypeStruct((batch, slots, seq, LANES), BF16),
            jax.ShapeDtypeStruct((batch, seq, LANES), F32),
        ],
        scratch_shapes=[pltpu.VMEM((tm, d), BF16)] if modulate else [],
        compiler_params=_params(("parallel", "arbitrary")),
        name="fox_inproj",
    )(*args)


def _fox_cum_kernel(fl_ref, col_ref, row_ref, *, heads):
    seq = fl_ref.shape[0]
    x = fl_ref[...]
    log_f = jnp.minimum(x, 0.0) - jnp.log1p(jnp.exp(-jnp.abs(x)))
    r = lax.broadcasted_iota(jnp.int32, (LANES, LANES), 0)
    c = lax.broadcasted_iota(jnp.int32, (LANES, LANES), 1)
    tri = (c <= r).astype(BF16)
    carry = jnp.zeros((1, LANES), F32)
    for blk in range(seq // LANES):
        xb = log_f[blk * LANES:(blk + 1) * LANES]
        hi = xb.astype(BF16)
        rem = xb - hi.astype(F32)
        mid = rem.astype(BF16)
        lo = (rem - mid.astype(F32)).astype(BF16)
        cb = _dot(tri, hi) + _dot(tri, mid) + _dot(tri, lo) + carry
        col_ref[blk * LANES:(blk + 1) * LANES, :] = cb
        carry = cb[LANES - 1:LANES, :]
    row_ref[...] = col_ref[...].T[LANES - heads:]


def _fox_cum(fl, heads):
    batch, seq, _ = fl.shape
    return pl.pallas_call(
        functools.partial(_fox_cum_kernel, heads=heads),
        grid=(batch,),
        in_specs=[pl.BlockSpec((None, seq, LANES), lambda b: (b, 0, 0))],
        out_specs=[
            pl.BlockSpec((None, seq, LANES), lambda b: (b, 0, 0)),
            pl.BlockSpec((None, heads, seq), lambda b: (b, 0, 0)),
        ],
        out_shape=[
            jax.ShapeDtypeStruct((batch, seq, LANES), F32),
            jax.ShapeDtypeStruct((batch, heads, seq), F32),
        ],
        compiler_params=_params(("parallel",)),
        name="fox_cum",
    )(fl)


def _fox_attn_kernel(q_ref, k_ref, v_ref, crow_ref, ccol_ref, o_ref, *, heads_per_step, blk):
    seq = q_ref.shape[1]
    heads = crow_ref.shape[0]
    hb = pl.program_id(1)
    lane = lax.broadcasted_iota(jnp.int32, (blk, LANES), 1)
    rows = lax.broadcasted_iota(jnp.int32, (blk, blk), 0)
    cols = lax.broadcasted_iota(jnp.int32, (blk, blk), 1)
    causal = cols <= rows

    def scores(hh, qi):
        h = hb * heads_per_step + hh
        q0 = qi * blk
        lk = q0 + blk
        s = _dot_nt(q_ref[hh, q0:lk, :], k_ref[hh, :lk, :]) - crow_ref[pl.ds(h, 1), :lk]
        diag = jnp.where(causal, s[:, q0:], MASK_VALUE)
        return diag if qi == 0 else jnp.concatenate([s[:, :q0], diag], axis=1)

    def finish(hh, qi, s):
        h = hb * heads_per_step + hh
        q0 = qi * blk
        lk = q0 + blk
        cq = jnp.sum(jnp.where(lane == h + (LANES - heads), ccol_ref[q0:lk, :], 0.0),
                     axis=-1, keepdims=True)
        m = jnp.max(s, axis=-1, keepdims=True) + cq
        p = jnp.exp(s + (cq - m))
        l = jnp.sum(p, axis=-1, keepdims=True)
        acc = _dot(p.astype(BF16), v_ref[hh, :lk, :])
        o_ref[q0:lk, hh * LANES:(hh + 1) * LANES] = (acc / l).astype(BF16)

    work = [(hh, qi) for hh in range(heads_per_step) for qi in range(seq // blk)]
    s_next = scores(*work[0])
    for idx, item in enumerate(work):
        s_cur = s_next
        if idx + 1 < len(work):
            s_next = scores(*work[idx + 1])
        finish(*item, s_cur)


def _fox_attn(qkv, crow, ccol, *, heads):
    batch, slots, seq, dh = qkv.shape
    assert slots == 3 * heads and dh == LANES
    hps = 2 if heads % 2 == 0 else 1
    groups = heads // hps
    blk = min(256, seq)
    kern = functools.partial(_fox_attn_kernel, heads_per_step=hps, blk=blk)
    head_spec = lambda off: pl.BlockSpec((None, hps, seq, dh), lambda b, g: (b, off + g, 0, 0))
    return pl.pallas_call(
        kern,
        grid=(batch, groups),
        in_specs=[
            head_spec(0), head_spec(groups), head_spec(2 * groups),
            pl.BlockSpec((None, heads, seq), lambda b, g: (b, 0, 0)),
            pl.BlockSpec((None, seq, LANES), lambda b, g: (b, 0, 0)),
        ],
        out_specs=pl.BlockSpec((None, seq, hps * dh), lambda b, g: (b, 0, g)),
        out_shape=jax.ShapeDtypeStruct((batch, seq, heads * dh), BF16),
        compiler_params=_params(("parallel", "parallel")),
        name="fox_attn",
    )(qkv, qkv, qkv, crow, ccol)


def _rope_table_kernel(pos_ref, inv_ref, o_ref):
    pos = pos_ref[...].astype(F32)
    ang = pos * inv_ref[...]
    cos = jnp.cos(ang)
    sin = jnp.sin(ang)
    idx = lax.broadcasted_iota(jnp.int32, ang.shape, 1) % SWA_HEAD_DIM
    o_ref[0] = jnp.where(idx < ROPE_DIM, cos, 1.0)
    o_ref[1] = jnp.where((idx >= ROPE_HALF) & (idx < ROPE_DIM), sin, 0.0)
    o_ref[2] = jnp.where(idx < ROPE_HALF, -sin, 0.0)


def _rope_tables(positions):
    batch, seq = positions.shape
    inv_freq = ROPE_THETA ** (-jnp.arange(0, ROPE_DIM, 2, dtype=F32) / ROPE_DIM)
    idx = jnp.arange(LANES) % SWA_HEAD_DIM
    inv_lane = jnp.where(idx < ROPE_DIM, inv_freq[idx % ROPE_HALF], 0.0).reshape(1, LANES)
    return pl.pallas_call(
        _rope_table_kernel,
        grid=(batch,),
        in_specs=[
            pl.BlockSpec((None, seq, 1), lambda b: (b, 0, 0)),
            pl.BlockSpec((1, LANES), lambda b: (0, 0)),
        ],
        out_specs=pl.BlockSpec((3, None, seq, LANES), lambda b: (0, b, 0, 0)),
        out_shape=jax.ShapeDtypeStruct((3, batch, seq, LANES), F32),
        compiler_params=_params(("parallel",)),
        name="rope_tables",
    )(positions.reshape(batch, seq, 1), inv_lane)


def _inproj_swa_kernel(h_ref, wq_ref, wkv_ref, rt_ref, q_ref, kv_ref, *, q_tiles, q_scale, k_width):
    j = pl.program_id(1)
    cos = rt_ref[0]
    sin_lo = rt_ref[1]
    sin_hi = rt_ref[2]

    def rope(a):
        return (a * cos + pltpu.roll(a, ROPE_HALF, 1) * sin_lo
                + pltpu.roll(a, LANES - ROPE_HALF, 1) * sin_hi)

    @pl.when(j < q_tiles)
    def _():
        acc = _dot(h_ref[...], wq_ref[...].astype(BF16))
        for cc in range(acc.shape[1] // LANES):
            sl = slice(cc * LANES, (cc + 1) * LANES)
            q_ref[:, sl] = (rope(acc[:, sl]) * q_scale).astype(BF16)

    @pl.when(j == q_tiles)
    def _():
        acc = _dot(h_ref[...], wkv_ref[...].astype(BF16))
        for cc in range(acc.shape[1] // LANES):
            sl = slice(cc * LANES, (cc + 1) * LANES)
            a = acc[:, sl]
            kv_ref[:, sl] = (rope(a) if cc * LANES < k_width else a).astype(BF16)


def _inproj_swa(h, swa_w_in, swa_layer, rope_tables, *, q_heads):
    batch, seq, d = h.shape
    nq = q_heads * SWA_HEAD_DIM
    kvw = swa_w_in.shape[2] - nq
    assert (kvw // 2) % LANES == 0
    tn = min(512, nq)
    assert nq % tn == 0 and nq % kvw == 0
    q_tiles = nq // tn
    kern = functools.partial(_inproj_swa_kernel, q_tiles=q_tiles,
                             q_scale=float(SWA_HEAD_DIM) ** -0.5, k_width=kvw // 2)
    return pl.pallas_call(
        kern,
        grid=(batch, q_tiles + 1),
        in_specs=[
            pl.BlockSpec((None, seq, d), lambda b, j: (b, 0, 0)),
            pl.BlockSpec((None, d, tn), lambda b, j: (swa_layer, 0, jnp.minimum(j, q_tiles - 1))),
            _resident((None, d, kvw), lambda b, j: (swa_layer, 0, nq // kvw)),
            pl.BlockSpec((3, None, seq, LANES), lambda b, j: (0, b, 0, 0)),
        ],
        out_specs=[
            pl.BlockSpec((None, seq, tn), lambda b, j: (b, 0, jnp.minimum(j, q_tiles - 1))),
            pl.BlockSpec((None, seq, kvw), lambda b, j: (b, 0, 0)),
        ],
        out_shape=[
            jax.ShapeDtypeStruct((batch, seq, nq), BF16),
            jax.ShapeDtypeStruct((batch, seq, kvw), BF16),
        ],
        compiler_params=_params(("parallel", "arbitrary")),
        name="swa_inproj",
    )(h, swa_w_in, swa_w_in, rope_tables)


def _swa_attn_kernel(sink_ref, q_ref, k_ref, v_ref, o_ref, k2_scr, vlo_scr, vhi_scr, *, group, layer):
    seq = q_ref.shape[0]
    blk = SWA_WINDOW
    g = pl.program_id(1)
    lane_s = lax.broadcasted_iota(jnp.int32, (seq, LANES), 1)
    hi_half = lane_s >= SWA_HEAD_DIM
    own_half = (lane_s // SWA_HEAD_DIM) == (g % 2)
    kp = k_ref[...].astype(F32)
    vp = v_ref[...].astype(F32)
    k2_scr[...] = jnp.where(own_half, kp, pltpu.roll(kp, SWA_HEAD_DIM, 1)).astype(BF16)
    v2 = jnp.where(own_half, vp, pltpu.roll(vp, SWA_HEAD_DIM, 1))
    vlo_scr[...] = jnp.where(hi_half, 0.0, v2).astype(BF16)
    vhi_scr[...] = jnp.where(hi_half, v2, 0.0).astype(BF16)

    lane_q = lax.broadcasted_iota(jnp.int32, (blk, LANES), 1) >= SWA_HEAD_DIM
    r = lax.broadcasted_iota(jnp.int32, (blk, 2 * blk), 0)
    c = lax.broadcasted_iota(jnp.int32, (blk, 2 * blk), 1)

    def q_block(n, _):
        q0 = pl.multiple_of(n * blk, blk)
        ks = pl.multiple_of(jnp.maximum(n - 1, 0) * blk, blk)
        rel = (q0 + r) - (ks + c)
        bias = jnp.where((rel >= 0) & (rel < SWA_WINDOW), 0.0, MASK_VALUE)
        k2 = k2_scr[pl.ds(ks, 2 * blk), :]
        vlo = vlo_scr[pl.ds(ks, 2 * blk), :]
        vhi = vhi_scr[pl.ds(ks, 2 * blk), :]
        for pair in range(group // 2):
            qp = q_ref[pl.ds(q0, blk), pair * LANES:(pair + 1) * LANES].astype(F32)
            out = jnp.zeros((blk, LANES), F32)
            for e in range(2):
                sink = sink_ref[layer, g * group + 2 * pair + e]
                qm = jnp.where(lane_q if e else jnp.logical_not(lane_q), qp, 0.0).astype(BF16)
                s = _dot_nt(qm, k2) + bias
                m = jnp.maximum(jnp.max(s, axis=-1, keepdims=True), sink)
                p = jnp.exp(s - m)
                denom = jnp.sum(p, axis=-1, keepdims=True) + jnp.exp(sink - m)
                out = out + _dot(p.astype(BF16), vhi if e else vlo) / denom
            o_ref[pl.ds(q0, blk), pair * LANES:(pair + 1) * LANES] = out.astype(BF16)
        return 0

    lax.fori_loop(0, seq // blk, q_block, 0, unroll=4)


def _swa_attn(q, kv, swa_sinks, swa_layer):
    batch, seq, nq = q.shape
    kvw = kv.shape[2]
    kv_heads = kvw // (2 * SWA_HEAD_DIM)
    q_heads = nq // SWA_HEAD_DIM
    group = q_heads // kv_heads
    gw = group * SWA_HEAD_DIM
    assert group % 2 == 0 and gw % LANES == 0 and kv_heads % 2 == 0
    v_off = kv_heads // 2
    return pl.pallas_call(
        functools.partial(_swa_attn_kernel, group=group, layer=swa_layer),
        grid=(batch, kv_heads),
        in_specs=[
            pl.BlockSpec(memory_space=pltpu.SMEM),
            pl.BlockSpec((None, seq, gw), lambda b, g: (b, 0, g)),
            pl.BlockSpec((None, seq, LANES), lambda b, g: (b, 0, g // 2)),
            pl.BlockSpec((None, seq, LANES), lambda b, g: (b, 0, v_off + g // 2)),
        ],
        out_specs=pl.BlockSpec((None, seq, gw), lambda b, g: (b, 0, g)),
        out_shape=jax.ShapeDtypeStruct((batch, seq, nq), BF16),
        scratch_shapes=[pltpu.VMEM((seq, LANES), BF16)] * 3,
        compiler_params=_params(("parallel", "arbitrary")),
        name="swa_attn",
    )(swa_sinks, q, kv, kv)


def _proj_ln_kernel(*refs, alpha, rows_per_batch, emit_next, cast_w, sub):
    refs = list(refs)
    w_scr = refs.pop() if cast_w else None
    if emit_next:
        a_ref, w_ref, x_ref, g_ref, lg_ref, lb_ref, sc_ref, sh_ref, xo_ref, ho_ref = refs
    else:
        a_ref, w_ref, x_ref, g_ref, lg_ref, lb_ref, xo_ref = refs
    if cast_w:
        @pl.when(pl.program_id(0) == 0)
        def _():
            w_scr[...] = w_ref[...].astype(BF16)
        w_ref = w_scr
    tm = a_ref.shape[0]
    b = (pl.program_id(0) * tm) // rows_per_batch
    gate = 1.0 + g_ref[pl.ds(b, 1), :]
    ln_g = lg_ref[...]
    ln_b = lb_ref[...]
    if emit_next:
        nsc = 1.0 + sc_ref[pl.ds(b, 1), :]
        nsh = sh_ref[pl.ds(b, 1), :]
    for c in range(tm // sub):
        sl = slice(c * sub, (c + 1) * sub)
        y = _dot(a_ref[sl, :], w_ref[...])
        z = alpha * x_ref[sl, :] + gate * y
        mu = jnp.mean(z, axis=-1, keepdims=True)
        zc = z - mu
        var = jnp.mean(zc * zc, axis=-1, keepdims=True)
        xn = zc * lax.rsqrt(var + LN_EPS) * ln_g + ln_b
        xo_ref[sl, :] = xn
        if emit_next:
            ho_ref[sl, :] = (xn * nsc + nsh).astype(BF16)


def _proj_ln(a, w, w_layer, x, mod, layer, gate_chunk, ln_g, ln_b, next_mod, *, alpha, tm):
    batch, seq, d = x.shape
    k = a.shape[-1]
    rows = batch * seq
    tm = min(tm, seq)
    emit_next = next_mod is not None
    cast_w = w.dtype != BF16
    kern = functools.partial(_proj_ln_kernel, alpha=alpha, rows_per_batch=seq, emit_next=emit_next,
                             cast_w=cast_w, sub=min(128, tm))
    row_spec = lambda width: pl.BlockSpec((tm, width), lambda i: (i, 0))
    vec_spec = _resident((None, 1, d), lambda i: (layer, 0, 0))
    in_specs = [row_spec(k), _resident((None, k, d), lambda i: (w_layer, 0, 0)), row_spec(d),
                _mod_spec(mod, layer, gate_chunk), vec_spec, vec_spec]
    depth = ln_g.shape[0]
    args = [a.reshape(rows, k), w, x.reshape(rows, d), mod, ln_g.reshape(depth, 1, d), ln_b.reshape(depth, 1, d)]
    out_specs = [row_spec(d)]
    out_shape = [jax.ShapeDtypeStruct((rows, d), F32)]
    if emit_next:
        nl, nsc, nsh = next_mod
        in_specs += [_mod_spec(mod, nl, nsc), _mod_spec(mod, nl, nsh)]
        args += [mod, mod]
        out_specs.append(row_spec(d))
        out_shape.append(jax.ShapeDtypeStruct((rows, d), BF16))
    outs = pl.pallas_call(
        kern,
        grid=(rows // tm,),
        in_specs=in_specs,
        out_specs=out_specs,
        out_shape=out_shape,
        scratch_shapes=[pltpu.VMEM((k, d), BF16)] if cast_w else [],
        compiler_params=_params(("arbitrary",)),
        name="proj_ln",
    )(*args)
    x_new = outs[0].reshape(batch, seq, d)
    return x_new, (outs[1].reshape(batch, seq, d) if emit_next else None)


def _ffn_up_kernel(h_ref, wg_ref, wv_ref, cwg_ref, cwv_ref, cbg_ref, cbv_ref, o_ref,
                   wg_scr, wv_scr, ug_scr, uv_scr, *, chunk):
    seq = h_ref.shape[0]
    tn = wg_ref.shape[1]
    n_chunks = seq // chunk
    wg_scr[...] = wg_ref[...].astype(BF16)
    wv_scr[...] = wv_ref[...].astype(BF16)

    def conv(u, halo, cw_ref, cb_ref):
        ext = jnp.concatenate([halo, u], axis=0)
        u1 = pltpu.roll(ext, 1, 0)[SUBLANES:]
        u2 = pltpu.roll(ext, 2, 0)[SUBLANES:]
        return u2 * cw_ref[0:1, :] + u1 * cw_ref[1:2, :] + u * cw_ref[2:3, :] + cb_ref[...]

    def matmuls(c):
        hc = h_ref[c * chunk:(c + 1) * chunk, :]
        ug_scr[c % 2] = _dot(hc, wg_scr[...])
        uv_scr[c % 2] = _dot(hc, wv_scr[...])

    def epilogue(c, halo_g, halo_v):
        ug = ug_scr[c % 2]
        uv = uv_scr[c % 2]
        cg = conv(ug, halo_g, cwg_ref, cbg_ref)
        cv = conv(uv, halo_v, cwv_ref, cbv_ref)
        o_ref[c * chunk:(c + 1) * chunk, :] = (cg * jax.nn.sigmoid(cg) * cv).astype(BF16)
        return ug[chunk - SUBLANES:], uv[chunk - SUBLANES:]

    halos = (jnp.zeros((SUBLANES, tn), F32), jnp.zeros((SUBLANES, tn), F32))
    matmuls(0)
    for c in range(1, n_chunks):
        matmuls(c)
        halos = epilogue(c - 1, *halos)
    epilogue(n_chunks - 1, *halos)


def _ffn_up(h, ffn_w_up, conv_w, conv_b, layer):
    batch, seq, d = h.shape
    depth, _, two_f = ffn_w_up.shape
    d_ff = two_f // 2
    tn = 512 if d_ff % 512 == 0 else LANES
    nt = d_ff // tn
    chunk = min(256, seq)
    lo = lambda b, j: (layer, 0, j)
    hi = lambda b, j: (layer, 0, nt + j)
    conv_b = conv_b.reshape(depth, 1, two_f)
    return pl.pallas_call(
        functools.partial(_ffn_up_kernel, chunk=chunk),
        grid=(batch, nt),
        in_specs=[
            pl.BlockSpec((None, seq, d), lambda b, j: (b, 0, 0)),
            pl.BlockSpec((None, d, tn), lo), pl.BlockSpec((None, d, tn), hi),
            pl.BlockSpec((None, CONV_WIDTH, tn), lo), pl.BlockSpec((None, CONV_WIDTH, tn), hi),
            pl.BlockSpec((None, 1, tn), lo), pl.BlockSpec((None, 1, tn), hi),
        ],
        out_specs=pl.BlockSpec((None, seq, tn), lambda b, j: (b, 0, j)),
        out_shape=jax.ShapeDtypeStruct((batch, seq, d_ff), BF16),
        scratch_shapes=[pltpu.VMEM((d, tn), BF16)] * 2 + [pltpu.VMEM((2, chunk, tn), F32)] * 2,
        compiler_params=_params(("parallel", "arbitrary")),
        name="ffn_up",
    )(h, ffn_w_up, ffn_w_up, conv_w, conv_w, conv_b, conv_b)


def kernel(x, c, positions, fox_w_in, fox_b_f, fox_w_o, swa_w_in, swa_sinks, swa_w_o, ada_w, ada_b,
           ffn_w_up, ffn_conv_w, ffn_conv_b, ffn_w_down, ln_mix_g, ln_mix_b, ln_ffn_g, ln_ffn_b):
    batch, seq, d = x.shape
    depth = ada_w.shape[0]
    alpha = (2.0 * depth) ** 0.25
    fox_heads = fox_b_f.shape[1]
    fox_dh = d // fox_heads

    mod = _ada(c, ada_w, ada_b)
    rope_tables = _rope_tables(positions) if depth > 1 else None
    w_down = ffn_w_down.astype(BF16)

    h = None
    for i in range(depth):
        j = i // 2
        if i % 2 == 0:
            w_f = fox_w_in[j, :, 3 * d + fox_heads - LANES:].astype(BF16)
            b_f = jnp.pad(fox_b_f[j], (LANES - fox_heads, 0)).reshape(1, LANES)
            if h is None:
                qkv, fl = _inproj_fox(x, mod, i, fox_w_in, j, w_f, b_f, head_dim=fox_dh)
            else:
                qkv, fl = _inproj_fox(h, None, i, fox_w_in, j, w_f, b_f, head_dim=fox_dh)
            ccol, crow = _fox_cum(fl, fox_heads)
            o = _fox_attn(qkv, crow, ccol, heads=fox_heads)
            w_o = fox_w_o
        else:
            if h is None:
                raise NotImplementedError("SWA as the first layer")
            q, kv = _inproj_swa(h, swa_w_in, j, rope_tables, q_heads=swa_sinks.shape[1])
            o = _swa_attn(q, kv, swa_sinks, j)
            w_o = swa_w_o
        x, h2 = _proj_ln(o, w_o, j, x, mod, i, G1, ln_mix_g, ln_mix_b, (i, SC2, SH2), alpha=alpha, tm=512)
        hmid = _ffn_up(h2, ffn_w_up, ffn_conv_w, ffn_conv_b, i)
        next_mod = (i + 1, SC1, SH1) if i + 1 < depth else None
        x, h = _proj_ln(hmid, w_down, i, x, mod, i, G2, ln_ffn_g, ln_ffn_b, next_mod, alpha=alpha, tm=256)
    return x
```

```python
import functools

import jax
import jax.numpy as jnp
from jax import lax
from jax.experimental import pallas as pl
from jax.experimental.pallas import tpu as pltpu

F32 = jnp.float32
BF16 = jnp.bfloat16

LANES = 128
SUBLANES = 8
VMEM_LIMIT_BYTES = 56 * 1024 * 1024

LN_EPS = 1e-5
ROPE_THETA = 500000.0
ROPE_DIM = 16
ROPE_HALF = ROPE_DIM // 2
SWA_HEAD_DIM = 64
SWA_WINDOW = 128
CONV_WIDTH = 3
MASK_VALUE = -1e30
LOG2E = 1.4426950408889634

SH1, SC1, G1, SH2, SC2, G2 = range(6)


def _params(semantics):
    return pltpu.CompilerParams(dimension_semantics=semantics, vmem_limit_bytes=VMEM_LIMIT_BYTES)


def _dot(a, b):
    return jnp.dot(a, b, preferred_element_type=F32)


def _dot_nt(a, b):
    return lax.dot_general(a, b, (((1,), (1,)), ((), ())), preferred_element_type=F32)


def _resident(block_shape, index_map):
    return pl.BlockSpec(block_shape, index_map, pipeline_mode=pl.Buffered(1))


def _mod_spec(mod, layer, chunk):
    _, _, batch, d = mod.shape
    return _resident((None, None, batch, d), lambda *_: (layer, chunk, 0, 0))


def _ada_kernel(c_ref, w_ref, b_ref, o_ref):
    c = c_ref[...]
    c_act = (c * jax.nn.sigmoid(c)).astype(BF16)
    o_ref[...] = _dot(c_act, w_ref[...].astype(BF16)) + b_ref[...]


def _ada(c, ada_w, ada_b):
    depth, d, six_d = ada_w.shape
    batch = c.shape[0]
    tn = min(1024, d)
    nt = d // tn
    return pl.pallas_call(
        _ada_kernel,
        grid=(depth, 6, nt),
        in_specs=[
            pl.BlockSpec((batch, d), lambda i, k, j: (0, 0)),
            pl.BlockSpec((None, d, tn), lambda i, k, j: (i, 0, k * nt + j)),
            pl.BlockSpec((None, 1, tn), lambda i, k, j: (i, 0, k * nt + j)),
        ],
        out_specs=pl.BlockSpec((None, None, batch, tn), lambda i, k, j: (i, k, 0, j)),
        out_shape=jax.ShapeDtypeStruct((depth, 6, batch, d), F32),
        compiler_params=_params(("parallel", "parallel", "parallel")),
        name="ada_mod",
    )(c, ada_w, ada_b.reshape(depth, 1, six_d))


def _cast_cols_kernel(w_ref, o_ref, *, valid_cols):
    tn = w_ref.shape[1]
    col = pl.program_id(1) * tn + lax.broadcasted_iota(jnp.int32, w_ref.shape, 1)
    o_ref[...] = jnp.where(col < valid_cols, w_ref[...], 0.0).astype(BF16)


def _fox_w_bf16(fox_w_in):
    n, d, cols = fox_w_in.shape
    tn = 512
    nt = pl.cdiv(cols, tn)
    return pl.pallas_call(
        functools.partial(_cast_cols_kernel, valid_cols=cols),
        grid=(n, nt),
        in_specs=[pl.BlockSpec((None, d, tn), lambda l, j: (l, 0, j))],
        out_specs=pl.BlockSpec((None, d, tn), lambda l, j: (l, 0, j)),
        out_shape=jax.ShapeDtypeStruct((n, d, nt * tn), BF16),
        compiler_params=_params(("parallel", "parallel")),
        name="fox_w_cast",
    )(fox_w_in)


def _inproj_fox_kernel(*refs, modulate, tiles_per_seq, q_tiles, q_scale):
    if modulate:
        x_ref, sc_ref, sh_ref, w_ref, wf_ref, bf_ref, qkv_ref, fl_ref, h_scr = refs
    else:
        x_ref, w_ref, wf_ref, bf_ref, qkv_ref, fl_ref = refs
    i = pl.program_id(0)
    j = pl.program_id(1)

    if modulate:
        @pl.when(j == 0)
        def _():
            b = i // tiles_per_seq
            sc = sc_ref[pl.ds(b, 1), :]
            sh = sh_ref[pl.ds(b, 1), :]
            h_scr[...] = (x_ref[...] * (1.0 + sc) + sh).astype(BF16)
        h_ref = h_scr
    else:
        h_ref = x_ref

    @pl.when(j == 0)
    def _():
        fl_ref[...] = _dot(h_ref[...], wf_ref[...]) + bf_ref[...]

    acc = _dot(h_ref[...], w_ref[...])
    acc = acc * jnp.where(j < q_tiles, q_scale, 1.0)
    for hh in range(acc.shape[1] // LANES):
        qkv_ref[hh] = acc[:, hh * LANES:(hh + 1) * LANES].astype(BF16)


def _inproj_fox(x_or_h, mod, layer, fox_w, fox_layer, b_f, *, head_dim):
    batch, seq, d = x_or_h.shape
    modulate = mod is not None
    n_out = 3 * d
    assert head_dim == LANES
    tn = min(1024, d)
    tm = min(1024 if modulate else 2048, seq)
    nm = seq // tm
    slots = n_out // LANES
    kern = functools.partial(_inproj_fox_kernel, modulate=modulate, tiles_per_seq=nm,
                             q_tiles=d // tn, q_scale=LOG2E * float(head_dim) ** -0.5)
    in_specs = [pl.BlockSpec((None, tm, d), lambda i, j: (i // nm, i % nm, 0))]
    args = [x_or_h]
    if modulate:
        in_specs += [_mod_spec(mod, layer, SC1), _mod_spec(mod, layer, SH1)]
        args += [mod, mod]
    in_specs += [
        pl.BlockSpec((None, d, tn), lambda i, j: (fox_layer, 0, j)),
        _resident((None, d, LANES), lambda i, j: (fox_layer, 0, n_out // LANES)),
        _resident((1, LANES), lambda i, j: (0, 0)),
    ]
    args += [fox_w, fox_w, b_f]
    return pl.pallas_call(
        kern,
        grid=(batch * nm, n_out // tn),
        in_specs=in_specs,
        out_specs=[
            pl.BlockSpec((None, tn // LANES, tm, LANES), lambda i, j: (i // nm, j, i % nm, 0)),
            pl.BlockSpec((None, tm, LANES), lambda i, j: (i // nm, i % nm, 0)),
        ],
        out_shape=[
            jax.ShapeDtypeStruct((batch, slots, seq, LANES), BF16),
            jax.ShapeDtypeStruct((batch, seq, LANES), F32),
        ],
        scratch_shapes=[pltpu.VMEM((tm, d), BF16)] if modulate else [],
        compiler_params=_params(("parallel", "arbitrary")),
        name="fox_inproj",
    )(*args)


def _fox_cum_kernel(fl_ref, col_ref, row_ref, *, heads):
    seq = fl_ref.shape[0]
    x = fl_ref[...]
    log_f = (jnp.minimum(x, 0.0) - jnp.log1p(jnp.exp(-jnp.abs(x)))) * LOG2E
    r = lax.broadcasted_iota(jnp.int32, (LANES, LANES), 0)
    c = lax.broadcasted_iota(jnp.int32, (LANES, LANES), 1)
    tri = (c <= r).astype(BF16)
    carry = jnp.zeros((1, LANES), F32)
    for blk in range(seq // LANES):
        xb = log_f[blk * LANES:(blk + 1) * LANES]
        hi = xb.astype(BF16)
        rem = xb - hi.astype(F32)
        mid = rem.astype(BF16)
        lo = (rem - mid.astype(F32)).astype(BF16)
        cb = _dot(tri, hi) + _dot(tri, mid) + _dot(tri, lo) + carry
        col_ref[blk * LANES:(blk + 1) * LANES, :] = cb
        carry = cb[LANES - 1:LANES, :]
    row_ref[...] = col_ref[...].T[:heads]


def _fox_cum(fl, heads):
    batch, seq, _ = fl.shape
    return pl.pallas_call(
        functools.partial(_fox_cum_kernel, heads=heads),
        grid=(batch,),
        in_specs=[pl.BlockSpec((None, seq, LANES), lambda b: (b, 0, 0))],
        out_specs=[
            pl.BlockSpec((None, seq, LANES), lambda b: (b, 0, 0)),
            pl.BlockSpec((None, heads, seq), lambda b: (b, 0, 0)),
        ],
        out_shape=[
            jax.ShapeDtypeStruct((batch, seq, LANES), F32),
            jax.ShapeDtypeStruct((batch, heads, seq), F32),
        ],
        compiler_params=_params(("parallel",)),
        name="fox_cum",
    )(fl)


def _fox_attn_kernel(q_ref, k_ref, v_ref, crow_ref, ccol_ref, o_ref, *, heads_per_step, blk):
    seq = q_ref.shape[1]
    hb = pl.program_id(1)
    lane = lax.broadcasted_iota(jnp.int32, (blk, LANES), 1)
    rows = lax.broadcasted_iota(jnp.int32, (blk, blk), 0)
    cols = lax.broadcasted_iota(jnp.int32, (blk, blk), 1)
    causal = cols <= rows

    def scores(hh, qi):
        h = hb * heads_per_step + hh
        q0 = qi * blk
        lk = q0 + blk
        s = _dot_nt(q_ref[hh, q0:lk, :], k_ref[hh, :lk, :]) - crow_ref[pl.ds(h, 1), :lk]
        diag = jnp.where(causal, s[:, q0:], MASK_VALUE)
        return diag if qi == 0 else jnp.concatenate([s[:, :q0], diag], axis=1)

    def finish(hh, qi, s):
        h = hb * heads_per_step + hh
        q0 = qi * blk
        lk = q0 + blk
        cq = jnp.sum(jnp.where(lane == h, ccol_ref[q0:lk, :], 0.0), axis=-1, keepdims=True)
        m = jnp.max(s, axis=-1, keepdims=True) + cq
        p = jnp.exp2(s + (cq - m))
        l = jnp.sum(p, axis=-1, keepdims=True)
        acc = _dot(p.astype(BF16), v_ref[hh, :lk, :])
        o_ref[q0:lk, hh * LANES:(hh + 1) * LANES] = (acc / l).astype(BF16)

    work = [(hh, qi) for hh in range(heads_per_step) for qi in range(seq // blk)]
    s_next = scores(*work[0])
    for idx, item in enumerate(work):
        s_cur = s_next
        if idx + 1 < len(work):
            s_next = scores(*work[idx + 1])
        finish(*item, s_cur)


def _fox_attn(qkv, crow, ccol, *, heads):
    batch, slots, seq, dh = qkv.shape
    assert slots == 3 * heads and dh == LANES
    hps = 2 if heads % 2 == 0 else 1
    groups = heads // hps
    blk = min(256, seq)
    kern = functools.partial(_fox_attn_kernel, heads_per_step=hps, blk=blk)
    head_spec = lambda off: pl.BlockSpec((None, hps, seq, dh), lambda b, g: (b, off + g, 0, 0))
    return pl.pallas_call(
        kern,
        grid=(batch, groups),
        in_specs=[
            head_spec(0), head_spec(groups), head_spec(2 * groups),
            pl.BlockSpec((None, heads, seq), lambda b, g: (b, 0, 0)),
            pl.BlockSpec((None, seq, LANES), lambda b, g: (b, 0, 0)),
        ],
        out_specs=pl.BlockSpec((None, seq, hps * dh), lambda b, g: (b, 0, g)),
        out_shape=jax.ShapeDtypeStruct((batch, seq, heads * dh), BF16),
        compiler_params=_params(("parallel", "parallel")),
        name="fox_attn",
    )(qkv, qkv, qkv, crow, ccol)


def _rope_table_kernel(pos_ref, inv_ref, o_ref):
    pos = pos_ref[...].astype(F32)
    ang = pos * inv_ref[...]
    cos = jnp.cos(ang)
    sin = jnp.sin(ang)
    idx = lax.broadcasted_iota(jnp.int32, ang.shape, 1) % SWA_HEAD_DIM
    o_ref[0] = jnp.where(idx < ROPE_DIM, cos, 1.0)
    o_ref[1] = jnp.where((idx >= ROPE_HALF) & (idx < ROPE_DIM), sin, 0.0)
    o_ref[2] = jnp.where(idx < ROPE_HALF, -sin, 0.0)


def _rope_tables(positions):
    batch, seq = positions.shape
    inv_freq = ROPE_THETA ** (-jnp.arange(0, ROPE_DIM, 2, dtype=F32) / ROPE_DIM)
    idx = jnp.arange(LANES) % SWA_HEAD_DIM
    inv_lane = jnp.where(idx < ROPE_DIM, inv_freq[idx % ROPE_HALF], 0.0).reshape(1, LANES)
    return pl.pallas_call(
        _rope_table_kernel,
        grid=(batch,),
        in_specs=[
            pl.BlockSpec((None, seq, 1), lambda b: (b, 0, 0)),
            pl.BlockSpec((1, LANES), lambda b: (0, 0)),
        ],
        out_specs=pl.BlockSpec((3, None, seq, LANES), lambda b: (0, b, 0, 0)),
        out_shape=jax.ShapeDtypeStruct((3, batch, seq, LANES), F32),
        compiler_params=_params(("parallel",)),
        name="rope_tables",
    )(positions.reshape(batch, seq, 1), inv_lane)


def _inproj_swa_kernel(h_ref, wq_ref, wkv_ref, rt_ref, q_ref, kv_ref, *, q_tiles, q_scale, k_width):
    j = pl.program_id(1)
    cos = rt_ref[0]
    sin_lo = rt_ref[1]
    sin_hi = rt_ref[2]

    def rope(a):
        return (a * cos + pltpu.roll(a, ROPE_HALF, 1) * sin_lo
                + pltpu.roll(a, LANES - ROPE_HALF, 1) * sin_hi)

    @pl.when(j < q_tiles)
    def _():
        acc = _dot(h_ref[...], wq_ref[...].astype(BF16))
        for cc in range(acc.shape[1] // LANES):
            sl = slice(cc * LANES, (cc + 1) * LANES)
            q_ref[:, sl] = (rope(acc[:, sl]) * q_scale).astype(BF16)

    @pl.when(j == q_tiles)
    def _():
        acc = _dot(h_ref[...], wkv_ref[...].astype(BF16))
        for cc in range(acc.shape[1] // LANES):
            sl = slice(cc * LANES, (cc + 1) * LANES)
            a = acc[:, sl]
            kv_ref[:, sl] = (rope(a) if cc * LANES < k_width else a).astype(BF16)


def _inproj_swa(h, swa_w_in, swa_layer, rope_tables, *, q_heads):
    batch, seq, d = h.shape
    nq = q_heads * SWA_HEAD_DIM
    kvw = swa_w_in.shape[2] - nq
    assert (kvw // 2) % LANES == 0
    tn = min(512, nq)
    assert nq % tn == 0 and nq % kvw == 0
    q_tiles = nq // tn
    kern = functools.partial(_inproj_swa_kernel, q_tiles=q_tiles,
                             q_scale=LOG2E * float(SWA_HEAD_DIM) ** -0.5, k_width=kvw // 2)
    return pl.pallas_call(
        kern,
        grid=(batch, q_tiles + 1),
        in_specs=[
            pl.BlockSpec((None, seq, d), lambda b, j: (b, 0, 0)),
            pl.BlockSpec((None, d, tn), lambda b, j: (swa_layer, 0, jnp.minimum(j, q_tiles - 1))),
            _resident((None, d, kvw), lambda b, j: (swa_layer, 0, nq // kvw)),
            pl.BlockSpec((3, None, seq, LANES), lambda b, j: (0, b, 0, 0)),
        ],
        out_specs=[
            pl.BlockSpec((None, seq, tn), lambda b, j: (b, 0, jnp.minimum(j, q_tiles - 1))),
            pl.BlockSpec((None, seq, kvw), lambda b, j: (b, 0, 0)),
        ],
        out_shape=[
            jax.ShapeDtypeStruct((batch, seq, nq), BF16),
            jax.ShapeDtypeStruct((batch, seq, kvw), BF16),
        ],
        compiler_params=_params(("parallel", "arbitrary")),
        name="swa_inproj",
    )(h, swa_w_in, swa_w_in, rope_tables)


def _swa_attn_kernel(sink_ref, q_ref, k_ref, v_ref, o_ref, k2_scr, vlo_scr, vhi_scr, *, group, layer):
    seq = q_ref.shape[0]
    blk = SWA_WINDOW
    g = pl.program_id(1)
    lane_s = lax.broadcasted_iota(jnp.int32, (seq, LANES), 1)
    hi_half = lane_s >= SWA_HEAD_DIM
    own_half = (lane_s // SWA_HEAD_DIM) == (g % 2)
    kp = k_ref[...].astype(F32)
    vp = v_ref[...].astype(F32)
    k2_scr[...] = jnp.where(own_half, kp, pltpu.roll(kp, SWA_HEAD_DIM, 1)).astype(BF16)
    v2 = jnp.where(own_half, vp, pltpu.roll(vp, SWA_HEAD_DIM, 1))
    vlo_scr[...] = jnp.where(hi_half, 0.0, v2).astype(BF16)
    vhi_scr[...] = jnp.where(hi_half, v2, 0.0).astype(BF16)

    lane_q = lax.broadcasted_iota(jnp.int32, (blk, LANES), 1) >= SWA_HEAD_DIM
    r = lax.broadcasted_iota(jnp.int32, (blk, 2 * blk), 0)
    c = lax.broadcasted_iota(jnp.int32, (blk, 2 * blk), 1)

    def q_block(n, _):
        q0 = pl.multiple_of(n * blk, blk)
        ks = pl.multiple_of(jnp.maximum(n - 1, 0) * blk, blk)
        rel = (q0 + r) - (ks + c)
        bias = jnp.where((rel >= 0) & (rel < SWA_WINDOW), 0.0, MASK_VALUE)
        k2 = k2_scr[pl.ds(ks, 2 * blk), :]
        vlo = vlo_scr[pl.ds(ks, 2 * blk), :]
        vhi = vhi_scr[pl.ds(ks, 2 * blk), :]
        for pair in range(group // 2):
            qp = q_ref[pl.ds(q0, blk), pair * LANES:(pair + 1) * LANES].astype(F32)
            out = jnp.zeros((blk, LANES), F32)
            for e in range(2):
                sink = sink_ref[layer, g * group + 2 * pair + e] * LOG2E
                qm = jnp.where(lane_q if e else jnp.logical_not(lane_q), qp, 0.0).astype(BF16)
                s = _dot_nt(qm, k2) + bias
                m = jnp.maximum(jnp.max(s, axis=-1, keepdims=True), sink)
                p = jnp.exp2(s - m)
                denom = jnp.sum(p, axis=-1, keepdims=True) + jnp.exp2(sink - m)
                out = out + _dot(p.astype(BF16), vhi if e else vlo) / denom
            o_ref[pl.ds(q0, blk), pair * LANES:(pair + 1) * LANES] = out.astype(BF16)
        return 0

    lax.fori_loop(0, seq // blk, q_block, 0, unroll=4)


def _swa_attn(q, kv, swa_sinks, swa_layer):
    batch, seq, nq = q.shape
    kvw = kv.shape[2]
    kv_heads = kvw // (2 * SWA_HEAD_DIM)
    q_heads = nq // SWA_HEAD_DIM
    group = q_heads // kv_heads
    gw = group * SWA_HEAD_DIM
    assert group % 2 == 0 and gw % LANES == 0 and kv_heads % 2 == 0
    v_off = kv_heads // 2
    return pl.pallas_call(
        functools.partial(_swa_attn_kernel, group=group, layer=swa_layer),
        grid=(batch, kv_heads),
        in_specs=[
            pl.BlockSpec(memory_space=pltpu.SMEM),
            pl.BlockSpec((None, seq, gw), lambda b, g: (b, 0, g)),
            pl.BlockSpec((None, seq, LANES), lambda b, g: (b, 0, g // 2)),
            pl.BlockSpec((None, seq, LANES), lambda b, g: (b, 0, v_off + g // 2)),
        ],
        out_specs=pl.BlockSpec((None, seq, gw), lambda b, g: (b, 0, g)),
        out_shape=jax.ShapeDtypeStruct((batch, seq, nq), BF16),
        scratch_shapes=[pltpu.VMEM((seq, LANES), BF16)] * 3,
        compiler_params=_params(("parallel", "arbitrary")),
        name="swa_attn",
    )(swa_sinks, q, kv, kv)


def _proj_ln_kernel(*refs, alpha, rows_per_batch, emit_next, cast_w, sub):
    refs = list(refs)
    w_scr = refs.pop() if cast_w else None
    if emit_next:
        a_ref, w_ref, x_ref, g_ref, lg_ref, lb_ref, sc_ref, sh_ref, xo_ref, ho_ref = refs
    else:
        a_ref, w_ref, x_ref, g_ref, lg_ref, lb_ref, xo_ref = refs
    if cast_w:
        @pl.when(pl.program_id(0) == 0)
        def _():
            w_scr[...] = w_ref[...].astype(BF16)
        w_ref = w_scr
    tm = a_ref.shape[0]
    b = (pl.program_id(0) * tm) // rows_per_batch
    gate = 1.0 + g_ref[pl.ds(b, 1), :]
    ln_g = lg_ref[...]
    ln_b = lb_ref[...]
    if emit_next:
        nsc = 1.0 + sc_ref[pl.ds(b, 1), :]
        nsh = sh_ref[pl.ds(b, 1), :]
    for c in range(tm // sub):
        sl = slice(c * sub, (c + 1) * sub)
        y = _dot(a_ref[sl, :], w_ref[...])
        z = alpha * x_ref[sl, :] + gate * y
        mu = jnp.mean(z, axis=-1, keepdims=True)
        zc = z - mu
        var = jnp.mean(zc * zc, axis=-1, keepdims=True)
        xn = zc * lax.rsqrt(var + LN_EPS) * ln_g + ln_b
        xo_ref[sl, :] = xn
        if emit_next:
            ho_ref[sl, :] = (xn * nsc + nsh).astype(BF16)


def _proj_ln(a, w, w_layer, x, mod, layer, gate_chunk, ln_g, ln_b, next_mod, *, alpha, tm):
    batch, seq, d = x.shape
    k = a.shape[-1]
    rows = batch * seq
    tm = min(tm, seq)
    emit_next = next_mod is not None
    cast_w = w.dtype != BF16
    kern = functools.partial(_proj_ln_kernel, alpha=alpha, rows_per_batch=seq, emit_next=emit_next,
                             cast_w=cast_w, sub=min(128, tm))
    row_spec = lambda width: pl.BlockSpec((tm, width), lambda i: (i, 0))
    vec_spec = _resident((None, 1, d), lambda i: (layer, 0, 0))
    in_specs = [row_spec(k), _resident((None, k, d), lambda i: (w_layer, 0, 0)), row_spec(d),
                _mod_spec(mod, layer, gate_chunk), vec_spec, vec_spec]
    depth = ln_g.shape[0]
    args = [a.reshape(rows, k), w, x.reshape(rows, d), mod, ln_g.reshape(depth, 1, d), ln_b.reshape(depth, 1, d)]
    out_specs = [row_spec(d)]
    out_shape = [jax.ShapeDtypeStruct((rows, d), F32)]
    if emit_next:
        nl, nsc, nsh = next_mod
        in_specs += [_mod_spec(mod, nl, nsc), _mod_spec(mod, nl, nsh)]
        args += [mod, mod]
        out_specs.append(row_spec(d))
        out_shape.append(jax.ShapeDtypeStruct((rows, d), BF16))
    outs = pl.pallas_call(
        kern,
        grid=(rows // tm,),
        in_specs=in_specs,
        out_specs=out_specs,
        out_shape=out_shape,
        scratch_shapes=[pltpu.VMEM((k, d), BF16)] if cast_w else [],
        compiler_params=_params(("arbitrary",)),
        name="proj_ln",
    )(*args)
    x_new = outs[0].reshape(batch, seq, d)
    return x_new, (outs[1].reshape(batch, seq, d) if emit_next else None)


def _ffn_up_kernel(h_ref, wg_ref, wv_ref, cwg_ref, cwv_ref, cbg_ref, cbv_ref, o_ref,
                   wg_scr, wv_scr, ug_scr, uv_scr, *, chunk):
    seq = h_ref.shape[0]
    tn = wg_ref.shape[1]
    n_chunks = seq // chunk

    @pl.when(pl.program_id(1) == 0)
    def _():
        wg_scr[...] = wg_ref[...].astype(BF16)
        wv_scr[...] = wv_ref[...].astype(BF16)

    def conv(u, halo, cw_ref, cb_ref):
        ext = jnp.concatenate([halo, u], axis=0)
        u1 = pltpu.roll(ext, 1, 0)[SUBLANES:]
        u2 = pltpu.roll(ext, 2, 0)[SUBLANES:]
        return u2 * cw_ref[0:1, :] + u1 * cw_ref[1:2, :] + u * cw_ref[2:3, :] + cb_ref[...]

    def matmuls(c):
        hc = h_ref[c * chunk:(c + 1) * chunk, :]
        ug_scr[c % 2] = _dot(hc, wg_scr[...])
        uv_scr[c % 2] = _dot(hc, wv_scr[...])

    def epilogue(c, halo_g, halo_v):
        ug = ug_scr[c % 2]
        uv = uv_scr[c % 2]
        cg = conv(ug, halo_g, cwg_ref, cbg_ref)
        cv = conv(uv, halo_v, cwv_ref, cbv_ref)
        o_ref[c * chunk:(c + 1) * chunk, :] = (cg * jax.nn.sigmoid(cg) * cv).astype(BF16)
        return ug[chunk - SUBLANES:], uv[chunk - SUBLANES:]

    halos = (jnp.zeros((SUBLANES, tn), F32), jnp.zeros((SUBLANES, tn), F32))
    matmuls(0)
    for c in range(1, n_chunks):
        matmuls(c)
        halos = epilogue(c - 1, *halos)
    epilogue(n_chunks - 1, *halos)


def _ffn_up(h, ffn_w_up, conv_w, conv_b, layer):
    batch, seq, d = h.shape
    depth, _, two_f = ffn_w_up.shape
    d_ff = two_f // 2
    tn = 512 if d_ff % 512 == 0 else LANES
    nt = d_ff // tn
    chunk = min(256, seq)
    lo = lambda j, b: (layer, 0, j)
    hi = lambda j, b: (layer, 0, nt + j)
    conv_b = conv_b.reshape(depth, 1, two_f)
    return pl.pallas_call(
        functools.partial(_ffn_up_kernel, chunk=chunk),
        grid=(nt, batch),
        in_specs=[
            pl.BlockSpec((None, seq, d), lambda j, b: (b, 0, 0)),
            pl.BlockSpec((None, d, tn), lo), pl.BlockSpec((None, d, tn), hi),
            pl.BlockSpec((None, CONV_WIDTH, tn), lo), pl.BlockSpec((None, CONV_WIDTH, tn), hi),
            pl.BlockSpec((None, 1, tn), lo), pl.BlockSpec((None, 1, tn), hi),
        ],
        out_specs=pl.BlockSpec((None, seq, tn), lambda j, b: (b, 0, j)),
        out_shape=jax.ShapeDtypeStruct((batch, seq, d_ff), BF16),
        scratch_shapes=[pltpu.VMEM((d, tn), BF16)] * 2 + [pltpu.VMEM((2, chunk, tn), F32)] * 2,
        compiler_params=_params(("parallel", "arbitrary")),
        name="ffn_up",
    )(h, ffn_w_up, ffn_w_up, conv_w, conv_w, conv_b, conv_b)


def kernel(x, c, positions, fox_w_in, fox_b_f, fox_w_o, swa_w_in, swa_sinks, swa_w_o, ada_w, ada_b,
           ffn_w_up, ffn_conv_w, ffn_conv_b, ffn_w_down, ln_mix_g, ln_mix_b, ln_ffn_g, ln_ffn_b):
    batch, seq, d = x.shape
    depth = ada_w.shape[0]
    alpha = (2.0 * depth) ** 0.25
    fox_heads = fox_b_f.shape[1]
    fox_dh = d // fox_heads

    mod = _ada(c, ada_w, ada_b)
    rope_tables = _rope_tables(positions) if depth > 1 else None
    w_down = ffn_w_down.astype(BF16)
    fox_w = _fox_w_bf16(fox_w_in)

    h = None
    for i in range(depth):
        j = i // 2
        if i % 2 == 0:
            b_f = jnp.pad(fox_b_f[j], (0, LANES - fox_heads)).reshape(1, LANES)
            if h is None:
                qkv, fl = _inproj_fox(x, mod, i, fox_w, j, b_f, head_dim=fox_dh)
            else:
                qkv, fl = _inproj_fox(h, None, i, fox_w, j, b_f, head_dim=fox_dh)
            ccol, crow = _fox_cum(fl, fox_heads)
            o = _fox_attn(qkv, crow, ccol, heads=fox_heads)
            w_o = fox_w_o
        else:
            if h is None:
                raise NotImplementedError("SWA as the first layer")
            q, kv = _inproj_swa(h, swa_w_in, j, rope_tables, q_heads=swa_sinks.shape[1])
            o = _swa_attn(q, kv, swa_sinks, j)
            w_o = swa_w_o
        x, h2 = _proj_ln(o, w_o, j, x, mod, i, G1, ln_mix_g, ln_mix_b, (i, SC2, SH2), alpha=alpha, tm=512)
        hmid = _ffn_up(h2, ffn_w_up, ffn_conv_w, ffn_conv_b, i)
        next_mod = (i + 1, SC1, SH1) if i + 1 < depth else None
        x, h = _proj_ln(hmid, w_down, i, x, mod, i, G2, ln_ffn_g, ln_ffn_b, next_mod, alpha=alpha, tm=256)
    return x
```

```python
import functools

import jax
import jax.numpy as jnp
from jax import lax
from jax.experimental import pallas as pl
from jax.experimental.pallas import tpu as pltpu

F32 = jnp.float32
BF16 = jnp.bfloat16

LANES = 128
SUBLANES = 8
VMEM_LIMIT_BYTES = 56 * 1024 * 1024

LN_EPS = 1e-5
ROPE_THETA = 500000.0
ROPE_DIM = 16
ROPE_HALF = ROPE_DIM // 2
SWA_HEAD_DIM = 64
SWA_WINDOW = 128
CONV_WIDTH = 3
MASK_VALUE = -1e30
LOG2E = 1.4426950408889634

SH1, SC1, G1, SH2, SC2, G2 = range(6)


def _params(semantics):
    return pltpu.CompilerParams(dimension_semantics=semantics, vmem_limit_bytes=VMEM_LIMIT_BYTES)


def _dot(a, b):
    return jnp.dot(a, b, preferred_element_type=F32)


def _dot_nt(a, b):
    return lax.dot_general(a, b, (((1,), (1,)), ((), ())), preferred_element_type=F32)


def _resident(block_shape, index_map):
    return pl.BlockSpec(block_shape, index_map, pipeline_mode=pl.Buffered(1))


def _mod_spec(mod, layer, chunk):
    _, _, batch, d = mod.shape
    return _resident((None, None, batch, d), lambda *_: (layer, chunk, 0, 0))


def _ada_kernel(c_ref, w_ref, b_ref, o_ref):
    c = c_ref[...]
    c_act = (c * jax.nn.sigmoid(c)).astype(BF16)
    o_ref[...] = _dot(c_act, w_ref[...].astype(BF16)) + b_ref[...]


def _ada(c, ada_w, ada_b):
    depth, d, six_d = ada_w.shape
    batch = c.shape[0]
    tn = min(1024, d)
    nt = d // tn
    return pl.pallas_call(
        _ada_kernel,
        grid=(depth, 6, nt),
        in_specs=[
            pl.BlockSpec((batch, d), lambda i, k, j: (0, 0)),
            pl.BlockSpec((None, d, tn), lambda i, k, j: (i, 0, k * nt + j)),
            pl.BlockSpec((None, 1, tn), lambda i, k, j: (i, 0, k * nt + j)),
        ],
        out_specs=pl.BlockSpec((None, None, batch, tn), lambda i, k, j: (i, k, 0, j)),
        out_shape=jax.ShapeDtypeStruct((depth, 6, batch, d), F32),
        compiler_params=_params(("parallel", "parallel", "parallel")),
        name="ada_mod",
    )(c, ada_w, ada_b.reshape(depth, 1, six_d))


def _cast_cols_kernel(wt_ref, o_ref, *, valid_cols):
    tn = wt_ref.shape[0]
    w = wt_ref[...].T
    col = pl.program_id(1) * tn + lax.broadcasted_iota(jnp.int32, w.shape, 1)
    o_ref[...] = jnp.where(col < valid_cols, w, 0.0).astype(BF16)


def _fox_w_bf16(fox_w_in):
    n, d, cols = fox_w_in.shape
    tn = 512
    nt = pl.cdiv(cols, tn)
    return pl.pallas_call(
        functools.partial(_cast_cols_kernel, valid_cols=cols),
        grid=(n, nt),
        in_specs=[pl.BlockSpec((None, tn, d), lambda l, j: (l, j, 0))],
        out_specs=pl.BlockSpec((None, d, tn), lambda l, j: (l, 0, j)),
        out_shape=jax.ShapeDtypeStruct((n, d, nt * tn), BF16),
        compiler_params=_params(("parallel", "parallel")),
        name="fox_w_cast",
    )(jnp.swapaxes(fox_w_in, 1, 2))


def _inproj_fox_kernel(*refs, modulate, tiles_per_seq, q_tiles, q_scale):
    if modulate:
        x_ref, sc_ref, sh_ref, w_ref, wf_ref, bf_ref, qkv_ref, fl_ref, h_scr = refs
    else:
        x_ref, w_ref, wf_ref, bf_ref, qkv_ref, fl_ref = refs
    i = pl.program_id(0)
    j = pl.program_id(1)

    if modulate:
        @pl.when(j == 0)
        def _():
            b = i // tiles_per_seq
            sc = sc_ref[pl.ds(b, 1), :]
            sh = sh_ref[pl.ds(b, 1), :]
            h_scr[...] = (x_ref[...] * (1.0 + sc) + sh).astype(BF16)
        h_ref = h_scr
    else:
        h_ref = x_ref

    @pl.when(j == 0)
    def _():
        fl_ref[...] = _dot(h_ref[...], wf_ref[...]) + bf_ref[...]

    acc = _dot(h_ref[...], w_ref[...])
    acc = acc * jnp.where(j < q_tiles, q_scale, 1.0)
    for hh in range(acc.shape[1] // LANES):
        qkv_ref[hh] = acc[:, hh * LANES:(hh + 1) * LANES].astype(BF16)


def _inproj_fox(x_or_h, mod, layer, fox_w, fox_layer, b_f, *, head_dim):
    batch, seq, d = x_or_h.shape
    modulate = mod is not None
    n_out = 3 * d
    assert head_dim == LANES
    tn = min(1024, d)
    tm = min(1024 if modulate else 2048, seq)
    nm = seq // tm
    slots = n_out // LANES
    kern = functools.partial(_inproj_fox_kernel, modulate=modulate, tiles_per_seq=nm,
                             q_tiles=d // tn, q_scale=LOG2E * float(head_dim) ** -0.5)
    in_specs = [pl.BlockSpec((None, tm, d), lambda i, j: (i // nm, i % nm, 0))]
    args = [x_or_h]
    if modulate:
        in_specs += [_mod_spec(mod, layer, SC1), _mod_spec(mod, layer, SH1)]
        args += [mod, mod]
    in_specs += [
        pl.BlockSpec((None, d, tn), lambda i, j: (fox_layer, 0, j)),
        _resident((None, d, LANES), lambda i, j: (fox_layer, 0, n_out // LANES)),
        _resident((1, LANES), lambda i, j: (0, 0)),
    ]
    args += [fox_w, fox_w, b_f]
    return pl.pallas_call(
        kern,
        grid=(batch * nm, n_out // tn),
        in_specs=in_specs,
        out_specs=[
            pl.BlockSpec((None, tn // LANES, tm, LANES), lambda i, j: (i // nm, j, i % nm, 0)),
            pl.BlockSpec((None, tm, LANES), lambda i, j: (i // nm, i % nm, 0)),
        ],
        out_shape=[
            jax.ShapeDtypeStruct((batch, slots, seq, LANES), BF16),
            jax.ShapeDtypeStruct((batch, seq, LANES), F32),
        ],
        scratch_shapes=[pltpu.VMEM((tm, d), BF16)] if modulate else [],
        compiler_params=_params(("parallel", "arbitrary")),
        name="fox_inproj",
    )(*args)


def _fox_cum_kernel(fl_ref, col_ref, row_ref, *, heads):
    seq = fl_ref.shape[0]
    x = fl_ref[...]
    log_f = (jnp.minimum(x, 0.0) - jnp.log1p(jnp.exp(-jnp.abs(x)))) * LOG2E
    r = lax.broadcasted_iota(jnp.int32, (LANES, LANES), 0)
    c = lax.broadcasted_iota(jnp.int32, (LANES, LANES), 1)
    tri = (c <= r).astype(BF16)
    carry = jnp.zeros((1, LANES), F32)
    for blk in range(seq // LANES):
        xb = log_f[blk * LANES:(blk + 1) * LANES]
        hi = xb.astype(BF16)
        rem = xb - hi.astype(F32)
        mid = rem.astype(BF16)
        lo = (rem - mid.astype(F32)).astype(BF16)
        cb = _dot(tri, hi) + _dot(tri, mid) + _dot(tri, lo) + carry
        col_ref[blk * LANES:(blk + 1) * LANES, :] = cb
        carry = cb[LANES - 1:LANES, :]
    row_ref[...] = col_ref[...].T[:heads]


def _fox_cum(fl, heads):
    batch, seq, _ = fl.shape
    return pl.pallas_call(
        functools.partial(_fox_cum_kernel, heads=heads),
        grid=(batch,),
        in_specs=[pl.BlockSpec((None, seq, LANES), lambda b: (b, 0, 0))],
        out_specs=[
            pl.BlockSpec((None, seq, LANES), lambda b: (b, 0, 0)),
            pl.BlockSpec((None, heads, seq), lambda b: (b, 0, 0)),
        ],
        out_shape=[
            jax.ShapeDtypeStruct((batch, seq, LANES), F32),
            jax.ShapeDtypeStruct((batch, heads, seq), F32),
        ],
        compiler_params=_params(("parallel",)),
        name="fox_cum",
    )(fl)


def _fox_attn_kernel(q_ref, k_ref, v_ref, crow_ref, ccol_ref, o_ref, *, heads_per_step, blk):
    seq = q_ref.shape[1]
    hb = pl.program_id(1)
    lane = lax.broadcasted_iota(jnp.int32, (blk, LANES), 1)
    rows = lax.broadcasted_iota(jnp.int32, (blk, blk), 0)
    cols = lax.broadcasted_iota(jnp.int32, (blk, blk), 1)
    causal = cols <= rows

    def scores(hh, qi):
        h = hb * heads_per_step + hh
        q0 = qi * blk
        lk = q0 + blk
        s = _dot_nt(q_ref[hh, q0:lk, :], k_ref[hh, :lk, :]) - crow_ref[pl.ds(h, 1), :lk]
        diag = jnp.where(causal, s[:, q0:], MASK_VALUE)
        return diag if qi == 0 else jnp.concatenate([s[:, :q0], diag], axis=1)

    def finish(hh, qi, s):
        h = hb * heads_per_step + hh
        q0 = qi * blk
        lk = q0 + blk
        cq = jnp.sum(jnp.where(lane == h, ccol_ref[q0:lk, :], 0.0), axis=-1, keepdims=True)
        m = jnp.max(s, axis=-1, keepdims=True) + cq
        p = jnp.exp2(s + (cq - m))
        l = jnp.sum(p, axis=-1, keepdims=True)
        acc = _dot(p.astype(BF16), v_ref[hh, :lk, :])
        o_ref[q0:lk, hh * LANES:(hh + 1) * LANES] = (acc / l).astype(BF16)

    work = [(hh, qi) for hh in range(heads_per_step) for qi in range(seq // blk)]
    s_next = scores(*work[0])
    for idx, item in enumerate(work):
        s_cur = s_next
        if idx + 1 < len(work):
            s_next = scores(*work[idx + 1])
        finish(*item, s_cur)


def _fox_attn(qkv, crow, ccol, *, heads):
    batch, slots, seq, dh = qkv.shape
    assert slots == 3 * heads and dh == LANES
    hps = 2 if heads % 2 == 0 else 1
    groups = heads // hps
    blk = min(256, seq)
    kern = functools.partial(_fox_attn_kernel, heads_per_step=hps, blk=blk)
    head_spec = lambda off: pl.BlockSpec((None, hps, seq, dh), lambda b, g: (b, off + g, 0, 0))
    return pl.pallas_call(
        kern,
        grid=(batch, groups),
        in_specs=[
            head_spec(0), head_spec(groups), head_spec(2 * groups),
            pl.BlockSpec((None, heads, seq), lambda b, g: (b, 0, 0)),
            pl.BlockSpec((None, seq, LANES), lambda b, g: (b, 0, 0)),
        ],
        out_specs=pl.BlockSpec((None, seq, hps * dh), lambda b, g: (b, 0, g)),
        out_shape=jax.ShapeDtypeStruct((batch, seq, heads * dh), BF16),
        compiler_params=_params(("parallel", "parallel")),
        name="fox_attn",
    )(qkv, qkv, qkv, crow, ccol)


def _rope_table_kernel(pos_ref, inv_ref, o_ref):
    pos = pos_ref[...].astype(F32)
    ang = pos * inv_ref[...]
    cos = jnp.cos(ang)
    sin = jnp.sin(ang)
    idx = lax.broadcasted_iota(jnp.int32, ang.shape, 1) % SWA_HEAD_DIM
    o_ref[0] = jnp.where(idx < ROPE_DIM, cos, 1.0)
    o_ref[1] = jnp.where((idx >= ROPE_HALF) & (idx < ROPE_DIM), sin, 0.0)
    o_ref[2] = jnp.where(idx < ROPE_HALF, -sin, 0.0)


def _rope_tables(positions):
    batch, seq = positions.shape
    inv_freq = ROPE_THETA ** (-jnp.arange(0, ROPE_DIM, 2, dtype=F32) / ROPE_DIM)
    idx = jnp.arange(LANES) % SWA_HEAD_DIM
    inv_lane = jnp.where(idx < ROPE_DIM, inv_freq[idx % ROPE_HALF], 0.0).reshape(1, LANES)
    return pl.pallas_call(
        _rope_table_kernel,
        grid=(batch,),
        in_specs=[
            pl.BlockSpec((None, seq, 1), lambda b: (b, 0, 0)),
            pl.BlockSpec((1, LANES), lambda b: (0, 0)),
        ],
        out_specs=pl.BlockSpec((3, None, seq, LANES), lambda b: (0, b, 0, 0)),
        out_shape=jax.ShapeDtypeStruct((3, batch, seq, LANES), F32),
        compiler_params=_params(("parallel",)),
        name="rope_tables",
    )(positions.reshape(batch, seq, 1), inv_lane)


def _inproj_swa_kernel(h_ref, wq_ref, wkv_ref, rt_ref, q_ref, kv_ref, *, q_tiles, q_scale, k_width):
    j = pl.program_id(1)
    seq = h_ref.shape[0]
    chunk = min(512, seq)
    n_chunks = seq // chunk

    def project(w_ref, store):
        w = w_ref[...].astype(BF16)
        acc_next = _dot(h_ref[0:chunk, :], w)
        for c in range(n_chunks):
            rows = slice(c * chunk, (c + 1) * chunk)
            acc = acc_next
            if c + 1 < n_chunks:
                acc_next = _dot(h_ref[(c + 1) * chunk:(c + 2) * chunk, :], w)
            cos = rt_ref[0, rows, :]
            sin_lo = rt_ref[1, rows, :]
            sin_hi = rt_ref[2, rows, :]
            for cc in range(acc.shape[1] // LANES):
                a = acc[:, cc * LANES:(cc + 1) * LANES]
                rot = (a * cos + pltpu.roll(a, ROPE_HALF, 1) * sin_lo
                       + pltpu.roll(a, LANES - ROPE_HALF, 1) * sin_hi)
                store(rows, cc, a, rot)

    @pl.when(j < q_tiles)
    def _():
        def store(rows, cc, a, rot):
            q_ref[rows, cc * LANES:(cc + 1) * LANES] = (rot * q_scale).astype(BF16)
        project(wq_ref, store)

    @pl.when(j == q_tiles)
    def _():
        def store(rows, cc, a, rot):
            kv_ref[rows, cc * LANES:(cc + 1) * LANES] = (rot if cc * LANES < k_width else a).astype(BF16)
        project(wkv_ref, store)


def _inproj_swa(h, swa_w_in, swa_layer, rope_tables, *, q_heads):
    batch, seq, d = h.shape
    nq = q_heads * SWA_HEAD_DIM
    kvw = swa_w_in.shape[2] - nq
    assert (kvw // 2) % LANES == 0
    tn = min(512, nq)
    assert nq % tn == 0 and nq % kvw == 0
    q_tiles = nq // tn
    kern = functools.partial(_inproj_swa_kernel, q_tiles=q_tiles,
                             q_scale=LOG2E * float(SWA_HEAD_DIM) ** -0.5, k_width=kvw // 2)
    return pl.pallas_call(
        kern,
        grid=(batch, q_tiles + 1),
        in_specs=[
            pl.BlockSpec((None, seq, d), lambda b, j: (b, 0, 0)),
            pl.BlockSpec((None, d, tn), lambda b, j: (swa_layer, 0, jnp.minimum(j, q_tiles - 1))),
            _resident((None, d, kvw), lambda b, j: (swa_layer, 0, nq // kvw)),
            pl.BlockSpec((3, None, seq, LANES), lambda b, j: (0, b, 0, 0)),
        ],
        out_specs=[
            pl.BlockSpec((None, seq, tn), lambda b, j: (b, 0, jnp.minimum(j, q_tiles - 1))),
            pl.BlockSpec((None, seq, kvw), lambda b, j: (b, 0, 0)),
        ],
        out_shape=[
            jax.ShapeDtypeStruct((batch, seq, nq), BF16),
            jax.ShapeDtypeStruct((batch, seq, kvw), BF16),
        ],
        compiler_params=_params(("parallel", "arbitrary")),
        name="swa_inproj",
    )(h, swa_w_in, swa_w_in, rope_tables)


def _swa_attn_kernel(sink_ref, q_ref, k_ref, v_ref, o_ref, k2_scr, vlo_scr, vhi_scr, *, group, layer):
    seq = q_ref.shape[0]
    blk = SWA_WINDOW
    g = pl.program_id(1)
    lane_s = lax.broadcasted_iota(jnp.int32, (seq, LANES), 1)
    hi_half = lane_s >= SWA_HEAD_DIM
    own_half = (lane_s // SWA_HEAD_DIM) == (g % 2)
    kp = k_ref[...].astype(F32)
    vp = v_ref[...].astype(F32)
    k2_scr[...] = jnp.where(own_half, kp, pltpu.roll(kp, SWA_HEAD_DIM, 1)).astype(BF16)
    v2 = jnp.where(own_half, vp, pltpu.roll(vp, SWA_HEAD_DIM, 1))
    vlo_scr[...] = jnp.where(hi_half, 0.0, v2).astype(BF16)
    vhi_scr[...] = jnp.where(hi_half, v2, 0.0).astype(BF16)

    lane_q = lax.broadcasted_iota(jnp.int32, (blk, LANES), 1) >= SWA_HEAD_DIM
    r = lax.broadcasted_iota(jnp.int32, (blk, 2 * blk), 0)
    c = lax.broadcasted_iota(jnp.int32, (blk, 2 * blk), 1)

    def q_block(n, _):
        q0 = pl.multiple_of(n * blk, blk)
        ks = pl.multiple_of(jnp.maximum(n - 1, 0) * blk, blk)
        rel = (q0 + r) - (ks + c)
        bias = jnp.where((rel >= 0) & (rel < SWA_WINDOW), 0.0, MASK_VALUE)
        k2 = k2_scr[pl.ds(ks, 2 * blk), :]
        vlo = vlo_scr[pl.ds(ks, 2 * blk), :]
        vhi = vhi_scr[pl.ds(ks, 2 * blk), :]
        for pair in range(group // 2):
            qp = q_ref[pl.ds(q0, blk), pair * LANES:(pair + 1) * LANES].astype(F32)
            out = jnp.zeros((blk, LANES), F32)
            for e in range(2):
                sink = sink_ref[layer, g * group + 2 * pair + e] * LOG2E
                qm = jnp.where(lane_q if e else jnp.logical_not(lane_q), qp, 0.0).astype(BF16)
                s = _dot_nt(qm, k2) + bias
                m = jnp.maximum(jnp.max(s, axis=-1, keepdims=True), sink)
                p = jnp.exp2(s - m)
                denom = jnp.sum(p, axis=-1, keepdims=True) + jnp.exp2(sink - m)
                out = out + _dot(p.astype(BF16), vhi if e else vlo) / denom
            o_ref[pl.ds(q0, blk), pair * LANES:(pair + 1) * LANES] = out.astype(BF16)
        return 0

    lax.fori_loop(0, seq // blk, q_block, 0, unroll=4)


def _swa_attn(q, kv, swa_sinks, swa_layer):
    batch, seq, nq = q.shape
    kvw = kv.shape[2]
    kv_heads = kvw // (2 * SWA_HEAD_DIM)
    q_heads = nq // SWA_HEAD_DIM
    group = q_heads // kv_heads
    gw = group * SWA_HEAD_DIM
    assert group % 2 == 0 and gw % LANES == 0 and kv_heads % 2 == 0
    v_off = kv_heads // 2
    return pl.pallas_call(
        functools.partial(_swa_attn_kernel, group=group, layer=swa_layer),
        grid=(batch, kv_heads),
        in_specs=[
            pl.BlockSpec(memory_space=pltpu.SMEM),
            pl.BlockSpec((None, seq, gw), lambda b, g: (b, 0, g)),
            pl.BlockSpec((None, seq, LANES), lambda b, g: (b, 0, g // 2)),
            pl.BlockSpec((None, seq, LANES), lambda b, g: (b, 0, v_off + g // 2)),
        ],
        out_specs=pl.BlockSpec((None, seq, gw), lambda b, g: (b, 0, g)),
        out_shape=jax.ShapeDtypeStruct((batch, seq, nq), BF16),
        scratch_shapes=[pltpu.VMEM((seq, LANES), BF16)] * 3,
        compiler_params=_params(("parallel", "arbitrary")),
        name="swa_attn",
    )(swa_sinks, q, kv, kv)


def _proj_ln_kernel(*refs, alpha, rows_per_batch, emit_next, cast_w, sub):
    refs = list(refs)
    w_scr = refs.pop() if cast_w else None
    if emit_next:
        a_ref, w_ref, x_ref, g_ref, lg_ref, lb_ref, sc_ref, sh_ref, xo_ref, ho_ref = refs
    else:
        a_ref, w_ref, x_ref, g_ref, lg_ref, lb_ref, xo_ref = refs
    if cast_w:
        @pl.when(pl.program_id(0) == 0)
        def _():
            w_scr[...] = w_ref[...].astype(BF16)
        w_ref = w_scr
    tm = a_ref.shape[0]
    b = (pl.program_id(0) * tm) // rows_per_batch
    gate = 1.0 + g_ref[pl.ds(b, 1), :]
    ln_g = lg_ref[...]
    ln_b = lb_ref[...]
    if emit_next:
        nsc = 1.0 + sc_ref[pl.ds(b, 1), :]
        nsh = sh_ref[pl.ds(b, 1), :]
    for c in range(tm // sub):
        sl = slice(c * sub, (c + 1) * sub)
        y = _dot(a_ref[sl, :], w_ref[...])
        z = alpha * x_ref[sl, :] + gate * y
        mu = jnp.mean(z, axis=-1, keepdims=True)
        zc = z - mu
        var = jnp.mean(zc * zc, axis=-1, keepdims=True)
        xn = zc * lax.rsqrt(var + LN_EPS) * ln_g + ln_b
        xo_ref[sl, :] = xn
        if emit_next:
            ho_ref[sl, :] = (xn * nsc + nsh).astype(BF16)


def _proj_ln(a, w, w_layer, x, mod, layer, gate_chunk, ln_g, ln_b, next_mod, *, alpha, tm):
    batch, seq, d = x.shape
    k = a.shape[-1]
    rows = batch * seq
    tm = min(tm, seq)
    emit_next = next_mod is not None
    cast_w = w.dtype != BF16
    kern = functools.partial(_proj_ln_kernel, alpha=alpha, rows_per_batch=seq, emit_next=emit_next,
                             cast_w=cast_w, sub=min(128, tm))
    row_spec = lambda width: pl.BlockSpec((tm, width), lambda i: (i, 0))
    vec_spec = _resident((None, 1, d), lambda i: (layer, 0, 0))
    in_specs = [row_spec(k), _resident((None, k, d), lambda i: (w_layer, 0, 0)), row_spec(d),
                _mod_spec(mod, layer, gate_chunk), vec_spec, vec_spec]
    depth = ln_g.shape[0]
    args = [a.reshape(rows, k), w, x.reshape(rows, d), mod, ln_g.reshape(depth, 1, d), ln_b.reshape(depth, 1, d)]
    out_specs = [row_spec(d)]
    out_shape = [jax.ShapeDtypeStruct((rows, d), F32)]
    if emit_next:
        nl, nsc, nsh = next_mod
        in_specs += [_mod_spec(mod, nl, nsc), _mod_spec(mod, nl, nsh)]
        args += [mod, mod]
        out_specs.append(row_spec(d))
        out_shape.append(jax.ShapeDtypeStruct((rows, d), BF16))
    outs = pl.pallas_call(
        kern,
        grid=(rows // tm,),
        in_specs=in_specs,
        out_specs=out_specs,
        out_shape=out_shape,
        scratch_shapes=[pltpu.VMEM((k, d), BF16)] if cast_w else [],
        compiler_params=_params(("arbitrary",)),
        name="proj_ln",
    )(*args)
    x_new = outs[0].reshape(batch, seq, d)
    return x_new, (outs[1].reshape(batch, seq, d) if emit_next else None)


def _ffn_up_kernel(h_ref, wg_ref, wv_ref, cwg_ref, cwv_ref, cbg_ref, cbv_ref, o_ref, *, chunk):
    seq = h_ref.shape[0]
    tn = wg_ref.shape[1]
    wg = wg_ref[...].astype(BF16)
    wv = wv_ref[...].astype(BF16)

    def conv(u, halo, cw_ref, cb_ref):
        ext = jnp.concatenate([halo, u], axis=0)
        u1 = pltpu.roll(ext, 1, 0)[SUBLANES:]
        u2 = pltpu.roll(ext, 2, 0)[SUBLANES:]
        return u2 * cw_ref[0:1, :] + u1 * cw_ref[1:2, :] + u * cw_ref[2:3, :] + cb_ref[...]

    def matmuls(c):
        hc = h_ref[c * chunk:(c + 1) * chunk, :]
        return _dot(hc, wg), _dot(hc, wv)

    n_chunks = seq // chunk
    halo_g = jnp.zeros((SUBLANES, tn), F32)
    halo_v = jnp.zeros((SUBLANES, tn), F32)
    u_next = matmuls(0)
    for c in range(n_chunks):
        sl = slice(c * chunk, (c + 1) * chunk)
        ug, uv = u_next
        if c + 1 < n_chunks:
            u_next = matmuls(c + 1)
        cg = conv(ug, halo_g, cwg_ref, cbg_ref)
        cv = conv(uv, halo_v, cwv_ref, cbv_ref)
        halo_g = ug[chunk - SUBLANES:]
        halo_v = uv[chunk - SUBLANES:]
        o_ref[sl, :] = (cg * jax.nn.sigmoid(cg) * cv).astype(BF16)


def _ffn_up(h, ffn_w_up, conv_w, conv_b, layer):
    batch, seq, d = h.shape
    depth, _, two_f = ffn_w_up.shape
    d_ff = two_f // 2
    tn = 512 if d_ff % 512 == 0 else LANES
    nt = d_ff // tn
    chunk = min(512, seq)
    lo = lambda b, j: (layer, 0, j)
    hi = lambda b, j: (layer, 0, nt + j)
    conv_b = conv_b.reshape(depth, 1, two_f)
    return pl.pallas_call(
        functools.partial(_ffn_up_kernel, chunk=chunk),
        grid=(batch, nt),
        in_specs=[
            pl.BlockSpec((None, seq, d), lambda b, j: (b, 0, 0)),
            pl.BlockSpec((None, d, tn), lo), pl.BlockSpec((None, d, tn), hi),
            pl.BlockSpec((None, CONV_WIDTH, tn), lo), pl.BlockSpec((None, CONV_WIDTH, tn), hi),
            pl.BlockSpec((None, 1, tn), lo), pl.BlockSpec((None, 1, tn), hi),
        ],
        out_specs=pl.BlockSpec((None, seq, tn), lambda b, j: (b, 0, j)),
        out_shape=jax.ShapeDtypeStruct((batch, seq, d_ff), BF16),
        compiler_params=_params(("parallel", "arbitrary")),
        name="ffn_up",
    )(h, ffn_w_up, ffn_w_up, conv_w, conv_w, conv_b, conv_b)


def kernel(x, c, positions, fox_w_in, fox_b_f, fox_w_o, swa_w_in, swa_sinks, swa_w_o, ada_w, ada_b,
           ffn_w_up, ffn_conv_w, ffn_conv_b, ffn_w_down, ln_mix_g, ln_mix_b, ln_ffn_g, ln_ffn_b):
    batch, seq, d = x.shape
    depth = ada_w.shape[0]
    alpha = (2.0 * depth) ** 0.25
    fox_heads = fox_b_f.shape[1]
    fox_dh = d // fox_heads

    mod = _ada(c, ada_w, ada_b)
    rope_tables = _rope_tables(positions) if depth > 1 else None
    w_down = ffn_w_down.astype(BF16)
    fox_w = _fox_w_bf16(fox_w_in)

    h = None
    for i in range(depth):
        j = i // 2
        if i % 2 == 0:
            b_f = jnp.pad(fox_b_f[j], (0, LANES - fox_heads)).reshape(1, LANES)
            if h is None:
                qkv, fl = _inproj_fox(x, mod, i, fox_w, j, b_f, head_dim=fox_dh)
            else:
                qkv, fl = _inproj_fox(h, None, i, fox_w, j, b_f, head_dim=fox_dh)
            ccol, crow = _fox_cum(fl, fox_heads)
            o = _fox_attn(qkv, crow, ccol, heads=fox_heads)
            w_o = fox_w_o
        else:
            if h is None:
                raise NotImplementedError("SWA as the first layer")
            q, kv = _inproj_swa(h, swa_w_in, j, rope_tables, q_heads=swa_sinks.shape[1])
            o = _swa_attn(q, kv, swa_sinks, j)
            w_o = swa_w_o
        x, h2 = _proj_ln(o, w_o, j, x, mod, i, G1, ln_mix_g, ln_mix_b, (i, SC2, SH2), alpha=alpha, tm=512)
        hmid = _ffn_up(h2, ffn_w_up, ffn_conv_w, ffn_conv_b, i)
        next_mod = (i + 1, SC1, SH1) if i + 1 < depth else None
        x, h = _proj_ln(hmid, w_down, i, x, mod, i, G2, ln_ffn_g, ln_ffn_b, next_mod, alpha=alpha, tm=256)
    return x
```

```python
import functools

import jax
import jax.numpy as jnp
from jax import lax
from jax.experimental import pallas as pl
from jax.experimental.pallas import tpu as pltpu

F32 = jnp.float32
BF16 = jnp.bfloat16

LANES = 128
SUBLANES = 8
VMEM_LIMIT_BYTES = 56 * 1024 * 1024

LN_EPS = 1e-5
ROPE_THETA = 500000.0
ROPE_DIM = 16
ROPE_HALF = ROPE_DIM // 2
SWA_HEAD_DIM = 64
SWA_WINDOW = 128
CONV_WIDTH = 3
MASK_VALUE = -1e30
LOG2E = 1.4426950408889634

SH1, SC1, G1, SH2, SC2, G2 = range(6)


def _params(semantics):
    return pltpu.CompilerParams(dimension_semantics=semantics, vmem_limit_bytes=VMEM_LIMIT_BYTES)


def _dot(a, b):
    return jnp.dot(a, b, preferred_element_type=F32)


def _dot_nt(a, b):
    return lax.dot_general(a, b, (((1,), (1,)), ((), ())), preferred_element_type=F32)


def _resident(block_shape, index_map):
    return pl.BlockSpec(block_shape, index_map, pipeline_mode=pl.Buffered(1))


def _mod_spec(mod, layer, chunk):
    _, _, batch, d = mod.shape
    return _resident((None, None, batch, d), lambda *_: (layer, chunk, 0, 0))


def _ada_kernel(c_ref, w_ref, b_ref, o_ref):
    c = c_ref[...]
    c_act = (c * jax.nn.sigmoid(c)).astype(BF16)
    o_ref[...] = _dot(c_act, w_ref[...].astype(BF16)) + b_ref[...]


def _ada(c, ada_w, ada_b):
    depth, d, six_d = ada_w.shape
    batch = c.shape[0]
    tn = min(1024, d)
    nt = d // tn
    return pl.pallas_call(
        _ada_kernel,
        grid=(depth, 6, nt),
        in_specs=[
            pl.BlockSpec((batch, d), lambda i, k, j: (0, 0)),
            pl.BlockSpec((None, d, tn), lambda i, k, j: (i, 0, k * nt + j)),
            pl.BlockSpec((None, 1, tn), lambda i, k, j: (i, 0, k * nt + j)),
        ],
        out_specs=pl.BlockSpec((None, None, batch, tn), lambda i, k, j: (i, k, 0, j)),
        out_shape=jax.ShapeDtypeStruct((depth, 6, batch, d), F32),
        compiler_params=_params(("parallel", "parallel", "parallel")),
        name="ada_mod",
    )(c, ada_w, ada_b.reshape(depth, 1, six_d))


def _cast_cols_kernel(wt_ref, o_ref, *, valid_cols):
    tn = wt_ref.shape[0]
    w = wt_ref[...].T
    col = pl.program_id(1) * tn + lax.broadcasted_iota(jnp.int32, w.shape, 1)
    o_ref[...] = jnp.where(col < valid_cols, w, 0.0).astype(BF16)


def _fox_w_bf16(fox_w_in):
    n, d, cols = fox_w_in.shape
    tn = 512
    nt = pl.cdiv(cols, tn)
    return pl.pallas_call(
        functools.partial(_cast_cols_kernel, valid_cols=cols),
        grid=(n, nt),
        in_specs=[pl.BlockSpec((None, tn, d), lambda l, j: (l, j, 0))],
        out_specs=pl.BlockSpec((None, d, tn), lambda l, j: (l, 0, j)),
        out_shape=jax.ShapeDtypeStruct((n, d, nt * tn), BF16),
        compiler_params=_params(("parallel", "parallel")),
        name="fox_w_cast",
    )(jnp.swapaxes(fox_w_in, 1, 2))


def _inproj_fox_kernel(*refs, modulate, tiles_per_seq, q_tiles, q_scale):
    if modulate:
        x_ref, sc_ref, sh_ref, w_ref, wf_ref, bf_ref, qkv_ref, fl_ref, h_scr = refs
    else:
        x_ref, w_ref, wf_ref, bf_ref, qkv_ref, fl_ref = refs
    i = pl.program_id(0)
    j = pl.program_id(1)

    if modulate:
        @pl.when(j == 0)
        def _():
            b = i // tiles_per_seq
            sc = sc_ref[pl.ds(b, 1), :]
            sh = sh_ref[pl.ds(b, 1), :]
            h_scr[...] = (x_ref[...] * (1.0 + sc) + sh).astype(BF16)
        h_ref = h_scr
    else:
        h_ref = x_ref

    @pl.when(j == 0)
    def _():
        fl_ref[...] = _dot(h_ref[...], wf_ref[...]) + bf_ref[...]

    acc = _dot(h_ref[...], w_ref[...])
    acc = acc * jnp.where(j < q_tiles, q_scale, 1.0)
    for hh in range(acc.shape[1] // LANES):
        qkv_ref[hh] = acc[:, hh * LANES:(hh + 1) * LANES].astype(BF16)


def _inproj_fox(x_or_h, mod, layer, fox_w, fox_layer, b_f, *, head_dim):
    batch, seq, d = x_or_h.shape
    modulate = mod is not None
    n_out = 3 * d
    assert head_dim == LANES
    tn = min(1024, d)
    tm = min(1024 if modulate else 2048, seq)
    nm = seq // tm
    slots = n_out // LANES
    kern = functools.partial(_inproj_fox_kernel, modulate=modulate, tiles_per_seq=nm,
                             q_tiles=d // tn, q_scale=LOG2E * float(head_dim) ** -0.5)
    in_specs = [pl.BlockSpec((None, tm, d), lambda i, j: (i // nm, i % nm, 0))]
    args = [x_or_h]
    if modulate:
        in_specs += [_mod_spec(mod, layer, SC1), _mod_spec(mod, layer, SH1)]
        args += [mod, mod]
    in_specs += [
        pl.BlockSpec((None, d, tn), lambda i, j: (fox_layer, 0, j)),
        _resident((None, d, LANES), lambda i, j: (fox_layer, 0, n_out // LANES)),
        _resident((1, LANES), lambda i, j: (0, 0)),
    ]
    args += [fox_w, fox_w, b_f]
    return pl.pallas_call(
        kern,
        grid=(batch * nm, n_out // tn),
        in_specs=in_specs,
        out_specs=[
            pl.BlockSpec((None, tn // LANES, tm, LANES), lambda i, j: (i // nm, j, i % nm, 0)),
            pl.BlockSpec((None, tm, LANES), lambda i, j: (i // nm, i % nm, 0)),
        ],
        out_shape=[
            jax.ShapeDtypeStruct((batch, slots, seq, LANES), BF16),
            jax.ShapeDtypeStruct((batch, seq, LANES), F32),
        ],
        scratch_shapes=[pltpu.VMEM((tm, d), BF16)] if modulate else [],
        compiler_params=_params(("parallel", "arbitrary")),
        name="fox_inproj",
    )(*args)


def _fox_cum_kernel(fl_ref, col_ref, row_ref, *, heads):
    seq = fl_ref.shape[0]
    x = fl_ref[...]
    log_f = (jnp.minimum(x, 0.0) - jnp.log1p(jnp.exp(-jnp.abs(x)))) * LOG2E
    r = lax.broadcasted_iota(jnp.int32, (LANES, LANES), 0)
    c = lax.broadcasted_iota(jnp.int32, (LANES, LANES), 1)
    tri = (c <= r).astype(BF16)
    carry = jnp.zeros((1, LANES), F32)
    for blk in range(seq // LANES):
        xb = log_f[blk * LANES:(blk + 1) * LANES]
        hi = xb.astype(BF16)
        rem = xb - hi.astype(F32)
        mid = rem.astype(BF16)
        lo = (rem - mid.astype(F32)).astype(BF16)
        cb = _dot(tri, hi) + _dot(tri, mid) + _dot(tri, lo) + carry
        col_ref[blk * LANES:(blk + 1) * LANES, :] = cb
        carry = cb[LANES - 1:LANES, :]
    row_ref[...] = col_ref[...].T[:heads]


def _fox_cum(fl, heads):
    batch, seq, _ = fl.shape
    return pl.pallas_call(
        functools.partial(_fox_cum_kernel, heads=heads),
        grid=(batch,),
        in_specs=[pl.BlockSpec((None, seq, LANES), lambda b: (b, 0, 0))],
        out_specs=[
            pl.BlockSpec((None, seq, LANES), lambda b: (b, 0, 0)),
            pl.BlockSpec((None, heads, seq), lambda b: (b, 0, 0)),
        ],
        out_shape=[
            jax.ShapeDtypeStruct((batch, seq, LANES), F32),
            jax.ShapeDtypeStruct((batch, heads, seq), F32),
        ],
        compiler_params=_params(("parallel",)),
        name="fox_cum",
    )(fl)


def _fox_attn_kernel(q_ref, k_ref, v_ref, crow_ref, ccol_ref, o_ref, *, heads_per_step, blk):
    seq = q_ref.shape[1]
    hb = pl.program_id(1)
    lane = lax.broadcasted_iota(jnp.int32, (blk, LANES), 1)
    rows = lax.broadcasted_iota(jnp.int32, (blk, blk), 0)
    cols = lax.broadcasted_iota(jnp.int32, (blk, blk), 1)
    causal = cols <= rows

    def scores(hh, qi):
        h = hb * heads_per_step + hh
        q0 = qi * blk
        lk = q0 + blk
        s = _dot_nt(q_ref[hh, q0:lk, :], k_ref[hh, :lk, :]) - crow_ref[pl.ds(h, 1), :lk]
        diag = jnp.where(causal, s[:, q0:], MASK_VALUE)
        return diag if qi == 0 else jnp.concatenate([s[:, :q0], diag], axis=1)

    def finish(hh, qi, s):
        h = hb * heads_per_step + hh
        q0 = qi * blk
        lk = q0 + blk
        cq = jnp.sum(jnp.where(lane == h, ccol_ref[q0:lk, :], 0.0), axis=-1, keepdims=True)
        m = jnp.max(s, axis=-1, keepdims=True) + cq
        p = jnp.exp2(s + (cq - m))
        l = jnp.sum(p, axis=-1, keepdims=True)
        acc = _dot(p.astype(BF16), v_ref[hh, :lk, :])
        o_ref[q0:lk, hh * LANES:(hh + 1) * LANES] = (acc / l).astype(BF16)

    work = [(hh, qi) for hh in range(heads_per_step) for qi in range(seq // blk)]
    s_next = scores(*work[0])
    for idx, item in enumerate(work):
        s_cur = s_next
        if idx + 1 < len(work):
            s_next = scores(*work[idx + 1])
        finish(*item, s_cur)


def _fox_attn(qkv, crow, ccol, *, heads):
    batch, slots, seq, dh = qkv.shape
    assert slots == 3 * heads and dh == LANES
    hps = 4 if heads % 4 == 0 else 1
    groups = heads // hps
    blk = min(256, seq)
    kern = functools.partial(_fox_attn_kernel, heads_per_step=hps, blk=blk)
    head_spec = lambda off: pl.BlockSpec((None, hps, seq, dh), lambda b, g: (b, off + g, 0, 0))
    return pl.pallas_call(
        kern,
        grid=(batch, groups),
        in_specs=[
            head_spec(0), head_spec(groups), head_spec(2 * groups),
            pl.BlockSpec((None, heads, seq), lambda b, g: (b, 0, 0)),
            pl.BlockSpec((None, seq, LANES), lambda b, g: (b, 0, 0)),
        ],
        out_specs=pl.BlockSpec((None, seq, hps * dh), lambda b, g: (b, 0, g)),
        out_shape=jax.ShapeDtypeStruct((batch, seq, heads * dh), BF16),
        compiler_params=_params(("parallel", "parallel")),
        name="fox_attn",
    )(qkv, qkv, qkv, crow, ccol)


def _rope_table_kernel(pos_ref, inv_ref, o_ref):
    pos = pos_ref[...].astype(F32)
    ang = pos * inv_ref[...]
    cos = jnp.cos(ang)
    sin = jnp.sin(ang)
    idx = lax.broadcasted_iota(jnp.int32, ang.shape, 1) % SWA_HEAD_DIM
    o_ref[0] = jnp.where(idx < ROPE_DIM, cos, 1.0)
    o_ref[1] = jnp.where((idx >= ROPE_HALF) & (idx < ROPE_DIM), sin, 0.0)
    o_ref[2] = jnp.where(idx < ROPE_HALF, -sin, 0.0)


def _rope_tables(positions):
    batch, seq = positions.shape
    inv_freq = ROPE_THETA ** (-jnp.arange(0, ROPE_DIM, 2, dtype=F32) / ROPE_DIM)
    idx = jnp.arange(LANES) % SWA_HEAD_DIM
    inv_lane = jnp.where(idx < ROPE_DIM, inv_freq[idx % ROPE_HALF], 0.0).reshape(1, LANES)
    return pl.pallas_call(
        _rope_table_kernel,
        grid=(batch,),
        in_specs=[
            pl.BlockSpec((None, seq, 1), lambda b: (b, 0, 0)),
            pl.BlockSpec((1, LANES), lambda b: (0, 0)),
        ],
        out_specs=pl.BlockSpec((3, None, seq, LANES), lambda b: (0, b, 0, 0)),
        out_shape=jax.ShapeDtypeStruct((3, batch, seq, LANES), F32),
        compiler_params=_params(("parallel",)),
        name="rope_tables",
    )(positions.reshape(batch, seq, 1), inv_lane)


def _inproj_swa_kernel(h_ref, wq_ref, wkv_ref, rt_ref, q_ref, kv_ref, *, q_tiles, q_scale, k_width):
    j = pl.program_id(1)
    seq = h_ref.shape[0]
    chunk = min(512, seq)
    n_chunks = seq // chunk

    def project(w_ref, store):
        w = w_ref[...].astype(BF16)
        acc_next = _dot(h_ref[0:chunk, :], w)
        for c in range(n_chunks):
            rows = slice(c * chunk, (c + 1) * chunk)
            acc = acc_next
            if c + 1 < n_chunks:
                acc_next = _dot(h_ref[(c + 1) * chunk:(c + 2) * chunk, :], w)
            cos = rt_ref[0, rows, :]
            sin_lo = rt_ref[1, rows, :]
            sin_hi = rt_ref[2, rows, :]
            for cc in range(acc.shape[1] // LANES):
                a = acc[:, cc * LANES:(cc + 1) * LANES]
                rot = (a * cos + pltpu.roll(a, ROPE_HALF, 1) * sin_lo
                       + pltpu.roll(a, LANES - ROPE_HALF, 1) * sin_hi)
                store(rows, cc, a, rot)

    @pl.when(j < q_tiles)
    def _():
        def store(rows, cc, a, rot):
            q_ref[rows, cc * LANES:(cc + 1) * LANES] = (rot * q_scale).astype(BF16)
        project(wq_ref, store)

    @pl.when(j == q_tiles)
    def _():
        def store(rows, cc, a, rot):
            kv_ref[rows, cc * LANES:(cc + 1) * LANES] = (rot if cc * LANES < k_width else a).astype(BF16)
        project(wkv_ref, store)


def _inproj_swa(h, swa_w_in, swa_layer, rope_tables, *, q_heads):
    batch, seq, d = h.shape
    nq = q_heads * SWA_HEAD_DIM
    kvw = swa_w_in.shape[2] - nq
    assert (kvw // 2) % LANES == 0
    tn = min(512, nq)
    assert nq % tn == 0 and nq % kvw == 0
    q_tiles = nq // tn
    kern = functools.partial(_inproj_swa_kernel, q_tiles=q_tiles,
                             q_scale=LOG2E * float(SWA_HEAD_DIM) ** -0.5, k_width=kvw // 2)
    return pl.pallas_call(
        kern,
        grid=(batch, q_tiles + 1),
        in_specs=[
            pl.BlockSpec((None, seq, d), lambda b, j: (b, 0, 0)),
            pl.BlockSpec((None, d, tn), lambda b, j: (swa_layer, 0, jnp.minimum(j, q_tiles - 1))),
            _resident((None, d, kvw), lambda b, j: (swa_layer, 0, nq // kvw)),
            pl.BlockSpec((3, None, seq, LANES), lambda b, j: (0, b, 0, 0)),
        ],
        out_specs=[
            pl.BlockSpec((None, seq, tn), lambda b, j: (b, 0, jnp.minimum(j, q_tiles - 1))),
            pl.BlockSpec((None, seq, kvw), lambda b, j: (b, 0, 0)),
        ],
        out_shape=[
            jax.ShapeDtypeStruct((batch, seq, nq), BF16),
            jax.ShapeDtypeStruct((batch, seq, kvw), BF16),
        ],
        compiler_params=_params(("parallel", "arbitrary")),
        name="swa_inproj",
    )(h, swa_w_in, swa_w_in, rope_tables)


def _swa_attn_kernel(sink_ref, q_ref, k_ref, v_ref, o_ref, klo_scr, khi_scr, vlo_scr, vhi_scr, *,
                     group, layer):
    seq = q_ref.shape[0]
    blk = SWA_WINDOW
    g = pl.program_id(1)
    lane_s = lax.broadcasted_iota(jnp.int32, (seq, LANES), 1)
    hi_half = lane_s >= SWA_HEAD_DIM
    own_half = (lane_s // SWA_HEAD_DIM) == (g % 2)
    kp = k_ref[...].astype(F32)
    vp = v_ref[...].astype(F32)
    k2 = jnp.where(own_half, kp, pltpu.roll(kp, SWA_HEAD_DIM, 1))
    v2 = jnp.where(own_half, vp, pltpu.roll(vp, SWA_HEAD_DIM, 1))
    klo_scr[...] = jnp.where(hi_half, 0.0, k2).astype(BF16)
    khi_scr[...] = jnp.where(hi_half, k2, 0.0).astype(BF16)
    vlo_scr[...] = jnp.where(hi_half, 0.0, v2).astype(BF16)
    vhi_scr[...] = jnp.where(hi_half, v2, 0.0).astype(BF16)

    r = lax.broadcasted_iota(jnp.int32, (blk, 2 * blk), 0)
    c = lax.broadcasted_iota(jnp.int32, (blk, 2 * blk), 1)
    rel_first = r - c
    rel_rest = r + blk - c
    bias_first = jnp.where((rel_first >= 0) & (rel_first < SWA_WINDOW), 0.0, MASK_VALUE)
    bias_rest = jnp.where((rel_rest >= 0) & (rel_rest < SWA_WINDOW), 0.0, MASK_VALUE)

    def q_block(n, _):
        q0 = pl.multiple_of(n * blk, blk)
        ks = pl.multiple_of(jnp.maximum(n - 1, 0) * blk, blk)
        bias = jnp.where(n == 0, bias_first, bias_rest)
        keys = (klo_scr[pl.ds(ks, 2 * blk), :], khi_scr[pl.ds(ks, 2 * blk), :])
        vals = (vlo_scr[pl.ds(ks, 2 * blk), :], vhi_scr[pl.ds(ks, 2 * blk), :])
        for pair in range(group // 2):
            qp = q_ref[pl.ds(q0, blk), pair * LANES:(pair + 1) * LANES]
            out = jnp.zeros((blk, LANES), F32)
            for e in range(2):
                sink = sink_ref[layer, g * group + 2 * pair + e] * LOG2E
                s = _dot_nt(qp, keys[e]) + bias
                m = jnp.maximum(jnp.max(s, axis=-1, keepdims=True), sink)
                p = jnp.exp2(s - m)
                denom = jnp.sum(p, axis=-1, keepdims=True) + jnp.exp2(sink - m)
                out = out + _dot(p.astype(BF16), vals[e]) / denom
            o_ref[pl.ds(q0, blk), pair * LANES:(pair + 1) * LANES] = out.astype(BF16)
        return 0

    lax.fori_loop(0, seq // blk, q_block, 0, unroll=8)


def _swa_attn(q, kv, swa_sinks, swa_layer):
    batch, seq, nq = q.shape
    kvw = kv.shape[2]
    kv_heads = kvw // (2 * SWA_HEAD_DIM)
    q_heads = nq // SWA_HEAD_DIM
    group = q_heads // kv_heads
    gw = group * SWA_HEAD_DIM
    assert group % 2 == 0 and gw % LANES == 0 and kv_heads % 2 == 0
    v_off = kv_heads // 2
    return pl.pallas_call(
        functools.partial(_swa_attn_kernel, group=group, layer=swa_layer),
        grid=(batch, kv_heads),
        in_specs=[
            pl.BlockSpec(memory_space=pltpu.SMEM),
            pl.BlockSpec((None, seq, gw), lambda b, g: (b, 0, g)),
            pl.BlockSpec((None, seq, LANES), lambda b, g: (b, 0, g // 2)),
            pl.BlockSpec((None, seq, LANES), lambda b, g: (b, 0, v_off + g // 2)),
        ],
        out_specs=pl.BlockSpec((None, seq, gw), lambda b, g: (b, 0, g)),
        out_shape=jax.ShapeDtypeStruct((batch, seq, nq), BF16),
        scratch_shapes=[pltpu.VMEM((seq, LANES), BF16)] * 4,
        compiler_params=_params(("parallel", "arbitrary")),
        name="swa_attn",
    )(swa_sinks, q, kv, kv)


def _proj_ln_kernel(*refs, alpha, rows_per_batch, emit_next, cast_w, sub):
    refs = list(refs)
    w_scr = refs.pop() if cast_w else None
    if emit_next:
        a_ref, w_ref, x_ref, g_ref, lg_ref, lb_ref, sc_ref, sh_ref, xo_ref, ho_ref = refs
    else:
        a_ref, w_ref, x_ref, g_ref, lg_ref, lb_ref, xo_ref = refs
    if cast_w:
        @pl.when(pl.program_id(0) == 0)
        def _():
            w_scr[...] = w_ref[...].astype(BF16)
        w_ref = w_scr
    tm = a_ref.shape[0]
    b = (pl.program_id(0) * tm) // rows_per_batch
    gate = 1.0 + g_ref[pl.ds(b, 1), :]
    ln_g = lg_ref[...]
    ln_b = lb_ref[...]
    if emit_next:
        nsc = 1.0 + sc_ref[pl.ds(b, 1), :]
        nsh = sh_ref[pl.ds(b, 1), :]
    for c in range(tm // sub):
        sl = slice(c * sub, (c + 1) * sub)
        y = _dot(a_ref[sl, :], w_ref[...])
        z = alpha * x_ref[sl, :] + gate * y
        mu = jnp.mean(z, axis=-1, keepdims=True)
        zc = z - mu
        var = jnp.mean(zc * zc, axis=-1, keepdims=True)
        xn = zc * lax.rsqrt(var + LN_EPS) * ln_g + ln_b
        xo_ref[sl, :] = xn
        if emit_next:
            ho_ref[sl, :] = (xn * nsc + nsh).astype(BF16)


def _proj_ln(a, w, w_layer, x, mod, layer, gate_chunk, ln_g, ln_b, next_mod, *, alpha, tm):
    batch, seq, d = x.shape
    k = a.shape[-1]
    rows = batch * seq
    tm = min(tm, seq)
    emit_next = next_mod is not None
    cast_w = w.dtype != BF16
    kern = functools.partial(_proj_ln_kernel, alpha=alpha, rows_per_batch=seq, emit_next=emit_next,
                             cast_w=cast_w, sub=min(128, tm))
    row_spec = lambda width: pl.BlockSpec((tm, width), lambda i: (i, 0))
    vec_spec = _resident((None, 1, d), lambda i: (layer, 0, 0))
    in_specs = [row_spec(k), _resident((None, k, d), lambda i: (w_layer, 0, 0)), row_spec(d),
                _mod_spec(mod, layer, gate_chunk), vec_spec, vec_spec]
    depth = ln_g.shape[0]
    args = [a.reshape(rows, k), w, x.reshape(rows, d), mod, ln_g.reshape(depth, 1, d), ln_b.reshape(depth, 1, d)]
    out_specs = [row_spec(d)]
    out_shape = [jax.ShapeDtypeStruct((rows, d), F32)]
    if emit_next:
        nl, nsc, nsh = next_mod
        in_specs += [_mod_spec(mod, nl, nsc), _mod_spec(mod, nl, nsh)]
        args += [mod, mod]
        out_specs.append(row_spec(d))
        out_shape.append(jax.ShapeDtypeStruct((rows, d), BF16))
    outs = pl.pallas_call(
        kern,
        grid=(rows // tm,),
        in_specs=in_specs,
        out_specs=out_specs,
        out_shape=out_shape,
        scratch_shapes=[pltpu.VMEM((k, d), BF16)] if cast_w else [],
        compiler_params=_params(("arbitrary",)),
        name="proj_ln",
    )(*args)
    x_new = outs[0].reshape(batch, seq, d)
    return x_new, (outs[1].reshape(batch, seq, d) if emit_next else None)


def _ffn_up_kernel(h_ref, wg_ref, wv_ref, cwg_ref, cwv_ref, cbg_ref, cbv_ref, o_ref, *, chunk):
    seq = h_ref.shape[0]
    tn = wg_ref.shape[1]
    wg = wg_ref[...].astype(BF16)
    wv = wv_ref[...].astype(BF16)

    def conv(u, halo, cw_ref, cb_ref):
        ext = jnp.concatenate([halo, u], axis=0)
        u1 = pltpu.roll(ext, 1, 0)[SUBLANES:]
        u2 = pltpu.roll(ext, 2, 0)[SUBLANES:]
        return u2 * cw_ref[0:1, :] + u1 * cw_ref[1:2, :] + u * cw_ref[2:3, :] + cb_ref[...]

    def matmuls(c):
        hc = h_ref[c * chunk:(c + 1) * chunk, :]
        return _dot(hc, wg), _dot(hc, wv)

    n_chunks = seq // chunk
    halo_g = jnp.zeros((SUBLANES, tn), F32)
    halo_v = jnp.zeros((SUBLANES, tn), F32)
    u_next = matmuls(0)
    for c in range(n_chunks):
        sl = slice(c * chunk, (c + 1) * chunk)
        ug, uv = u_next
        if c + 1 < n_chunks:
            u_next = matmuls(c + 1)
        cg = conv(ug, halo_g, cwg_ref, cbg_ref)
        cv = conv(uv, halo_v, cwv_ref, cbv_ref)
        halo_g = ug[chunk - SUBLANES:]
        halo_v = uv[chunk - SUBLANES:]
        o_ref[sl, :] = (cg * jax.nn.sigmoid(cg) * cv).astype(BF16)


def _ffn_up(h, ffn_w_up, conv_w, conv_b, layer):
    batch, seq, d = h.shape
    depth, _, two_f = ffn_w_up.shape
    d_ff = two_f // 2
    tn = 512 if d_ff % 512 == 0 else LANES
    nt = d_ff // tn
    chunk = min(512, seq)
    lo = lambda b, j: (layer, 0, j)
    hi = lambda b, j: (layer, 0, nt + j)
    conv_b = conv_b.reshape(depth, 1, two_f)
    return pl.pallas_call(
        functools.partial(_ffn_up_kernel, chunk=chunk),
        grid=(batch, nt),
        in_specs=[
            pl.BlockSpec((None, seq, d), lambda b, j: (b, 0, 0)),
            pl.BlockSpec((None, d, tn), lo), pl.BlockSpec((None, d, tn), hi),
            pl.BlockSpec((None, CONV_WIDTH, tn), lo), pl.BlockSpec((None, CONV_WIDTH, tn), hi),
            pl.BlockSpec((None, 1, tn), lo), pl.BlockSpec((None, 1, tn), hi),
        ],
        out_specs=pl.BlockSpec((None, seq, tn), lambda b, j: (b, 0, j)),
        out_shape=jax.ShapeDtypeStruct((batch, seq, d_ff), BF16),
        compiler_params=_params(("parallel", "arbitrary")),
        name="ffn_up",
    )(h, ffn_w_up, ffn_w_up, conv_w, conv_w, conv_b, conv_b)


def kernel(x, c, positions, fox_w_in, fox_b_f, fox_w_o, swa_w_in, swa_sinks, swa_w_o, ada_w, ada_b,
           ffn_w_up, ffn_conv_w, ffn_conv_b, ffn_w_down, ln_mix_g, ln_mix_b, ln_ffn_g, ln_ffn_b):
    batch, seq, d = x.shape
    depth = ada_w.shape[0]
    alpha = (2.0 * depth) ** 0.25
    fox_heads = fox_b_f.shape[1]
    fox_dh = d // fox_heads

    mod = _ada(c, ada_w, ada_b)
    rope_tables = _rope_tables(positions) if depth > 1 else None
    w_down = ffn_w_down.astype(BF16)
    fox_w = _fox_w_bf16(fox_w_in)

    h = None
    for i in range(depth):
        j = i // 2
        if i % 2 == 0:
            b_f = jnp.pad(fox_b_f[j], (0, LANES - fox_heads)).reshape(1, LANES)
            if h is None:
                qkv, fl = _inproj_fox(x, mod, i, fox_w, j, b_f, head_dim=fox_dh)
            else:
                qkv, fl = _inproj_fox(h, None, i, fox_w, j, b_f, head_dim=fox_dh)
            ccol, crow = _fox_cum(fl, fox_heads)
            o = _fox_attn(qkv, crow, ccol, heads=fox_heads)
            w_o = fox_w_o
        else:
            if h is None:
                raise NotImplementedError("SWA as the first layer")
            q, kv = _inproj_swa(h, swa_w_in, j, rope_tables, q_heads=swa_sinks.shape[1])
            o = _swa_attn(q, kv, swa_sinks, j)
            w_o = swa_w_o
        x, h2 = _proj_ln(o, w_o, j, x, mod, i, G1, ln_mix_g, ln_mix_b, (i, SC2, SH2), alpha=alpha, tm=512)
        hmid = _ffn_up(h2, ffn_w_up, ffn_conv_w, ffn_conv_b, i)
        next_mod = (i + 1, SC1, SH1) if i + 1 < depth else None
        x, h = _proj_ln(hmid, w_down, i, x, mod, i, G2, ln_ffn_g, ln_ffn_b, next_mod, alpha=alpha, tm=256)
    return x
```

```python
import functools

import jax
import jax.numpy as jnp
from jax import lax
from jax.experimental import pallas as pl
from jax.experimental.pallas import tpu as pltpu

F32 = jnp.float32
BF16 = jnp.bfloat16

LANES = 128
SUBLANES = 8
VMEM_LIMIT_BYTES = 56 * 1024 * 1024

LN_EPS = 1e-5
ROPE_THETA = 500000.0
ROPE_DIM = 16
ROPE_HALF = ROPE_DIM // 2
SWA_HEAD_DIM = 64
SWA_WINDOW = 128
CONV_WIDTH = 3
MASK_VALUE = -1e30
LOG2E = 1.4426950408889634

SH1, SC1, G1, SH2, SC2, G2 = range(6)


def _params(semantics):
    return pltpu.CompilerParams(dimension_semantics=semantics, vmem_limit_bytes=VMEM_LIMIT_BYTES)


def _dot(a, b):
    return jnp.dot(a, b, preferred_element_type=F32)


def _dot_nt(a, b):
    return lax.dot_general(a, b, (((1,), (1,)), ((), ())), preferred_element_type=F32)


def _resident(block_shape, index_map):
    return pl.BlockSpec(block_shape, index_map, pipeline_mode=pl.Buffered(1))


def _mod_spec(mod, layer, chunk):
    _, _, batch, d = mod.shape
    return _resident((None, None, batch, d), lambda *_: (layer, chunk, 0, 0))


def _ada_kernel(c_ref, w_ref, b_ref, o_ref):
    c = c_ref[...]
    c_act = (c * jax.nn.sigmoid(c)).astype(BF16)
    o_ref[...] = _dot(c_act, w_ref[...].astype(BF16)) + b_ref[...]


def _ada(c, ada_w, ada_b):
    depth, d, six_d = ada_w.shape
    batch = c.shape[0]
    tn = min(1024, d)
    nt = d // tn
    return pl.pallas_call(
        _ada_kernel,
        grid=(depth, 6, nt),
        in_specs=[
            pl.BlockSpec((batch, d), lambda i, k, j: (0, 0)),
            pl.BlockSpec((None, d, tn), lambda i, k, j: (i, 0, k * nt + j)),
            pl.BlockSpec((None, 1, tn), lambda i, k, j: (i, 0, k * nt + j)),
        ],
        out_specs=pl.BlockSpec((None, None, batch, tn), lambda i, k, j: (i, k, 0, j)),
        out_shape=jax.ShapeDtypeStruct((depth, 6, batch, d), F32),
        compiler_params=_params(("parallel", "parallel", "parallel")),
        name="ada_mod",
    )(c, ada_w, ada_b.reshape(depth, 1, six_d))


def _cast_cols_kernel(wt_ref, o_ref, *, valid_cols):
    tn = wt_ref.shape[0]
    w = wt_ref[...].T
    col = pl.program_id(1) * tn + lax.broadcasted_iota(jnp.int32, w.shape, 1)
    o_ref[...] = jnp.where(col < valid_cols, w, 0.0).astype(BF16)


def _fox_w_bf16(fox_w_in):
    n, d, cols = fox_w_in.shape
    tn = 512
    nt = pl.cdiv(cols, tn)
    return pl.pallas_call(
        functools.partial(_cast_cols_kernel, valid_cols=cols),
        grid=(n, nt),
        in_specs=[pl.BlockSpec((None, tn, d), lambda l, j: (l, j, 0))],
        out_specs=pl.BlockSpec((None, d, tn), lambda l, j: (l, 0, j)),
        out_shape=jax.ShapeDtypeStruct((n, d, nt * tn), BF16),
        compiler_params=_params(("parallel", "parallel")),
        name="fox_w_cast",
    )(jnp.swapaxes(fox_w_in, 1, 2))


def _inproj_fox_kernel(*refs, modulate, tiles_per_seq, q_tiles, q_scale):
    if modulate:
        x_ref, sc_ref, sh_ref, w_ref, wf_ref, bf_ref, qkv_ref, fl_ref, h_scr = refs
    else:
        x_ref, w_ref, wf_ref, bf_ref, qkv_ref, fl_ref = refs
    i = pl.program_id(0)
    j = pl.program_id(1)

    if modulate:
        @pl.when(j == 0)
        def _():
            b = i // tiles_per_seq
            sc = sc_ref[pl.ds(b, 1), :]
            sh = sh_ref[pl.ds(b, 1), :]
            h_scr[...] = (x_ref[...] * (1.0 + sc) + sh).astype(BF16)
        h_ref = h_scr
    else:
        h_ref = x_ref

    @pl.when(j == 0)
    def _():
        fl_ref[...] = _dot(h_ref[...], wf_ref[...]) + bf_ref[...]

    acc = _dot(h_ref[...], w_ref[...])
    acc = acc * jnp.where(j < q_tiles, q_scale, 1.0)
    for hh in range(acc.shape[1] // LANES):
        qkv_ref[hh] = acc[:, hh * LANES:(hh + 1) * LANES].astype(BF16)


def _inproj_fox(x_or_h, mod, layer, fox_w, fox_layer, b_f, *, head_dim):
    batch, seq, d = x_or_h.shape
    modulate = mod is not None
    n_out = 3 * d
    assert head_dim == LANES
    tn = min(1024, d)
    tm = min(1024 if modulate else 2048, seq)
    nm = seq // tm
    slots = n_out // LANES
    kern = functools.partial(_inproj_fox_kernel, modulate=modulate, tiles_per_seq=nm,
                             q_tiles=d // tn, q_scale=LOG2E * float(head_dim) ** -0.5)
    in_specs = [pl.BlockSpec((None, tm, d), lambda i, j: (i // nm, i % nm, 0))]
    args = [x_or_h]
    if modulate:
        in_specs += [_mod_spec(mod, layer, SC1), _mod_spec(mod, layer, SH1)]
        args += [mod, mod]
    in_specs += [
        pl.BlockSpec((None, d, tn), lambda i, j: (fox_layer, 0, j)),
        _resident((None, d, LANES), lambda i, j: (fox_layer, 0, n_out // LANES)),
        _resident((1, LANES), lambda i, j: (0, 0)),
    ]
    args += [fox_w, fox_w, b_f]
    return pl.pallas_call(
        kern,
        grid=(batch * nm, n_out // tn),
        in_specs=in_specs,
        out_specs=[
            pl.BlockSpec((None, tn // LANES, tm, LANES), lambda i, j: (i // nm, j, i % nm, 0)),
            pl.BlockSpec((None, tm, LANES), lambda i, j: (i // nm, i % nm, 0)),
        ],
        out_shape=[
            jax.ShapeDtypeStruct((batch, slots, seq, LANES), BF16),
            jax.ShapeDtypeStruct((batch, seq, LANES), F32),
        ],
        scratch_shapes=[pltpu.VMEM((tm, d), BF16)] if modulate else [],
        compiler_params=_params(("parallel", "arbitrary")),
        name="fox_inproj",
    )(*args)


def _fox_cum_kernel(fl_ref, col_ref, row_ref, *, heads):
    seq = fl_ref.shape[0]
    x = fl_ref[...]
    log_f = (jnp.minimum(x, 0.0) - jnp.log1p(jnp.exp(-jnp.abs(x)))) * LOG2E
    r = lax.broadcasted_iota(jnp.int32, (LANES, LANES), 0)
    c = lax.broadcasted_iota(jnp.int32, (LANES, LANES), 1)
    tri = (c <= r).astype(BF16)
    carry = jnp.zeros((1, LANES), F32)
    for blk in range(seq // LANES):
        xb = log_f[blk * LANES:(blk + 1) * LANES]
        hi = xb.astype(BF16)
        rem = xb - hi.astype(F32)
        mid = rem.astype(BF16)
        lo = (rem - mid.astype(F32)).astype(BF16)
        cb = _dot(tri, hi) + _dot(tri, mid) + _dot(tri, lo) + carry
        col_ref[blk * LANES:(blk + 1) * LANES, :] = cb
        carry = cb[LANES - 1:LANES, :]
    row_ref[...] = col_ref[...].T[:heads]


def _fox_cum(fl, heads):
    batch, seq, _ = fl.shape
    return pl.pallas_call(
        functools.partial(_fox_cum_kernel, heads=heads),
        grid=(batch,),
        in_specs=[pl.BlockSpec((None, seq, LANES), lambda b: (b, 0, 0))],
        out_specs=[
            pl.BlockSpec((None, seq, LANES), lambda b: (b, 0, 0)),
            pl.BlockSpec((None, heads, seq), lambda b: (b, 0, 0)),
        ],
        out_shape=[
            jax.ShapeDtypeStruct((batch, seq, LANES), F32),
            jax.ShapeDtypeStruct((batch, heads, seq), F32),
        ],
        compiler_params=_params(("parallel",)),
        name="fox_cum",
    )(fl)


def _fox_attn_kernel(q_ref, k_ref, v_ref, crow_ref, ccol_ref, o_ref, *, heads_per_step, blk):
    seq = q_ref.shape[1]
    hb = pl.program_id(1)
    lane = lax.broadcasted_iota(jnp.int32, (blk, LANES), 1)
    rows = lax.broadcasted_iota(jnp.int32, (blk, blk), 0)
    cols = lax.broadcasted_iota(jnp.int32, (blk, blk), 1)
    causal = cols <= rows

    def scores(hh, qi):
        h = hb * heads_per_step + hh
        q0 = qi * blk
        lk = q0 + blk
        s = _dot_nt(q_ref[hh, q0:lk, :], k_ref[hh, :lk, :]) - crow_ref[pl.ds(h, 1), :lk]
        diag = jnp.where(causal, s[:, q0:], MASK_VALUE)
        return diag if qi == 0 else jnp.concatenate([s[:, :q0], diag], axis=1)

    def finish(hh, qi, s):
        h = hb * heads_per_step + hh
        q0 = qi * blk
        lk = q0 + blk
        cq = jnp.sum(jnp.where(lane == h, ccol_ref[q0:lk, :], 0.0), axis=-1, keepdims=True)
        m = jnp.max(s, axis=-1, keepdims=True) + cq
        p = jnp.exp2(s + (cq - m))
        l = jnp.sum(p, axis=-1, keepdims=True)
        acc = _dot(p.astype(BF16), v_ref[hh, :lk, :])
        o_ref[q0:lk, hh * LANES:(hh + 1) * LANES] = (acc / l).astype(BF16)

    work = [(hh, qi) for hh in range(heads_per_step) for qi in range(seq // blk)]
    s_next = scores(*work[0])
    for idx, item in enumerate(work):
        s_cur = s_next
        if idx + 1 < len(work):
            s_next = scores(*work[idx + 1])
        finish(*item, s_cur)


def _fox_attn(qkv, crow, ccol, *, heads):
    batch, slots, seq, dh = qkv.shape
    assert slots == 3 * heads and dh == LANES
    hps = 4 if heads % 4 == 0 else 1
    groups = heads // hps
    blk = min(256, seq)
    kern = functools.partial(_fox_attn_kernel, heads_per_step=hps, blk=blk)
    head_spec = lambda off: pl.BlockSpec((None, hps, seq, dh), lambda b, g: (b, off + g, 0, 0))
    return pl.pallas_call(
        kern,
        grid=(batch, groups),
        in_specs=[
            head_spec(0), head_spec(groups), head_spec(2 * groups),
            pl.BlockSpec((None, heads, seq), lambda b, g: (b, 0, 0)),
            pl.BlockSpec((None, seq, LANES), lambda b, g: (b, 0, 0)),
        ],
        out_specs=pl.BlockSpec((None, seq, hps * dh), lambda b, g: (b, 0, g)),
        out_shape=jax.ShapeDtypeStruct((batch, seq, heads * dh), BF16),
        compiler_params=_params(("parallel", "parallel")),
        name="fox_attn",
    )(qkv, qkv, qkv, crow, ccol)


def _rope_table_kernel(pos_ref, inv_ref, o_ref):
    pos = pos_ref[...].astype(F32)
    ang = pos * inv_ref[...]
    cos = jnp.cos(ang)
    sin = jnp.sin(ang)
    idx = lax.broadcasted_iota(jnp.int32, ang.shape, 1) % SWA_HEAD_DIM
    o_ref[0] = jnp.where(idx < ROPE_DIM, cos, 1.0)
    o_ref[1] = jnp.where((idx >= ROPE_HALF) & (idx < ROPE_DIM), sin, 0.0)
    o_ref[2] = jnp.where(idx < ROPE_HALF, -sin, 0.0)


def _rope_tables(positions):
    batch, seq = positions.shape
    inv_freq = ROPE_THETA ** (-jnp.arange(0, ROPE_DIM, 2, dtype=F32) / ROPE_DIM)
    idx = jnp.arange(LANES) % SWA_HEAD_DIM
    inv_lane = jnp.where(idx < ROPE_DIM, inv_freq[idx % ROPE_HALF], 0.0).reshape(1, LANES)
    return pl.pallas_call(
        _rope_table_kernel,
        grid=(batch,),
        in_specs=[
            pl.BlockSpec((None, seq, 1), lambda b: (b, 0, 0)),
            pl.BlockSpec((1, LANES), lambda b: (0, 0)),
        ],
        out_specs=pl.BlockSpec((3, None, seq, LANES), lambda b: (0, b, 0, 0)),
        out_shape=jax.ShapeDtypeStruct((3, batch, seq, LANES), F32),
        compiler_params=_params(("parallel",)),
        name="rope_tables",
    )(positions.reshape(batch, seq, 1), inv_lane)


def _inproj_swa_kernel(h_ref, wq_ref, wkv_ref, rt_ref, q_ref, kv_ref, *, q_tiles, q_scale, k_width):
    j = pl.program_id(1)
    seq = h_ref.shape[0]
    chunk = min(512, seq)
    n_chunks = seq // chunk

    def project(w_ref, store):
        w = w_ref[...].astype(BF16)
        acc_next = _dot(h_ref[0:chunk, :], w)
        for c in range(n_chunks):
            rows = slice(c * chunk, (c + 1) * chunk)
            acc = acc_next
            if c + 1 < n_chunks:
                acc_next = _dot(h_ref[(c + 1) * chunk:(c + 2) * chunk, :], w)
            cos = rt_ref[0, rows, :]
            sin_lo = rt_ref[1, rows, :]
            sin_hi = rt_ref[2, rows, :]
            for cc in range(acc.shape[1] // LANES):
                a = acc[:, cc * LANES:(cc + 1) * LANES]
                rot = (a * cos + pltpu.roll(a, ROPE_HALF, 1) * sin_lo
                       + pltpu.roll(a, LANES - ROPE_HALF, 1) * sin_hi)
                store(rows, cc, a, rot)

    @pl.when(j < q_tiles)
    def _():
        def store(rows, cc, a, rot):
            q_ref[rows, cc * LANES:(cc + 1) * LANES] = (rot * q_scale).astype(BF16)
        project(wq_ref, store)

    @pl.when(j == q_tiles)
    def _():
        def store(rows, cc, a, rot):
            kv_ref[rows, cc * LANES:(cc + 1) * LANES] = (rot if cc * LANES < k_width else a).astype(BF16)
        project(wkv_ref, store)


def _inproj_swa(h, swa_w_in, swa_layer, rope_tables, *, q_heads):
    batch, seq, d = h.shape
    nq = q_heads * SWA_HEAD_DIM
    kvw = swa_w_in.shape[2] - nq
    assert (kvw // 2) % LANES == 0
    tn = min(512, nq)
    assert nq % tn == 0 and nq % kvw == 0
    q_tiles = nq // tn
    kern = functools.partial(_inproj_swa_kernel, q_tiles=q_tiles,
                             q_scale=LOG2E * float(SWA_HEAD_DIM) ** -0.5, k_width=kvw // 2)
    return pl.pallas_call(
        kern,
        grid=(batch, q_tiles + 1),
        in_specs=[
            pl.BlockSpec((None, seq, d), lambda b, j: (b, 0, 0)),
            pl.BlockSpec((None, d, tn), lambda b, j: (swa_layer, 0, jnp.minimum(j, q_tiles - 1))),
            _resident((None, d, kvw), lambda b, j: (swa_layer, 0, nq // kvw)),
            pl.BlockSpec((3, None, seq, LANES), lambda b, j: (0, b, 0, 0)),
        ],
        out_specs=[
            pl.BlockSpec((None, seq, tn), lambda b, j: (b, 0, jnp.minimum(j, q_tiles - 1))),
            pl.BlockSpec((None, seq, kvw), lambda b, j: (b, 0, 0)),
        ],
        out_shape=[
            jax.ShapeDtypeStruct((batch, seq, nq), BF16),
            jax.ShapeDtypeStruct((batch, seq, kvw), BF16),
        ],
        compiler_params=_params(("parallel", "arbitrary")),
        name="swa_inproj",
    )(h, swa_w_in, swa_w_in, rope_tables)


def _swa_attn_kernel(sink_ref, q_ref, k_ref, v_ref, o_ref, klo_scr, khi_scr, vlo_scr, vhi_scr, *,
                     group, layer):
    seq = q_ref.shape[0]
    blk = SWA_WINDOW
    g = pl.program_id(1)
    lane_s = lax.broadcasted_iota(jnp.int32, (seq, LANES), 1)
    hi_half = lane_s >= SWA_HEAD_DIM
    own_half = (lane_s // SWA_HEAD_DIM) == (g % 2)
    kp = k_ref[...].astype(F32)
    vp = v_ref[...].astype(F32)
    k2 = jnp.where(own_half, kp, pltpu.roll(kp, SWA_HEAD_DIM, 1))
    v2 = jnp.where(own_half, vp, pltpu.roll(vp, SWA_HEAD_DIM, 1))
    klo_scr[...] = jnp.where(hi_half, 0.0, k2).astype(BF16)
    khi_scr[...] = jnp.where(hi_half, k2, 0.0).astype(BF16)
    vlo_scr[...] = jnp.where(hi_half, 0.0, v2).astype(BF16)
    vhi_scr[...] = jnp.where(hi_half, v2, 0.0).astype(BF16)

    r = lax.broadcasted_iota(jnp.int32, (blk, 2 * blk), 0)
    c = lax.broadcasted_iota(jnp.int32, (blk, 2 * blk), 1)
    rel_first = r - c
    rel_rest = r + blk - c
    bias_first = jnp.where((rel_first >= 0) & (rel_first < SWA_WINDOW), 0.0, MASK_VALUE)
    bias_rest = jnp.where((rel_rest >= 0) & (rel_rest < SWA_WINDOW), 0.0, MASK_VALUE)

    def q_block(n, _):
        q0 = pl.multiple_of(n * blk, blk)
        ks = pl.multiple_of(jnp.maximum(n - 1, 0) * blk, blk)
        bias = jnp.where(n == 0, bias_first, bias_rest)
        keys = (klo_scr[pl.ds(ks, 2 * blk), :], khi_scr[pl.ds(ks, 2 * blk), :])
        vals = (vlo_scr[pl.ds(ks, 2 * blk), :], vhi_scr[pl.ds(ks, 2 * blk), :])
        for pair in range(group // 2):
            qp = q_ref[pl.ds(q0, blk), pair * LANES:(pair + 1) * LANES]
            out = jnp.zeros((blk, LANES), F32)
            for e in range(2):
                sink = sink_ref[layer, g * group + 2 * pair + e] * LOG2E
                s = _dot_nt(qp, keys[e]) + bias
                m = jnp.maximum(jnp.max(s, axis=-1, keepdims=True), sink)
                p = jnp.exp2(s - m)
                denom = jnp.sum(p, axis=-1, keepdims=True) + jnp.exp2(sink - m)
                out = out + _dot(p.astype(BF16), vals[e]) / denom
            o_ref[pl.ds(q0, blk), pair * LANES:(pair + 1) * LANES] = out.astype(BF16)
        return 0

    lax.fori_loop(0, seq // blk, q_block, 0, unroll=8)


def _swa_attn(q, kv, swa_sinks, swa_layer):
    batch, seq, nq = q.shape
    kvw = kv.shape[2]
    kv_heads = kvw // (2 * SWA_HEAD_DIM)
    q_heads = nq // SWA_HEAD_DIM
    group = q_heads // kv_heads
    gw = group * SWA_HEAD_DIM
    assert group % 2 == 0 and gw % LANES == 0 and kv_heads % 2 == 0
    v_off = kv_heads // 2
    return pl.pallas_call(
        functools.partial(_swa_attn_kernel, group=group, layer=swa_layer),
        grid=(batch, kv_heads),
        in_specs=[
            pl.BlockSpec(memory_space=pltpu.SMEM),
            pl.BlockSpec((None, seq, gw), lambda b, g: (b, 0, g)),
            pl.BlockSpec((None, seq, LANES), lambda b, g: (b, 0, g // 2)),
            pl.BlockSpec((None, seq, LANES), lambda b, g: (b, 0, v_off + g // 2)),
        ],
        out_specs=pl.BlockSpec((None, seq, gw), lambda b, g: (b, 0, g)),
        out_shape=jax.ShapeDtypeStruct((batch, seq, nq), BF16),
        scratch_shapes=[pltpu.VMEM((seq, LANES), BF16)] * 4,
        compiler_params=_params(("parallel", "arbitrary")),
        name="swa_attn",
    )(swa_sinks, q, kv, kv)


def _proj_ln_kernel(*refs, alpha, rows_per_batch, emit_next, cast_w, sub):
    refs = list(refs)
    w_scr = refs.pop() if cast_w else None
    if emit_next:
        a_ref, w_ref, x_ref, g_ref, lg_ref, lb_ref, sc_ref, sh_ref, xo_ref, ho_ref = refs
    else:
        a_ref, w_ref, x_ref, g_ref, lg_ref, lb_ref, xo_ref = refs
    if cast_w:
        @pl.when(pl.program_id(0) == 0)
        def _():
            w_scr[...] = w_ref[...].astype(BF16)
        w_ref = w_scr
    tm = a_ref.shape[0]
    b = (pl.program_id(0) * tm) // rows_per_batch
    gate = 1.0 + g_ref[pl.ds(b, 1), :]
    ln_g = lg_ref[...]
    ln_b = lb_ref[...]
    if emit_next:
        nsc = 1.0 + sc_ref[pl.ds(b, 1), :]
        nsh = sh_ref[pl.ds(b, 1), :]
    for c in range(tm // sub):
        sl = slice(c * sub, (c + 1) * sub)
        y = _dot(a_ref[sl, :], w_ref[...])
        z = alpha * x_ref[sl, :] + gate * y
        mu = jnp.mean(z, axis=-1, keepdims=True)
        zc = z - mu
        var = jnp.mean(zc * zc, axis=-1, keepdims=True)
        xn = zc * lax.rsqrt(var + LN_EPS) * ln_g + ln_b
        xo_ref[sl, :] = xn
        if emit_next:
            ho_ref[sl, :] = (xn * nsc + nsh).astype(BF16)


def _proj_ln(a, w, w_layer, x, mod, layer, gate_chunk, ln_g, ln_b, next_mod, *, alpha, tm):
    batch, seq, d = x.shape
    k = a.shape[-1]
    rows = batch * seq
    tm = min(tm, seq)
    emit_next = next_mod is not None
    cast_w = w.dtype != BF16
    kern = functools.partial(_proj_ln_kernel, alpha=alpha, rows_per_batch=seq, emit_next=emit_next,
                             cast_w=cast_w, sub=min(128 if cast_w else 256, tm))
    row_spec = lambda width: pl.BlockSpec((tm, width), lambda i: (i, 0))
    vec_spec = _resident((None, 1, d), lambda i: (layer, 0, 0))
    in_specs = [row_spec(k), _resident((None, k, d), lambda i: (w_layer, 0, 0)), row_spec(d),
                _mod_spec(mod, layer, gate_chunk), vec_spec, vec_spec]
    depth = ln_g.shape[0]
    args = [a.reshape(rows, k), w, x.reshape(rows, d), mod, ln_g.reshape(depth, 1, d), ln_b.reshape(depth, 1, d)]
    out_specs = [row_spec(d)]
    out_shape = [jax.ShapeDtypeStruct((rows, d), F32)]
    if emit_next:
        nl, nsc, nsh = next_mod
        in_specs += [_mod_spec(mod, nl, nsc), _mod_spec(mod, nl, nsh)]
        args += [mod, mod]
        out_specs.append(row_spec(d))
        out_shape.append(jax.ShapeDtypeStruct((rows, d), BF16))
    outs = pl.pallas_call(
        kern,
        grid=(rows // tm,),
        in_specs=in_specs,
        out_specs=out_specs,
        out_shape=out_shape,
        scratch_shapes=[pltpu.VMEM((k, d), BF16)] if cast_w else [],
        compiler_params=_params(("arbitrary",)),
        name="proj_ln",
    )(*args)
    x_new = outs[0].reshape(batch, seq, d)
    return x_new, (outs[1].reshape(batch, seq, d) if emit_next else None)


def _ffn_up_kernel(h_ref, wg_ref, wv_ref, cwg_ref, cwv_ref, cbg_ref, cbv_ref, o_ref, *, chunk):
    seq = h_ref.shape[0]
    tn = wg_ref.shape[1]
    wg = wg_ref[...].astype(BF16)
    wv = wv_ref[...].astype(BF16)

    def conv(u, halo, cw_ref, cb_ref):
        ext = jnp.concatenate([halo, u], axis=0)
        u1 = pltpu.roll(ext, 1, 0)[SUBLANES:]
        u2 = pltpu.roll(ext, 2, 0)[SUBLANES:]
        return u2 * cw_ref[0:1, :] + u1 * cw_ref[1:2, :] + u * cw_ref[2:3, :] + cb_ref[...]

    def matmuls(c):
        hc = h_ref[c * chunk:(c + 1) * chunk, :]
        return _dot(hc, wg), _dot(hc, wv)

    n_chunks = seq // chunk
    halo_g = jnp.zeros((SUBLANES, tn), F32)
    halo_v = jnp.zeros((SUBLANES, tn), F32)
    u_next = matmuls(0)
    for c in range(n_chunks):
        sl = slice(c * chunk, (c + 1) * chunk)
        ug, uv = u_next
        if c + 1 < n_chunks:
            u_next = matmuls(c + 1)
        cg = conv(ug, halo_g, cwg_ref, cbg_ref)
        cv = conv(uv, halo_v, cwv_ref, cbv_ref)
        halo_g = ug[chunk - SUBLANES:]
        halo_v = uv[chunk - SUBLANES:]
        o_ref[sl, :] = (cg * jax.nn.sigmoid(cg) * cv).astype(BF16)


def _ffn_up(h, ffn_w_up, conv_w, conv_b, layer):
    batch, seq, d = h.shape
    depth, _, two_f = ffn_w_up.shape
    d_ff = two_f // 2
    tn = 512 if d_ff % 512 == 0 else LANES
    nt = d_ff // tn
    chunk = min(1024, seq)
    lo = lambda b, j: (layer, 0, j)
    hi = lambda b, j: (layer, 0, nt + j)
    conv_b = conv_b.reshape(depth, 1, two_f)
    return pl.pallas_call(
        functools.partial(_ffn_up_kernel, chunk=chunk),
        grid=(batch, nt),
        in_specs=[
            pl.BlockSpec((None, seq, d), lambda b, j: (b, 0, 0)),
            pl.BlockSpec((None, d, tn), lo), pl.BlockSpec((None, d, tn), hi),
            pl.BlockSpec((None, CONV_WIDTH, tn), lo), pl.BlockSpec((None, CONV_WIDTH, tn), hi),
            pl.BlockSpec((None, 1, tn), lo), pl.BlockSpec((None, 1, tn), hi),
        ],
        out_specs=pl.BlockSpec((None, seq, tn), lambda b, j: (b, 0, j)),
        out_shape=jax.ShapeDtypeStruct((batch, seq, d_ff), BF16),
        compiler_params=_params(("parallel", "arbitrary")),
        name="ffn_up",
    )(h, ffn_w_up, ffn_w_up, conv_w, conv_w, conv_b, conv_b)


def kernel(x, c, positions, fox_w_in, fox_b_f, fox_w_o, swa_w_in, swa_sinks, swa_w_o, ada_w, ada_b,
           ffn_w_up, ffn_conv_w, ffn_conv_b, ffn_w_down, ln_mix_g, ln_mix_b, ln_ffn_g, ln_ffn_b):
    batch, seq, d = x.shape
    depth = ada_w.shape[0]
    alpha = (2.0 * depth) ** 0.25
    fox_heads = fox_b_f.shape[1]
    fox_dh = d // fox_heads

    mod = _ada(c, ada_w, ada_b)
    rope_tables = _rope_tables(positions) if depth > 1 else None
    w_down = ffn_w_down.astype(BF16)
    fox_w = _fox_w_bf16(fox_w_in)

    h = None
    for i in range(depth):
        j = i // 2
        if i % 2 == 0:
            b_f = jnp.pad(fox_b_f[j], (0, LANES - fox_heads)).reshape(1, LANES)
            if h is None:
                qkv, fl = _inproj_fox(x, mod, i, fox_w, j, b_f, head_dim=fox_dh)
            else:
                qkv, fl = _inproj_fox(h, None, i, fox_w, j, b_f, head_dim=fox_dh)
            ccol, crow = _fox_cum(fl, fox_heads)
            o = _fox_attn(qkv, crow, ccol, heads=fox_heads)
            w_o = fox_w_o
        else:
            if h is None:
                raise NotImplementedError("SWA as the first layer")
            q, kv = _inproj_swa(h, swa_w_in, j, rope_tables, q_heads=swa_sinks.shape[1])
            o = _swa_attn(q, kv, swa_sinks, j)
            w_o = swa_w_o
        x, h2 = _proj_ln(o, w_o, j, x, mod, i, G1, ln_mix_g, ln_mix_b, (i, SC2, SH2), alpha=alpha, tm=512)
        hmid = _ffn_up(h2, ffn_w_up, ffn_conv_w, ffn_conv_b, i)
        next_mod = (i + 1, SC1, SH1) if i + 1 < depth else None
        x, h = _proj_ln(hmid, w_down, i, x, mod, i, G2, ln_ffn_g, ln_ffn_b, next_mod, alpha=alpha, tm=256)
    return x
```

```python
import functools

import jax
import jax.numpy as jnp
from jax import lax
from jax.experimental import pallas as pl
from jax.experimental.pallas import tpu as pltpu

F32 = jnp.float32
BF16 = jnp.bfloat16

LANES = 128
SUBLANES = 8
VMEM_LIMIT_BYTES = 56 * 1024 * 1024

LN_EPS = 1e-5
ROPE_THETA = 500000.0
ROPE_DIM = 16
ROPE_HALF = ROPE_DIM // 2
SWA_HEAD_DIM = 64
SWA_WINDOW = 128
CONV_WIDTH = 3
MASK_VALUE = -1e30
LOG2E = 1.4426950408889634

SH1, SC1, G1, SH2, SC2, G2 = range(6)


def _params(semantics):
    return pltpu.CompilerParams(dimension_semantics=semantics, vmem_limit_bytes=VMEM_LIMIT_BYTES)


def _dot(a, b):
    return jnp.dot(a, b, preferred_element_type=F32)


def _dot_nt(a, b):
    return lax.dot_general(a, b, (((1,), (1,)), ((), ())), preferred_element_type=F32)


def _resident(block_shape, index_map):
    return pl.BlockSpec(block_shape, index_map, pipeline_mode=pl.Buffered(1))


def _mod_spec(mod, layer, chunk):
    _, _, batch, d = mod.shape
    return _resident((None, None, batch, d), lambda *_: (layer, chunk, 0, 0))


def _ada_kernel(c_ref, w_ref, b_ref, o_ref):
    c = c_ref[...]
    c_act = (c * jax.nn.sigmoid(c)).astype(BF16)
    o_ref[...] = _dot(c_act, w_ref[...].astype(BF16)) + b_ref[...]


def _ada(c, ada_w, ada_b):
    depth, d, six_d = ada_w.shape
    batch = c.shape[0]
    tn = min(1024, d)
    nt = d // tn
    return pl.pallas_call(
        _ada_kernel,
        grid=(depth, 6, nt),
        in_specs=[
            pl.BlockSpec((batch, d), lambda i, k, j: (0, 0)),
            pl.BlockSpec((None, d, tn), lambda i, k, j: (i, 0, k * nt + j)),
            pl.BlockSpec((None, 1, tn), lambda i, k, j: (i, 0, k * nt + j)),
        ],
        out_specs=pl.BlockSpec((None, None, batch, tn), lambda i, k, j: (i, k, 0, j)),
        out_shape=jax.ShapeDtypeStruct((depth, 6, batch, d), F32),
        compiler_params=_params(("parallel", "parallel", "parallel")),
        name="ada_mod",
    )(c, ada_w, ada_b.reshape(depth, 1, six_d))


def _cast_cols_kernel(wt_ref, o_ref, *, valid_cols):
    tn = wt_ref.shape[0]
    w = wt_ref[...].T
    col = pl.program_id(1) * tn + lax.broadcasted_iota(jnp.int32, w.shape, 1)
    o_ref[...] = jnp.where(col < valid_cols, w, 0.0).astype(BF16)


def _fox_w_bf16(fox_w_in):
    n, d, cols = fox_w_in.shape
    tn = 512
    nt = pl.cdiv(cols, tn)
    return pl.pallas_call(
        functools.partial(_cast_cols_kernel, valid_cols=cols),
        grid=(n, nt),
        in_specs=[pl.BlockSpec((None, tn, d), lambda l, j: (l, j, 0))],
        out_specs=pl.BlockSpec((None, d, tn), lambda l, j: (l, 0, j)),
        out_shape=jax.ShapeDtypeStruct((n, d, nt * tn), BF16),
        compiler_params=_params(("parallel", "parallel")),
        name="fox_w_cast",
    )(jnp.swapaxes(fox_w_in, 1, 2))


def _inproj_fox_kernel(*refs, modulate, tiles_per_seq, q_tiles, q_scale):
    if modulate:
        x_ref, sc_ref, sh_ref, w_ref, wf_ref, bf_ref, qkv_ref, fl_ref, h_scr = refs
    else:
        x_ref, w_ref, wf_ref, bf_ref, qkv_ref, fl_ref = refs
    i = pl.program_id(0)
    j = pl.program_id(1)

    if modulate:
        @pl.when(j == 0)
        def _():
            b = i // tiles_per_seq
            sc = sc_ref[pl.ds(b, 1), :]
            sh = sh_ref[pl.ds(b, 1), :]
            h_scr[...] = (x_ref[...] * (1.0 + sc) + sh).astype(BF16)
        h_ref = h_scr
    else:
        h_ref = x_ref

    @pl.when(j == 0)
    def _():
        fl_ref[...] = _dot(h_ref[...], wf_ref[...]) + bf_ref[...]

    acc = _dot(h_ref[...], w_ref[...])
    acc = acc * jnp.where(j < q_tiles, q_scale, 1.0)
    for hh in range(acc.shape[1] // LANES):
        qkv_ref[hh] = acc[:, hh * LANES:(hh + 1) * LANES].astype(BF16)


def _inproj_fox(x_or_h, mod, layer, fox_w, fox_layer, b_f, *, head_dim):
    batch, seq, d = x_or_h.shape
    modulate = mod is not None
    n_out = 3 * d
    assert head_dim == LANES
    tn = min(1024, d)
    tm = min(1024 if modulate else 2048, seq)
    nm = seq // tm
    slots = n_out // LANES
    kern = functools.partial(_inproj_fox_kernel, modulate=modulate, tiles_per_seq=nm,
                             q_tiles=d // tn, q_scale=LOG2E * float(head_dim) ** -0.5)
    in_specs = [pl.BlockSpec((None, tm, d), lambda i, j: (i // nm, i % nm, 0))]
    args = [x_or_h]
    if modulate:
        in_specs += [_mod_spec(mod, layer, SC1), _mod_spec(mod, layer, SH1)]
        args += [mod, mod]
    in_specs += [
        pl.BlockSpec((None, d, tn), lambda i, j: (fox_layer, 0, j)),
        _resident((None, d, LANES), lambda i, j: (fox_layer, 0, n_out // LANES)),
        _resident((1, LANES), lambda i, j: (0, 0)),
    ]
    args += [fox_w, fox_w, b_f]
    return pl.pallas_call(
        kern,
        grid=(batch * nm, n_out // tn),
        in_specs=in_specs,
        out_specs=[
            pl.BlockSpec((None, tn // LANES, tm, LANES), lambda i, j: (i // nm, j, i % nm, 0)),
            pl.BlockSpec((None, tm, LANES), lambda i, j: (i // nm, i % nm, 0)),
        ],
        out_shape=[
            jax.ShapeDtypeStruct((batch, slots, seq, LANES), BF16),
            jax.ShapeDtypeStruct((batch, seq, LANES), F32),
        ],
        scratch_shapes=[pltpu.VMEM((tm, d), BF16)] if modulate else [],
        compiler_params=_params(("parallel", "arbitrary")),
        name="fox_inproj",
    )(*args)


def _fox_cum_kernel(fl_ref, col_ref, row_ref, *, heads):
    seq = fl_ref.shape[0]
    x = fl_ref[...]
    log_f = (jnp.minimum(x, 0.0) - jnp.log1p(jnp.exp(-jnp.abs(x)))) * LOG2E
    r = lax.broadcasted_iota(jnp.int32, (LANES, LANES), 0)
    c = lax.broadcasted_iota(jnp.int32, (LANES, LANES), 1)
    tri = (c <= r).astype(BF16)
    carry = jnp.zeros((1, LANES), F32)
    for blk in range(seq // LANES):
        xb = log_f[blk * LANES:(blk + 1) * LANES]
        hi = xb.astype(BF16)
        rem = xb - hi.astype(F32)
        mid = rem.astype(BF16)
        lo = (rem - mid.astype(F32)).astype(BF16)
        cb = _dot(tri, hi) + _dot(tri, mid) + _dot(tri, lo) + carry
        col_ref[blk * LANES:(blk + 1) * LANES, :] = cb
        carry = cb[LANES - 1:LANES, :]
    row_ref[...] = col_ref[...].T[:heads]


def _fox_cum(fl, heads):
    batch, seq, _ = fl.shape
    return pl.pallas_call(
        functools.partial(_fox_cum_kernel, heads=heads),
        grid=(batch,),
        in_specs=[pl.BlockSpec((None, seq, LANES), lambda b: (b, 0, 0))],
        out_specs=[
            pl.BlockSpec((None, seq, LANES), lambda b: (b, 0, 0)),
            pl.BlockSpec((None, heads, seq), lambda b: (b, 0, 0)),
        ],
        out_shape=[
            jax.ShapeDtypeStruct((batch, seq, LANES), F32),
            jax.ShapeDtypeStruct((batch, heads, seq), F32),
        ],
        compiler_params=_params(("parallel",)),
        name="fox_cum",
    )(fl)


def _fox_attn_kernel(q_ref, k_ref, v_ref, crow_ref, ccol_ref, o_ref, *, heads_per_step, blk):
    seq = q_ref.shape[1]
    hb = pl.program_id(1)
    lane = lax.broadcasted_iota(jnp.int32, (blk, LANES), 1)
    rows = lax.broadcasted_iota(jnp.int32, (blk, blk), 0)
    cols = lax.broadcasted_iota(jnp.int32, (blk, blk), 1)
    causal = cols <= rows

    def scores(hh, qi):
        h = hb * heads_per_step + hh
        q0 = qi * blk
        lk = q0 + blk
        s = _dot_nt(q_ref[hh, q0:lk, :], k_ref[hh, :lk, :]) - crow_ref[pl.ds(h, 1), :lk]
        diag = jnp.where(causal, s[:, q0:], MASK_VALUE)
        return diag if qi == 0 else jnp.concatenate([s[:, :q0], diag], axis=1)

    def finish(hh, qi, s):
        h = hb * heads_per_step + hh
        q0 = qi * blk
        lk = q0 + blk
        cq = jnp.sum(jnp.where(lane == h, ccol_ref[q0:lk, :], 0.0), axis=-1, keepdims=True)
        m = jnp.max(s, axis=-1, keepdims=True) + cq
        p = jnp.exp2(s + (cq - m))
        l = jnp.sum(p, axis=-1, keepdims=True)
        acc = _dot(p.astype(BF16), v_ref[hh, :lk, :])
        o_ref[q0:lk, hh * LANES:(hh + 1) * LANES] = (acc / l).astype(BF16)

    work = [(hh, qi) for hh in range(heads_per_step) for qi in range(seq // blk)]
    s_next = scores(*work[0])
    for idx, item in enumerate(work):
        s_cur = s_next
        if idx + 1 < len(work):
            s_next = scores(*work[idx + 1])
        finish(*item, s_cur)


def _fox_attn(qkv, crow, ccol, *, heads):
    batch, slots, seq, dh = qkv.shape
    assert slots == 3 * heads and dh == LANES
    hps = 4 if heads % 4 == 0 else 1
    groups = heads // hps
    blk = min(256, seq)
    kern = functools.partial(_fox_attn_kernel, heads_per_step=hps, blk=blk)
    head_spec = lambda off: pl.BlockSpec((None, hps, seq, dh), lambda b, g: (b, off + g, 0, 0))
    return pl.pallas_call(
        kern,
        grid=(batch, groups),
        in_specs=[
            head_spec(0), head_spec(groups), head_spec(2 * groups),
            pl.BlockSpec((None, heads, seq), lambda b, g: (b, 0, 0)),
            pl.BlockSpec((None, seq, LANES), lambda b, g: (b, 0, 0)),
        ],
        out_specs=pl.BlockSpec((None, seq, hps * dh), lambda b, g: (b, 0, g)),
        out_shape=jax.ShapeDtypeStruct((batch, seq, heads * dh), BF16),
        compiler_params=_params(("parallel", "parallel")),
        name="fox_attn",
    )(qkv, qkv, qkv, crow, ccol)


def _rope_table_kernel(pos_ref, inv_ref, o_ref):
    pos = pos_ref[...].astype(F32)
    ang = pos * inv_ref[...]
    cos = jnp.cos(ang)
    sin = jnp.sin(ang)
    idx = lax.broadcasted_iota(jnp.int32, ang.shape, 1) % SWA_HEAD_DIM
    o_ref[0] = jnp.where(idx < ROPE_DIM, cos, 1.0)
    o_ref[1] = jnp.where((idx >= ROPE_HALF) & (idx < ROPE_DIM), sin, 0.0)
    o_ref[2] = jnp.where(idx < ROPE_HALF, -sin, 0.0)


def _rope_tables(positions):
    batch, seq = positions.shape
    inv_freq = ROPE_THETA ** (-jnp.arange(0, ROPE_DIM, 2, dtype=F32) / ROPE_DIM)
    idx = jnp.arange(LANES) % SWA_HEAD_DIM
    inv_lane = jnp.where(idx < ROPE_DIM, inv_freq[idx % ROPE_HALF], 0.0).reshape(1, LANES)
    return pl.pallas_call(
        _rope_table_kernel,
        grid=(batch,),
        in_specs=[
            pl.BlockSpec((None, seq, 1), lambda b: (b, 0, 0)),
            pl.BlockSpec((1, LANES), lambda b: (0, 0)),
        ],
        out_specs=pl.BlockSpec((3, None, seq, LANES), lambda b: (0, b, 0, 0)),
        out_shape=jax.ShapeDtypeStruct((3, batch, seq, LANES), F32),
        compiler_params=_params(("parallel",)),
        name="rope_tables",
    )(positions.reshape(batch, seq, 1), inv_lane)


def _inproj_swa_kernel(h_ref, wq_ref, wkv_ref, rt_ref, q_ref, kv_ref, *, q_tiles, q_scale, k_width):
    j = pl.program_id(1)
    seq = h_ref.shape[0]
    chunk = min(512, seq)
    n_chunks = seq // chunk

    def project(w_ref, store):
        w = w_ref[...].astype(BF16)
        acc_next = _dot(h_ref[0:chunk, :], w)
        for c in range(n_chunks):
            rows = slice(c * chunk, (c + 1) * chunk)
            acc = acc_next
            if c + 1 < n_chunks:
                acc_next = _dot(h_ref[(c + 1) * chunk:(c + 2) * chunk, :], w)
            cos = rt_ref[0, rows, :]
            sin_lo = rt_ref[1, rows, :]
            sin_hi = rt_ref[2, rows, :]
            for cc in range(acc.shape[1] // LANES):
                a = acc[:, cc * LANES:(cc + 1) * LANES]
                rot = (a * cos + pltpu.roll(a, ROPE_HALF, 1) * sin_lo
                       + pltpu.roll(a, LANES - ROPE_HALF, 1) * sin_hi)
                store(rows, cc, a, rot)

    @pl.when(j < q_tiles)
    def _():
        def store(rows, cc, a, rot):
            q_ref[rows, cc * LANES:(cc + 1) * LANES] = (rot * q_scale).astype(BF16)
        project(wq_ref, store)

    @pl.when(j == q_tiles)
    def _():
        def store(rows, cc, a, rot):
            kv_ref[rows, cc * LANES:(cc + 1) * LANES] = (rot if cc * LANES < k_width else a).astype(BF16)
        project(wkv_ref, store)


def _inproj_swa(h, swa_w_in, swa_layer, rope_tables, *, q_heads):
    batch, seq, d = h.shape
    nq = q_heads * SWA_HEAD_DIM
    kvw = swa_w_in.shape[2] - nq
    assert (kvw // 2) % LANES == 0
    tn = min(512, nq)
    assert nq % tn == 0 and nq % kvw == 0
    q_tiles = nq // tn
    kern = functools.partial(_inproj_swa_kernel, q_tiles=q_tiles,
                             q_scale=LOG2E * float(SWA_HEAD_DIM) ** -0.5, k_width=kvw // 2)
    return pl.pallas_call(
        kern,
        grid=(batch, q_tiles + 1),
        in_specs=[
            pl.BlockSpec((None, seq, d), lambda b, j: (b, 0, 0)),
            pl.BlockSpec((None, d, tn), lambda b, j: (swa_layer, 0, jnp.minimum(j, q_tiles - 1))),
            _resident((None, d, kvw), lambda b, j: (swa_layer, 0, nq // kvw)),
            pl.BlockSpec((3, None, seq, LANES), lambda b, j: (0, b, 0, 0)),
        ],
        out_specs=[
            pl.BlockSpec((None, seq, tn), lambda b, j: (b, 0, jnp.minimum(j, q_tiles - 1))),
            pl.BlockSpec((None, seq, kvw), lambda b, j: (b, 0, 0)),
        ],
        out_shape=[
            jax.ShapeDtypeStruct((batch, seq, nq), BF16),
            jax.ShapeDtypeStruct((batch, seq, kvw), BF16),
        ],
        compiler_params=_params(("parallel", "arbitrary")),
        name="swa_inproj",
    )(h, swa_w_in, swa_w_in, rope_tables)


def _swa_attn_kernel(sink_ref, q_ref, k_ref, v_ref, o_ref, klo_scr, khi_scr, vlo_scr, vhi_scr, *,
                     group, layer):
    seq = q_ref.shape[0]
    blk = SWA_WINDOW
    g = pl.program_id(1)
    lane_s = lax.broadcasted_iota(jnp.int32, (seq, LANES), 1)
    hi_half = lane_s >= SWA_HEAD_DIM
    own_half = (lane_s // SWA_HEAD_DIM) == (g % 2)
    kp = k_ref[...].astype(F32)
    vp = v_ref[...].astype(F32)
    k2 = jnp.where(own_half, kp, pltpu.roll(kp, SWA_HEAD_DIM, 1))
    v2 = jnp.where(own_half, vp, pltpu.roll(vp, SWA_HEAD_DIM, 1))
    klo_scr[...] = jnp.where(hi_half, 0.0, k2).astype(BF16)
    khi_scr[...] = jnp.where(hi_half, k2, 0.0).astype(BF16)
    vlo_scr[...] = jnp.where(hi_half, 0.0, v2).astype(BF16)
    vhi_scr[...] = jnp.where(hi_half, v2, 0.0).astype(BF16)

    r = lax.broadcasted_iota(jnp.int32, (blk, 2 * blk), 0)
    c = lax.broadcasted_iota(jnp.int32, (blk, 2 * blk), 1)
    rel_first = r - c
    rel_rest = r + blk - c
    bias_first = jnp.where((rel_first >= 0) & (rel_first < SWA_WINDOW), 0.0, MASK_VALUE)
    bias_rest = jnp.where((rel_rest >= 0) & (rel_rest < SWA_WINDOW), 0.0, MASK_VALUE)

    def q_block(n, _):
        q0 = pl.multiple_of(n * blk, blk)
        ks = pl.multiple_of(jnp.maximum(n - 1, 0) * blk, blk)
        bias = jnp.where(n == 0, bias_first, bias_rest)
        keys = (klo_scr[pl.ds(ks, 2 * blk), :], khi_scr[pl.ds(ks, 2 * blk), :])
        vals = (vlo_scr[pl.ds(ks, 2 * blk), :], vhi_scr[pl.ds(ks, 2 * blk), :])
        for pair in range(group // 2):
            qp = q_ref[pl.ds(q0, blk), pair * LANES:(pair + 1) * LANES]
            out = jnp.zeros((blk, LANES), F32)
            for e in range(2):
                sink = sink_ref[layer, g * group + 2 * pair + e] * LOG2E
                s = _dot_nt(qp, keys[e]) + bias
                m = jnp.maximum(jnp.max(s, axis=-1, keepdims=True), sink)
                p = jnp.exp2(s - m)
                denom = jnp.sum(p, axis=-1, keepdims=True) + jnp.exp2(sink - m)
                out = out + _dot(p.astype(BF16), vals[e]) / denom
            o_ref[pl.ds(q0, blk), pair * LANES:(pair + 1) * LANES] = out.astype(BF16)
        return 0

    lax.fori_loop(0, seq // blk, q_block, 0, unroll=8)


def _swa_attn(q, kv, swa_sinks, swa_layer):
    batch, seq, nq = q.shape
    kvw = kv.shape[2]
    kv_heads = kvw // (2 * SWA_HEAD_DIM)
    q_heads = nq // SWA_HEAD_DIM
    group = q_heads // kv_heads
    gw = group * SWA_HEAD_DIM
    assert group % 2 == 0 and gw % LANES == 0 and kv_heads % 2 == 0
    v_off = kv_heads // 2
    return pl.pallas_call(
        functools.partial(_swa_attn_kernel, group=group, layer=swa_layer),
        grid=(batch, kv_heads),
        in_specs=[
            pl.BlockSpec(memory_space=pltpu.SMEM),
            pl.BlockSpec((None, seq, gw), lambda b, g: (b, 0, g)),
            pl.BlockSpec((None, seq, LANES), lambda b, g: (b, 0, g // 2)),
            pl.BlockSpec((None, seq, LANES), lambda b, g: (b, 0, v_off + g // 2)),
        ],
        out_specs=pl.BlockSpec((None, seq, gw), lambda b, g: (b, 0, g)),
        out_shape=jax.ShapeDtypeStruct((batch, seq, nq), BF16),
        scratch_shapes=[pltpu.VMEM((seq, LANES), BF16)] * 4,
        compiler_params=_params(("parallel", "arbitrary")),
        name="swa_attn",
    )(swa_sinks, q, kv, kv)


def _proj_ln_kernel(*refs, alpha, rows_per_batch, emit_next, cast_w, sub):
    refs = list(refs)
    w_scr = refs.pop() if cast_w else None
    if emit_next:
        a_ref, w_ref, x_ref, g_ref, lg_ref, lb_ref, sc_ref, sh_ref, xo_ref, ho_ref = refs
    else:
        a_ref, w_ref, x_ref, g_ref, lg_ref, lb_ref, xo_ref = refs
    if cast_w:
        @pl.when(pl.program_id(0) == 0)
        def _():
            w_scr[...] = w_ref[...].astype(BF16)
        w_ref = w_scr
    tm = a_ref.shape[0]
    b = (pl.program_id(0) * tm) // rows_per_batch
    gate = 1.0 + g_ref[pl.ds(b, 1), :]
    ln_g = lg_ref[...]
    ln_b = lb_ref[...]
    if emit_next:
        nsc = 1.0 + sc_ref[pl.ds(b, 1), :]
        nsh = sh_ref[pl.ds(b, 1), :]
    for c in range(tm // sub):
        sl = slice(c * sub, (c + 1) * sub)
        y = _dot(a_ref[sl, :], w_ref[...])
        z = alpha * x_ref[sl, :] + gate * y
        mu = jnp.mean(z, axis=-1, keepdims=True)
        zc = z - mu
        var = jnp.mean(zc * zc, axis=-1, keepdims=True)
        xn = zc * lax.rsqrt(var + LN_EPS) * ln_g + ln_b
        xo_ref[sl, :] = xn
        if emit_next:
            ho_ref[sl, :] = (xn * nsc + nsh).astype(BF16)


def _proj_ln(a, w, w_layer, x, mod, layer, gate_chunk, ln_g, ln_b, next_mod, *, alpha, tm):
    batch, seq, d = x.shape
    k = a.shape[-1]
    rows = batch * seq
    tm = min(tm, seq)
    emit_next = next_mod is not None
    cast_w = w.dtype != BF16
    kern = functools.partial(_proj_ln_kernel, alpha=alpha, rows_per_batch=seq, emit_next=emit_next,
                             cast_w=cast_w, sub=min(128, tm))
    row_spec = lambda width: pl.BlockSpec((tm, width), lambda i: (i, 0))
    vec_spec = _resident((None, 1, d), lambda i: (layer, 0, 0))
    in_specs = [row_spec(k), _resident((None, k, d), lambda i: (w_layer, 0, 0)), row_spec(d),
                _mod_spec(mod, layer, gate_chunk), vec_spec, vec_spec]
    depth = ln_g.shape[0]
    args = [a.reshape(rows, k), w, x.reshape(rows, d), mod, ln_g.reshape(depth, 1, d), ln_b.reshape(depth, 1, d)]
    out_specs = [row_spec(d)]
    out_shape = [jax.ShapeDtypeStruct((rows, d), F32)]
    if emit_next:
        nl, nsc, nsh = next_mod
        in_specs += [_mod_spec(mod, nl, nsc), _mod_spec(mod, nl, nsh)]
        args += [mod, mod]
        out_specs.append(row_spec(d))
        out_shape.append(jax.ShapeDtypeStruct((rows, d), BF16))
    outs = pl.pallas_call(
        kern,
        grid=(rows // tm,),
        in_specs=in_specs,
        out_specs=out_specs,
        out_shape=out_shape,
        scratch_shapes=[pltpu.VMEM((k, d), BF16)] if cast_w else [],
        compiler_params=_params(("arbitrary",)),
        name="proj_ln",
    )(*args)
    x_new = outs[0].reshape(batch, seq, d)
    return x_new, (outs[1].reshape(batch, seq, d) if emit_next else None)


def _ffn_up_kernel(h_ref, wg_ref, wv_ref, cwg_ref, cwv_ref, cbg_ref, cbv_ref, o_ref, *, chunk):
    seq = h_ref.shape[0]
    tn = wg_ref.shape[1]
    wg = wg_ref[...].astype(BF16)
    wv = wv_ref[...].astype(BF16)

    def conv(u, halo, cw_ref, cb_ref):
        ext = jnp.concatenate([halo, u], axis=0)
        u1 = pltpu.roll(ext, 1, 0)[SUBLANES:]
        u2 = pltpu.roll(ext, 2, 0)[SUBLANES:]
        return u2 * cw_ref[0:1, :] + u1 * cw_ref[1:2, :] + u * cw_ref[2:3, :] + cb_ref[...]

    def matmuls(c):
        hc = h_ref[c * chunk:(c + 1) * chunk, :]
        return _dot(hc, wg), _dot(hc, wv)

    n_chunks = seq // chunk
    halo_g = jnp.zeros((SUBLANES, tn), F32)
    halo_v = jnp.zeros((SUBLANES, tn), F32)
    u_next = matmuls(0)
    for c in range(n_chunks):
        sl = slice(c * chunk, (c + 1) * chunk)
        ug, uv = u_next
        if c + 1 < n_chunks:
            u_next = matmuls(c + 1)
        cg = conv(ug, halo_g, cwg_ref, cbg_ref)
        cv = conv(uv, halo_v, cwv_ref, cbv_ref)
        halo_g = ug[chunk - SUBLANES:]
        halo_v = uv[chunk - SUBLANES:]
        o_ref[sl, :] = (cg * jax.nn.sigmoid(cg) * cv).astype(BF16)


def _ffn_up(h, ffn_w_up, conv_w, conv_b, layer):
    batch, seq, d = h.shape
    depth, _, two_f = ffn_w_up.shape
    d_ff = two_f // 2
    tn = 512 if d_ff % 512 == 0 else LANES
    nt = d_ff // tn
    chunk = min(2048, seq)
    lo = lambda b, j: (layer, 0, j)
    hi = lambda b, j: (layer, 0, nt + j)
    conv_b = conv_b.reshape(depth, 1, two_f)
    return pl.pallas_call(
        functools.partial(_ffn_up_kernel, chunk=chunk),
        grid=(batch, nt),
        in_specs=[
            pl.BlockSpec((None, seq, d), lambda b, j: (b, 0, 0)),
            pl.BlockSpec((None, d, tn), lo), pl.BlockSpec((None, d, tn), hi),
            pl.BlockSpec((None, CONV_WIDTH, tn), lo), pl.BlockSpec((None, CONV_WIDTH, tn), hi),
            pl.BlockSpec((None, 1, tn), lo), pl.BlockSpec((None, 1, tn), hi),
        ],
        out_specs=pl.BlockSpec((None, seq, tn), lambda b, j: (b, 0, j)),
        out_shape=jax.ShapeDtypeStruct((batch, seq, d_ff), BF16),
        compiler_params=_params(("parallel", "arbitrary")),
        name="ffn_up",
    )(h, ffn_w_up, ffn_w_up, conv_w, conv_w, conv_b, conv_b)


def kernel(x, c, positions, fox_w_in, fox_b_f, fox_w_o, swa_w_in, swa_sinks, swa_w_o, ada_w, ada_b,
           ffn_w_up, ffn_conv_w, ffn_conv_b, ffn_w_down, ln_mix_g, ln_mix_b, ln_ffn_g, ln_ffn_b):
    batch, seq, d = x.shape
    depth = ada_w.shape[0]
    alpha = (2.0 * depth) ** 0.25
    fox_heads = fox_b_f.shape[1]
    fox_dh = d // fox_heads

    mod = _ada(c, ada_w, ada_b)
    rope_tables = _rope_tables(positions) if depth > 1 else None
    w_down = ffn_w_down.astype(BF16)
    fox_w = _fox_w_bf16(fox_w_in)

    h = None
    for i in range(depth):
        j = i // 2
        if i % 2 == 0:
            b_f = jnp.pad(fox_b_f[j], (0, LANES - fox_heads)).reshape(1, LANES)
            if h is None:
                qkv, fl = _inproj_fox(x, mod, i, fox_w, j, b_f, head_dim=fox_dh)
            else:
                qkv, fl = _inproj_fox(h, None, i, fox_w, j, b_f, head_dim=fox_dh)
            ccol, crow = _fox_cum(fl, fox_heads)
            o = _fox_attn(qkv, crow, ccol, heads=fox_heads)
            w_o = fox_w_o
        else:
            if h is None:
                raise NotImplementedError("SWA as the first layer")
            q, kv = _inproj_swa(h, swa_w_in, j, rope_tables, q_heads=swa_sinks.shape[1])
            o = _swa_attn(q, kv, swa_sinks, j)
            w_o = swa_w_o
        x, h2 = _proj_ln(o, w_o, j, x, mod, i, G1, ln_mix_g, ln_mix_b, (i, SC2, SH2), alpha=alpha, tm=512)
        hmid = _ffn_up(h2, ffn_w_up, ffn_conv_w, ffn_conv_b, i)
        next_mod = (i + 1, SC1, SH1) if i + 1 < depth else None
        x, h = _proj_ln(hmid, w_down, i, x, mod, i, G2, ln_ffn_g, ln_ffn_b, next_mod, alpha=alpha, tm=256)
    return x
```

```python
import functools

import jax
import jax.numpy as jnp
from jax import lax
from jax.experimental import pallas as pl
from jax.experimental.pallas import tpu as pltpu

F32 = jnp.float32
BF16 = jnp.bfloat16

LANES = 128
SUBLANES = 8
VMEM_LIMIT_BYTES = 56 * 1024 * 1024

LN_EPS = 1e-5
ROPE_THETA = 500000.0
ROPE_DIM = 16
ROPE_HALF = ROPE_DIM // 2
SWA_HEAD_DIM = 64
SWA_WINDOW = 128
CONV_WIDTH = 3
MASK_VALUE = -1e30
LOG2E = 1.4426950408889634

SH1, SC1, G1, SH2, SC2, G2 = range(6)


def _params(semantics):
    return pltpu.CompilerParams(dimension_semantics=semantics, vmem_limit_bytes=VMEM_LIMIT_BYTES)


def _dot(a, b):
    return jnp.dot(a, b, preferred_element_type=F32)


def _dot_nt(a, b):
    return lax.dot_general(a, b, (((1,), (1,)), ((), ())), preferred_element_type=F32)


def _resident(block_shape, index_map):
    return pl.BlockSpec(block_shape, index_map, pipeline_mode=pl.Buffered(1))


def _mod_spec(mod, layer, chunk):
    _, _, batch, d = mod.shape
    return _resident((None, None, batch, d), lambda *_: (layer, chunk, 0, 0))


def _ada_kernel(c_ref, w_ref, b_ref, o_ref):
    c = c_ref[...]
    c_act = (c * jax.nn.sigmoid(c)).astype(BF16)
    o_ref[...] = _dot(c_act, w_ref[...].astype(BF16)) + b_ref[...]


def _ada(c, ada_w, ada_b):
    depth, d, six_d = ada_w.shape
    batch = c.shape[0]
    tn = min(1024, d)
    nt = d // tn
    return pl.pallas_call(
        _ada_kernel,
        grid=(depth, 6, nt),
        in_specs=[
            pl.BlockSpec((batch, d), lambda i, k, j: (0, 0)),
            pl.BlockSpec((None, d, tn), lambda i, k, j: (i, 0, k * nt + j)),
            pl.BlockSpec((None, 1, tn), lambda i, k, j: (i, 0, k * nt + j)),
        ],
        out_specs=pl.BlockSpec((None, None, batch, tn), lambda i, k, j: (i, k, 0, j)),
        out_shape=jax.ShapeDtypeStruct((depth, 6, batch, d), F32),
        compiler_params=_params(("parallel", "parallel", "parallel")),
        name="ada_mod",
    )(c, ada_w, ada_b.reshape(depth, 1, six_d))


def _cast_cols_kernel(wt_ref, o_ref, *, valid_cols):
    tn = wt_ref.shape[0]
    w = wt_ref[...].T
    col = pl.program_id(1) * tn + lax.broadcasted_iota(jnp.int32, w.shape, 1)
    o_ref[...] = jnp.where(col < valid_cols, w, 0.0).astype(BF16)


def _fox_w_bf16(fox_w_in):
    n, d, cols = fox_w_in.shape
    tn = 512
    nt = pl.cdiv(cols, tn)
    return pl.pallas_call(
        functools.partial(_cast_cols_kernel, valid_cols=cols),
        grid=(n, nt),
        in_specs=[pl.BlockSpec((None, tn, d), lambda l, j: (l, j, 0))],
        out_specs=pl.BlockSpec((None, d, tn), lambda l, j: (l, 0, j)),
        out_shape=jax.ShapeDtypeStruct((n, d, nt * tn), BF16),
        compiler_params=_params(("parallel", "parallel")),
        name="fox_w_cast",
    )(jnp.swapaxes(fox_w_in, 1, 2))


def _inproj_fox_kernel(*refs, modulate, tiles_per_seq, q_tiles, q_scale):
    if modulate:
        x_ref, sc_ref, sh_ref, w_ref, wf_ref, bf_ref, qkv_ref, fl_ref, h_scr = refs
    else:
        x_ref, w_ref, wf_ref, bf_ref, qkv_ref, fl_ref = refs
    i = pl.program_id(0)
    j = pl.program_id(1)

    if modulate:
        @pl.when(j == 0)
        def _():
            b = i // tiles_per_seq
            sc = sc_ref[pl.ds(b, 1), :]
            sh = sh_ref[pl.ds(b, 1), :]
            h_scr[...] = (x_ref[...] * (1.0 + sc) + sh).astype(BF16)
        h_ref = h_scr
    else:
        h_ref = x_ref

    @pl.when(j == 0)
    def _():
        fl_ref[...] = _dot(h_ref[...], wf_ref[...]) + bf_ref[...]

    acc = _dot(h_ref[...], w_ref[...])
    acc = acc * jnp.where(j < q_tiles, q_scale, 1.0)
    for hh in range(acc.shape[1] // LANES):
        qkv_ref[hh] = acc[:, hh * LANES:(hh + 1) * LANES].astype(BF16)


def _inproj_fox(x_or_h, mod, layer, fox_w, fox_layer, b_f, *, head_dim):
    batch, seq, d = x_or_h.shape
    modulate = mod is not None
    n_out = 3 * d
    assert head_dim == LANES
    tn = min(1024, d)
    tm = min(1024 if modulate else 2048, seq)
    nm = seq // tm
    slots = n_out // LANES
    kern = functools.partial(_inproj_fox_kernel, modulate=modulate, tiles_per_seq=nm,
                             q_tiles=d // tn, q_scale=LOG2E * float(head_dim) ** -0.5)
    in_specs = [pl.BlockSpec((None, tm, d), lambda i, j: (i // nm, i % nm, 0))]
    args = [x_or_h]
    if modulate:
        in_specs += [_mod_spec(mod, layer, SC1), _mod_spec(mod, layer, SH1)]
        args += [mod, mod]
    in_specs += [
        pl.BlockSpec((None, d, tn), lambda i, j: (fox_layer, 0, j)),
        _resident((None, d, LANES), lambda i, j: (fox_layer, 0, n_out // LANES)),
        _resident((1, LANES), lambda i, j: (0, 0)),
    ]
    args += [fox_w, fox_w, b_f]
    return pl.pallas_call(
        kern,
        grid=(batch * nm, n_out // tn),
        in_specs=in_specs,
        out_specs=[
            pl.BlockSpec((None, tn // LANES, tm, LANES), lambda i, j: (i // nm, j, i % nm, 0)),
            pl.BlockSpec((None, tm, LANES), lambda i, j: (i // nm, i % nm, 0)),
        ],
        out_shape=[
            jax.ShapeDtypeStruct((batch, slots, seq, LANES), BF16),
            jax.ShapeDtypeStruct((batch, seq, LANES), F32),
        ],
        scratch_shapes=[pltpu.VMEM((tm, d), BF16)] if modulate else [],
        compiler_params=_params(("parallel", "arbitrary")),
        name="fox_inproj",
    )(*args)


def _fox_cum_kernel(fl_ref, col_ref, row_ref, *, heads):
    seq = fl_ref.shape[0]
    x = fl_ref[...]
    log_f = (jnp.minimum(x, 0.0) - jnp.log1p(jnp.exp(-jnp.abs(x)))) * LOG2E
    r = lax.broadcasted_iota(jnp.int32, (LANES, LANES), 0)
    c = lax.broadcasted_iota(jnp.int32, (LANES, LANES), 1)
    tri = (c <= r).astype(BF16)
    carry = jnp.zeros((1, LANES), F32)
    for blk in range(seq // LANES):
        xb = log_f[blk * LANES:(blk + 1) * LANES]
        hi = xb.astype(BF16)
        rem = xb - hi.astype(F32)
        mid = rem.astype(BF16)
        lo = (rem - mid.astype(F32)).astype(BF16)
        cb = _dot(tri, hi) + _dot(tri, mid) + _dot(tri, lo) + carry
        col_ref[blk * LANES:(blk + 1) * LANES, :] = cb
        carry = cb[LANES - 1:LANES, :]
    row_ref[...] = col_ref[...].T[:heads]


def _fox_cum(fl, heads):
    batch, seq, _ = fl.shape
    return pl.pallas_call(
        functools.partial(_fox_cum_kernel, heads=heads),
        grid=(batch,),
        in_specs=[pl.BlockSpec((None, seq, LANES), lambda b: (b, 0, 0))],
        out_specs=[
            pl.BlockSpec((None, seq, LANES), lambda b: (b, 0, 0)),
            pl.BlockSpec((None, heads, seq), lambda b: (b, 0, 0)),
        ],
        out_shape=[
            jax.ShapeDtypeStruct((batch, seq, LANES), F32),
            jax.ShapeDtypeStruct((batch, heads, seq), F32),
        ],
        compiler_params=_params(("parallel",)),
        name="fox_cum",
    )(fl)


def _fox_attn_kernel(q_ref, k_ref, v_ref, crow_ref, ccol_ref, o_ref, *, heads_per_step, blk):
    seq = q_ref.shape[1]
    hb = pl.program_id(1)
    lane = lax.broadcasted_iota(jnp.int32, (blk, LANES), 1)
    rows = lax.broadcasted_iota(jnp.int32, (blk, blk), 0)
    cols = lax.broadcasted_iota(jnp.int32, (blk, blk), 1)
    causal = cols <= rows

    def scores(hh, qi):
        h = hb * heads_per_step + hh
        q0 = qi * blk
        lk = q0 + blk
        s = _dot_nt(q_ref[hh, q0:lk, :], k_ref[hh, :lk, :]) - crow_ref[pl.ds(h, 1), :lk]
        diag = jnp.where(causal, s[:, q0:], MASK_VALUE)
        return diag if qi == 0 else jnp.concatenate([s[:, :q0], diag], axis=1)

    def finish(hh, qi, s):
        h = hb * heads_per_step + hh
        q0 = qi * blk
        lk = q0 + blk
        cq = jnp.sum(jnp.where(lane == h, ccol_ref[q0:lk, :], 0.0), axis=-1, keepdims=True)
        m = jnp.max(s, axis=-1, keepdims=True) + cq
        p = jnp.exp2(s + (cq - m))
        l = jnp.sum(p, axis=-1, keepdims=True)
        acc = _dot(p.astype(BF16), v_ref[hh, :lk, :])
        o_ref[q0:lk, hh * LANES:(hh + 1) * LANES] = (acc / l).astype(BF16)

    work = [(hh, qi) for hh in range(heads_per_step) for qi in range(seq // blk)]
    s_next = scores(*work[0])
    for idx, item in enumerate(work):
        s_cur = s_next
        if idx + 1 < len(work):
            s_next = scores(*work[idx + 1])
        finish(*item, s_cur)


def _fox_attn(qkv, crow, ccol, *, heads):
    batch, slots, seq, dh = qkv.shape
    assert slots == 3 * heads and dh == LANES
    hps = 4 if heads % 4 == 0 else 1
    groups = heads // hps
    blk = min(256, seq)
    kern = functools.partial(_fox_attn_kernel, heads_per_step=hps, blk=blk)
    head_spec = lambda off: pl.BlockSpec((None, hps, seq, dh), lambda b, g: (b, off + g, 0, 0))
    return pl.pallas_call(
        kern,
        grid=(batch, groups),
        in_specs=[
            head_spec(0), head_spec(groups), head_spec(2 * groups),
            pl.BlockSpec((None, heads, seq), lambda b, g: (b, 0, 0)),
            pl.BlockSpec((None, seq, LANES), lambda b, g: (b, 0, 0)),
        ],
        out_specs=pl.BlockSpec((None, seq, hps * dh), lambda b, g: (b, 0, g)),
        out_shape=jax.ShapeDtypeStruct((batch, seq, heads * dh), BF16),
        compiler_params=_params(("parallel", "parallel")),
        name="fox_attn",
    )(qkv, qkv, qkv, crow, ccol)


def _rope_table_kernel(pos_ref, inv_ref, o_ref):
    pos = pos_ref[...].astype(F32)
    ang = pos * inv_ref[...]
    cos = jnp.cos(ang)
    sin = jnp.sin(ang)
    idx = lax.broadcasted_iota(jnp.int32, ang.shape, 1) % SWA_HEAD_DIM
    o_ref[0] = jnp.where(idx < ROPE_DIM, cos, 1.0)
    o_ref[1] = jnp.where((idx >= ROPE_HALF) & (idx < ROPE_DIM), sin, 0.0)
    o_ref[2] = jnp.where(idx < ROPE_HALF, -sin, 0.0)


def _rope_tables(positions):
    batch, seq = positions.shape
    inv_freq = ROPE_THETA ** (-jnp.arange(0, ROPE_DIM, 2, dtype=F32) / ROPE_DIM)
    idx = jnp.arange(LANES) % SWA_HEAD_DIM
    inv_lane = jnp.where(idx < ROPE_DIM, inv_freq[idx % ROPE_HALF], 0.0).reshape(1, LANES)
    return pl.pallas_call(
        _rope_table_kernel,
        grid=(batch,),
        in_specs=[
            pl.BlockSpec((None, seq, 1), lambda b: (b, 0, 0)),
            pl.BlockSpec((1, LANES), lambda b: (0, 0)),
        ],
        out_specs=pl.BlockSpec((3, None, seq, LANES), lambda b: (0, b, 0, 0)),
        out_shape=jax.ShapeDtypeStruct((3, batch, seq, LANES), F32),
        compiler_params=_params(("parallel",)),
        name="rope_tables",
    )(positions.reshape(batch, seq, 1), inv_lane)


def _inproj_swa_kernel(h_ref, wq_ref, wkv_ref, rt_ref, q_ref, kv_ref, *, q_tiles, q_scale, k_width):
    j = pl.program_id(1)
    seq = h_ref.shape[0]
    chunk = min(1024, seq)
    n_chunks = seq // chunk

    def project(w_ref, store):
        w = w_ref[...].astype(BF16)
        acc_next = _dot(h_ref[0:chunk, :], w)
        for c in range(n_chunks):
            rows = slice(c * chunk, (c + 1) * chunk)
            acc = acc_next
            if c + 1 < n_chunks:
                acc_next = _dot(h_ref[(c + 1) * chunk:(c + 2) * chunk, :], w)
            cos = rt_ref[0, rows, :]
            sin_lo = rt_ref[1, rows, :]
            sin_hi = rt_ref[2, rows, :]
            for cc in range(acc.shape[1] // LANES):
                a = acc[:, cc * LANES:(cc + 1) * LANES]
                rot = (a * cos + pltpu.roll(a, ROPE_HALF, 1) * sin_lo
                       + pltpu.roll(a, LANES - ROPE_HALF, 1) * sin_hi)
                store(rows, cc, a, rot)

    @pl.when(j < q_tiles)
    def _():
        def store(rows, cc, a, rot):
            q_ref[rows, cc * LANES:(cc + 1) * LANES] = (rot * q_scale).astype(BF16)
        project(wq_ref, store)

    @pl.when(j == q_tiles)
    def _():
        def store(rows, cc, a, rot):
            kv_ref[rows, cc * LANES:(cc + 1) * LANES] = (rot if cc * LANES < k_width else a).astype(BF16)
        project(wkv_ref, store)


def _inproj_swa(h, swa_w_in, swa_layer, rope_tables, *, q_heads):
    batch, seq, d = h.shape
    nq = q_heads * SWA_HEAD_DIM
    kvw = swa_w_in.shape[2] - nq
    assert (kvw // 2) % LANES == 0
    tn = min(512, nq)
    assert nq % tn == 0 and nq % kvw == 0
    q_tiles = nq // tn
    kern = functools.partial(_inproj_swa_kernel, q_tiles=q_tiles,
                             q_scale=LOG2E * float(SWA_HEAD_DIM) ** -0.5, k_width=kvw // 2)
    return pl.pallas_call(
        kern,
        grid=(batch, q_tiles + 1),
        in_specs=[
            pl.BlockSpec((None, seq, d), lambda b, j: (b, 0, 0)),
            pl.BlockSpec((None, d, tn), lambda b, j: (swa_layer, 0, jnp.minimum(j, q_tiles - 1))),
            _resident((None, d, kvw), lambda b, j: (swa_layer, 0, nq // kvw)),
            pl.BlockSpec((3, None, seq, LANES), lambda b, j: (0, b, 0, 0)),
        ],
        out_specs=[
            pl.BlockSpec((None, seq, tn), lambda b, j: (b, 0, jnp.minimum(j, q_tiles - 1))),
            pl.BlockSpec((None, seq, kvw), lambda b, j: (b, 0, 0)),
        ],
        out_shape=[
            jax.ShapeDtypeStruct((batch, seq, nq), BF16),
            jax.ShapeDtypeStruct((batch, seq, kvw), BF16),
        ],
        compiler_params=_params(("parallel", "arbitrary")),
        name="swa_inproj",
    )(h, swa_w_in, swa_w_in, rope_tables)


def _swa_attn_kernel(sink_ref, q_ref, k_ref, v_ref, o_ref, klo_scr, khi_scr, vlo_scr, vhi_scr, *,
                     group, layer):
    seq = q_ref.shape[0]
    blk = SWA_WINDOW
    g = pl.program_id(1)
    lane_s = lax.broadcasted_iota(jnp.int32, (seq, LANES), 1)
    hi_half = lane_s >= SWA_HEAD_DIM
    own_half = (lane_s // SWA_HEAD_DIM) == (g % 2)
    kp = k_ref[...].astype(F32)
    vp = v_ref[...].astype(F32)
    k2 = jnp.where(own_half, kp, pltpu.roll(kp, SWA_HEAD_DIM, 1))
    v2 = jnp.where(own_half, vp, pltpu.roll(vp, SWA_HEAD_DIM, 1))
    klo_scr[...] = jnp.where(hi_half, 0.0, k2).astype(BF16)
    khi_scr[...] = jnp.where(hi_half, k2, 0.0).astype(BF16)
    vlo_scr[...] = jnp.where(hi_half, 0.0, v2).astype(BF16)
    vhi_scr[...] = jnp.where(hi_half, v2, 0.0).astype(BF16)

    r = lax.broadcasted_iota(jnp.int32, (blk, 2 * blk), 0)
    c = lax.broadcasted_iota(jnp.int32, (blk, 2 * blk), 1)
    rel_first = r - c
    rel_rest = r + blk - c
    bias_first = jnp.where((rel_first >= 0) & (rel_first < SWA_WINDOW), 0.0, MASK_VALUE)
    bias_rest = jnp.where((rel_rest >= 0) & (rel_rest < SWA_WINDOW), 0.0, MASK_VALUE)

    def q_block(n, _):
        q0 = pl.multiple_of(n * blk, blk)
        ks = pl.multiple_of(jnp.maximum(n - 1, 0) * blk, blk)
        bias = jnp.where(n == 0, bias_first, bias_rest)
        keys = (klo_scr[pl.ds(ks, 2 * blk), :], khi_scr[pl.ds(ks, 2 * blk), :])
        vals = (vlo_scr[pl.ds(ks, 2 * blk), :], vhi_scr[pl.ds(ks, 2 * blk), :])
        for pair in range(group // 2):
            qp = q_ref[pl.ds(q0, blk), pair * LANES:(pair + 1) * LANES]
            out = jnp.zeros((blk, LANES), F32)
            for e in range(2):
                sink = sink_ref[layer, g * group + 2 * pair + e] * LOG2E
                s = _dot_nt(qp, keys[e]) + bias
                m = jnp.maximum(jnp.max(s, axis=-1, keepdims=True), sink)
                p = jnp.exp2(s - m)
                denom = jnp.sum(p, axis=-1, keepdims=True) + jnp.exp2(sink - m)
                out = out + _dot(p.astype(BF16), vals[e]) / denom
            o_ref[pl.ds(q0, blk), pair * LANES:(pair + 1) * LANES] = out.astype(BF16)
        return 0

    lax.fori_loop(0, seq // blk, q_block, 0, unroll=8)


def _swa_attn(q, kv, swa_sinks, swa_layer):
    batch, seq, nq = q.shape
    kvw = kv.shape[2]
    kv_heads = kvw // (2 * SWA_HEAD_DIM)
    q_heads = nq // SWA_HEAD_DIM
    group = q_heads // kv_heads
    gw = group * SWA_HEAD_DIM
    assert group % 2 == 0 and gw % LANES == 0 and kv_heads % 2 == 0
    v_off = kv_heads // 2
    return pl.pallas_call(
        functools.partial(_swa_attn_kernel, group=group, layer=swa_layer),
        grid=(batch, kv_heads),
        in_specs=[
            pl.BlockSpec(memory_space=pltpu.SMEM),
            pl.BlockSpec((None, seq, gw), lambda b, g: (b, 0, g)),
            pl.BlockSpec((None, seq, LANES), lambda b, g: (b, 0, g // 2)),
            pl.BlockSpec((None, seq, LANES), lambda b, g: (b, 0, v_off + g // 2)),
        ],
        out_specs=pl.BlockSpec((None, seq, gw), lambda b, g: (b, 0, g)),
        out_shape=jax.ShapeDtypeStruct((batch, seq, nq), BF16),
        scratch_shapes=[pltpu.VMEM((seq, LANES), BF16)] * 4,
        compiler_params=_params(("parallel", "arbitrary")),
        name="swa_attn",
    )(swa_sinks, q, kv, kv)


def _proj_ln_kernel(*refs, alpha, rows_per_batch, emit_next, cast_w, sub):
    refs = list(refs)
    w_scr = refs.pop() if cast_w else None
    if emit_next:
        a_ref, w_ref, x_ref, g_ref, lg_ref, lb_ref, sc_ref, sh_ref, xo_ref, ho_ref = refs
    else:
        a_ref, w_ref, x_ref, g_ref, lg_ref, lb_ref, xo_ref = refs
    if cast_w:
        @pl.when(pl.program_id(0) == 0)
        def _():
            w_scr[...] = w_ref[...].astype(BF16)
        w_ref = w_scr
    tm = a_ref.shape[0]
    b = (pl.program_id(0) * tm) // rows_per_batch
    gate = 1.0 + g_ref[pl.ds(b, 1), :]
    ln_g = lg_ref[...]
    ln_b = lb_ref[...]
    if emit_next:
        nsc = 1.0 + sc_ref[pl.ds(b, 1), :]
        nsh = sh_ref[pl.ds(b, 1), :]
    for c in range(tm // sub):
        sl = slice(c * sub, (c + 1) * sub)
        y = _dot(a_ref[sl, :], w_ref[...])
        z = alpha * x_ref[sl, :] + gate * y
        mu = jnp.mean(z, axis=-1, keepdims=True)
        zc = z - mu
        var = jnp.mean(zc * zc, axis=-1, keepdims=True)
        xn = zc * lax.rsqrt(var + LN_EPS) * ln_g + ln_b
        xo_ref[sl, :] = xn
        if emit_next:
            ho_ref[sl, :] = (xn * nsc + nsh).astype(BF16)


def _proj_ln(a, w, w_layer, x, mod, layer, gate_chunk, ln_g, ln_b, next_mod, *, alpha, tm):
    batch, seq, d = x.shape
    k = a.shape[-1]
    rows = batch * seq
    tm = min(tm, seq)
    emit_next = next_mod is not None
    cast_w = w.dtype != BF16
    kern = functools.partial(_proj_ln_kernel, alpha=alpha, rows_per_batch=seq, emit_next=emit_next,
                             cast_w=cast_w, sub=min(128, tm))
    row_spec = lambda width: pl.BlockSpec((tm, width), lambda i: (i, 0))
    vec_spec = _resident((None, 1, d), lambda i: (layer, 0, 0))
    in_specs = [row_spec(k), _resident((None, k, d), lambda i: (w_layer, 0, 0)), row_spec(d),
                _mod_spec(mod, layer, gate_chunk), vec_spec, vec_spec]
    depth = ln_g.shape[0]
    args = [a.reshape(rows, k), w, x.reshape(rows, d), mod, ln_g.reshape(depth, 1, d), ln_b.reshape(depth, 1, d)]
    out_specs = [row_spec(d)]
    out_shape = [jax.ShapeDtypeStruct((rows, d), F32)]
    if emit_next:
        nl, nsc, nsh = next_mod
        in_specs += [_mod_spec(mod, nl, nsc), _mod_spec(mod, nl, nsh)]
        args += [mod, mod]
        out_specs.append(row_spec(d))
        out_shape.append(jax.ShapeDtypeStruct((rows, d), BF16))
    outs = pl.pallas_call(
        kern,
        grid=(rows // tm,),
        in_specs=in_specs,
        out_specs=out_specs,
        out_shape=out_shape,
        scratch_shapes=[pltpu.VMEM((k, d), BF16)] if cast_w else [],
        compiler_params=_params(("arbitrary",)),
        name="proj_ln",
    )(*args)
    x_new = outs[0].reshape(batch, seq, d)
    return x_new, (outs[1].reshape(batch, seq, d) if emit_next else None)


def _ffn_up_kernel(h_ref, wg_ref, wv_ref, cwg_ref, cwv_ref, cbg_ref, cbv_ref, o_ref, *, chunk):
    seq = h_ref.shape[0]
    tn = wg_ref.shape[1]
    wg = wg_ref[...].astype(BF16)
    wv = wv_ref[...].astype(BF16)

    def conv(u, halo, cw_ref, cb_ref):
        ext = jnp.concatenate([halo, u], axis=0)
        u1 = pltpu.roll(ext, 1, 0)[SUBLANES:]
        u2 = pltpu.roll(ext, 2, 0)[SUBLANES:]
        return u2 * cw_ref[0:1, :] + u1 * cw_ref[1:2, :] + u * cw_ref[2:3, :] + cb_ref[...]

    def matmuls(c):
        hc = h_ref[c * chunk:(c + 1) * chunk, :]
        return _dot(hc, wg), _dot(hc, wv)

    n_chunks = seq // chunk
    halo_g = jnp.zeros((SUBLANES, tn), F32)
    halo_v = jnp.zeros((SUBLANES, tn), F32)
    u_next = matmuls(0)
    for c in range(n_chunks):
        sl = slice(c * chunk, (c + 1) * chunk)
        ug, uv = u_next
        if c + 1 < n_chunks:
            u_next = matmuls(c + 1)
        cg = conv(ug, halo_g, cwg_ref, cbg_ref)
        cv = conv(uv, halo_v, cwv_ref, cbv_ref)
        halo_g = ug[chunk - SUBLANES:]
        halo_v = uv[chunk - SUBLANES:]
        o_ref[sl, :] = (cg * jax.nn.sigmoid(cg) * cv).astype(BF16)


def _ffn_up(h, ffn_w_up, conv_w, conv_b, layer):
    batch, seq, d = h.shape
    depth, _, two_f = ffn_w_up.shape
    d_ff = two_f // 2
    tn = 512 if d_ff % 512 == 0 else LANES
    nt = d_ff // tn
    chunk = min(1024, seq)
    lo = lambda b, j: (layer, 0, j)
    hi = lambda b, j: (layer, 0, nt + j)
    conv_b = conv_b.reshape(depth, 1, two_f)
    return pl.pallas_call(
        functools.partial(_ffn_up_kernel, chunk=chunk),
        grid=(batch, nt),
        in_specs=[
            pl.BlockSpec((None, seq, d), lambda b, j: (b, 0, 0)),
            pl.BlockSpec((None, d, tn), lo), pl.BlockSpec((None, d, tn), hi),
            pl.BlockSpec((None, CONV_WIDTH, tn), lo), pl.BlockSpec((None, CONV_WIDTH, tn), hi),
            pl.BlockSpec((None, 1, tn), lo), pl.BlockSpec((None, 1, tn), hi),
        ],
        out_specs=pl.BlockSpec((None, seq, tn), lambda b, j: (b, 0, j)),
        out_shape=jax.ShapeDtypeStruct((batch, seq, d_ff), BF16),
        compiler_params=_params(("parallel", "arbitrary")),
        name="ffn_up",
    )(h, ffn_w_up, ffn_w_up, conv_w, conv_w, conv_b, conv_b)


def kernel(x, c, positions, fox_w_in, fox_b_f, fox_w_o, swa_w_in, swa_sinks, swa_w_o, ada_w, ada_b,
           ffn_w_up, ffn_conv_w, ffn_conv_b, ffn_w_down, ln_mix_g, ln_mix_b, ln_ffn_g, ln_ffn_b):
    batch, seq, d = x.shape
    depth = ada_w.shape[0]
    alpha = (2.0 * depth) ** 0.25
    fox_heads = fox_b_f.shape[1]
    fox_dh = d // fox_heads

    mod = _ada(c, ada_w, ada_b)
    rope_tables = _rope_tables(positions) if depth > 1 else None
    w_down = ffn_w_down.astype(BF16)
    fox_w = _fox_w_bf16(fox_w_in)

    h = None
    for i in range(depth):
        j = i // 2
        if i % 2 == 0:
            b_f = jnp.pad(fox_b_f[j], (0, LANES - fox_heads)).reshape(1, LANES)
            if h is None:
                qkv, fl = _inproj_fox(x, mod, i, fox_w, j, b_f, head_dim=fox_dh)
            else:
                qkv, fl = _inproj_fox(h, None, i, fox_w, j, b_f, head_dim=fox_dh)
            ccol, crow = _fox_cum(fl, fox_heads)
            o = _fox_attn(qkv, crow, ccol, heads=fox_heads)
            w_o = fox_w_o
        else:
            if h is None:
                raise NotImplementedError("SWA as the first layer")
            q, kv = _inproj_swa(h, swa_w_in, j, rope_tables, q_heads=swa_sinks.shape[1])
            o = _swa_attn(q, kv, swa_sinks, j)
            w_o = swa_w_o
        x, h2 = _proj_ln(o, w_o, j, x, mod, i, G1, ln_mix_g, ln_mix_b, (i, SC2, SH2), alpha=alpha, tm=512)
        hmid = _ffn_up(h2, ffn_w_up, ffn_conv_w, ffn_conv_b, i)
        next_mod = (i + 1, SC1, SH1) if i + 1 < depth else None
        x, h = _proj_ln(hmid, w_down, i, x, mod, i, G2, ln_ffn_g, ln_ffn_b, next_mod, alpha=alpha, tm=256)
    return x
```

```python
import functools

import jax
import jax.numpy as jnp
from jax import lax
from jax.experimental import pallas as pl
from jax.experimental.pallas import tpu as pltpu

F32 = jnp.float32
BF16 = jnp.bfloat16

LANES = 128
SUBLANES = 8
VMEM_LIMIT_BYTES = 56 * 1024 * 1024

LN_EPS = 1e-5
ROPE_THETA = 500000.0
ROPE_DIM = 16
ROPE_HALF = ROPE_DIM // 2
SWA_HEAD_DIM = 64
SWA_WINDOW = 128
CONV_WIDTH = 3
MASK_VALUE = -1e30
LOG2E = 1.4426950408889634

SH1, SC1, G1, SH2, SC2, G2 = range(6)


def _params(semantics):
    return pltpu.CompilerParams(dimension_semantics=semantics, vmem_limit_bytes=VMEM_LIMIT_BYTES)


def _dot(a, b):
    return jnp.dot(a, b, preferred_element_type=F32)


def _dot_nt(a, b):
    return lax.dot_general(a, b, (((1,), (1,)), ((), ())), preferred_element_type=F32)


def _resident(block_shape, index_map):
    return pl.BlockSpec(block_shape, index_map, pipeline_mode=pl.Buffered(1))


def _mod_spec(mod, layer, chunk):
    _, _, batch, d = mod.shape
    return _resident((None, None, batch, d), lambda *_: (layer, chunk, 0, 0))


def _ada_kernel(c_ref, w_ref, b_ref, o_ref):
    c = c_ref[...]
    c_act = (c * jax.nn.sigmoid(c)).astype(BF16)
    o_ref[...] = _dot(c_act, w_ref[...].astype(BF16)) + b_ref[...]


def _ada(c, ada_w, ada_b):
    depth, d, six_d = ada_w.shape
    batch = c.shape[0]
    tn = min(1024, d)
    nt = d // tn
    return pl.pallas_call(
        _ada_kernel,
        grid=(depth, 6, nt),
        in_specs=[
            pl.BlockSpec((batch, d), lambda i, k, j: (0, 0)),
            pl.BlockSpec((None, d, tn), lambda i, k, j: (i, 0, k * nt + j)),
            pl.BlockSpec((None, 1, tn), lambda i, k, j: (i, 0, k * nt + j)),
        ],
        out_specs=pl.BlockSpec((None, None, batch, tn), lambda i, k, j: (i, k, 0, j)),
        out_shape=jax.ShapeDtypeStruct((depth, 6, batch, d), F32),
        compiler_params=_params(("parallel", "parallel", "parallel")),
        name="ada_mod",
    )(c, ada_w, ada_b.reshape(depth, 1, six_d))


def _cast_cols_kernel(wt_ref, o_ref, *, valid_cols):
    tn = wt_ref.shape[0]
    w = wt_ref[...].T
    col = pl.program_id(1) * tn + lax.broadcasted_iota(jnp.int32, w.shape, 1)
    o_ref[...] = jnp.where(col < valid_cols, w, 0.0).astype(BF16)


def _fox_w_bf16(fox_w_in):
    n, d, cols = fox_w_in.shape
    tn = 512
    nt = pl.cdiv(cols, tn)
    return pl.pallas_call(
        functools.partial(_cast_cols_kernel, valid_cols=cols),
        grid=(n, nt),
        in_specs=[pl.BlockSpec((None, tn, d), lambda l, j: (l, j, 0))],
        out_specs=pl.BlockSpec((None, d, tn), lambda l, j: (l, 0, j)),
        out_shape=jax.ShapeDtypeStruct((n, d, nt * tn), BF16),
        compiler_params=_params(("parallel", "parallel")),
        name="fox_w_cast",
    )(jnp.swapaxes(fox_w_in, 1, 2))


def _inproj_fox_kernel(*refs, modulate, tiles_per_seq, q_tiles, q_scale):
    if modulate:
        x_ref, sc_ref, sh_ref, w_ref, wf_ref, bf_ref, qkv_ref, fl_ref, h_scr = refs
    else:
        x_ref, w_ref, wf_ref, bf_ref, qkv_ref, fl_ref = refs
    i = pl.program_id(0)
    j = pl.program_id(1)

    if modulate:
        @pl.when(j == 0)
        def _():
            b = i // tiles_per_seq
            sc = sc_ref[pl.ds(b, 1), :]
            sh = sh_ref[pl.ds(b, 1), :]
            h_scr[...] = (x_ref[...] * (1.0 + sc) + sh).astype(BF16)
        h_ref = h_scr
    else:
        h_ref = x_ref

    @pl.when(j == 0)
    def _():
        fl_ref[...] = _dot(h_ref[...], wf_ref[...]) + bf_ref[...]

    acc = _dot(h_ref[...], w_ref[...])
    acc = acc * jnp.where(j < q_tiles, q_scale, 1.0)
    for hh in range(acc.shape[1] // LANES):
        qkv_ref[hh] = acc[:, hh * LANES:(hh + 1) * LANES].astype(BF16)


def _inproj_fox(x_or_h, mod, layer, fox_w, fox_layer, b_f, *, head_dim):
    batch, seq, d = x_or_h.shape
    modulate = mod is not None
    n_out = 3 * d
    assert head_dim == LANES
    tn = min(1024, d)
    tm = min(1024 if modulate else 2048, seq)
    nm = seq // tm
    slots = n_out // LANES
    kern = functools.partial(_inproj_fox_kernel, modulate=modulate, tiles_per_seq=nm,
                             q_tiles=d // tn, q_scale=LOG2E * float(head_dim) ** -0.5)
    in_specs = [pl.BlockSpec((None, tm, d), lambda i, j: (i // nm, i % nm, 0))]
    args = [x_or_h]
    if modulate:
        in_specs += [_mod_spec(mod, layer, SC1), _mod_spec(mod, layer, SH1)]
        args += [mod, mod]
    in_specs += [
        pl.BlockSpec((None, d, tn), lambda i, j: (fox_layer, 0, j)),
        _resident((None, d, LANES), lambda i, j: (fox_layer, 0, n_out // LANES)),
        _resident((1, LANES), lambda i, j: (0, 0)),
    ]
    args += [fox_w, fox_w, b_f]
    return pl.pallas_call(
        kern,
        grid=(batch * nm, n_out // tn),
        in_specs=in_specs,
        out_specs=[
            pl.BlockSpec((None, tn // LANES, tm, LANES), lambda i, j: (i // nm, j, i % nm, 0)),
            pl.BlockSpec((None, tm, LANES), lambda i, j: (i // nm, i % nm, 0)),
        ],
        out_shape=[
            jax.ShapeDtypeStruct((batch, slots, seq, LANES), BF16),
            jax.ShapeDtypeStruct((batch, seq, LANES), F32),
        ],
        scratch_shapes=[pltpu.VMEM((tm, d), BF16)] if modulate else [],
        compiler_params=_params(("parallel", "arbitrary")),
        name="fox_inproj",
    )(*args)


def _fox_cum_kernel(fl_ref, col_ref, row_ref, *, heads):
    seq = fl_ref.shape[0]
    x = fl_ref[...]
    log_f = (jnp.minimum(x, 0.0) - jnp.log1p(jnp.exp(-jnp.abs(x)))) * LOG2E
    r = lax.broadcasted_iota(jnp.int32, (LANES, LANES), 0)
    c = lax.broadcasted_iota(jnp.int32, (LANES, LANES), 1)
    tri = (c <= r).astype(BF16)
    carry = jnp.zeros((1, LANES), F32)
    for blk in range(seq // LANES):
        xb = log_f[blk * LANES:(blk + 1) * LANES]
        hi = xb.astype(BF16)
        rem = xb - hi.astype(F32)
        mid = rem.astype(BF16)
        lo = (rem - mid.astype(F32)).astype(BF16)
        cb = _dot(tri, hi) + _dot(tri, mid) + _dot(tri, lo) + carry
        col_ref[blk * LANES:(blk + 1) * LANES, :] = cb
        carry = cb[LANES - 1:LANES, :]
    row_ref[...] = col_ref[...].T[:heads]


def _fox_cum(fl, heads):
    batch, seq, _ = fl.shape
    return pl.pallas_call(
        functools.partial(_fox_cum_kernel, heads=heads),
        grid=(batch,),
        in_specs=[pl.BlockSpec((None, seq, LANES), lambda b: (b, 0, 0))],
        out_specs=[
            pl.BlockSpec((None, seq, LANES), lambda b: (b, 0, 0)),
            pl.BlockSpec((None, heads, seq), lambda b: (b, 0, 0)),
        ],
        out_shape=[
            jax.ShapeDtypeStruct((batch, seq, LANES), F32),
            jax.ShapeDtypeStruct((batch, heads, seq), F32),
        ],
        compiler_params=_params(("parallel",)),
        name="fox_cum",
    )(fl)


def _fox_attn_kernel(q_ref, k_ref, v_ref, crow_ref, ccol_ref, o_ref, *, heads_per_step, blk):
    seq = q_ref.shape[1]
    hb = pl.program_id(1)
    lane = lax.broadcasted_iota(jnp.int32, (blk, LANES), 1)
    rows = lax.broadcasted_iota(jnp.int32, (blk, blk), 0)
    cols = lax.broadcasted_iota(jnp.int32, (blk, blk), 1)
    causal = cols <= rows

    def scores(hh, qi):
        h = hb * heads_per_step + hh
        q0 = qi * blk
        lk = q0 + blk
        s = _dot_nt(q_ref[hh, q0:lk, :], k_ref[hh, :lk, :]) - crow_ref[pl.ds(h, 1), :lk]
        diag = jnp.where(causal, s[:, q0:], MASK_VALUE)
        return diag if qi == 0 else jnp.concatenate([s[:, :q0], diag], axis=1)

    def finish(hh, qi, s):
        h = hb * heads_per_step + hh
        q0 = qi * blk
        lk = q0 + blk
        cq = jnp.sum(jnp.where(lane == h, ccol_ref[q0:lk, :], 0.0), axis=-1, keepdims=True)
        m = jnp.max(s, axis=-1, keepdims=True) + cq
        p = jnp.exp2(s + (cq - m))
        l = jnp.sum(p, axis=-1, keepdims=True)
        acc = _dot(p.astype(BF16), v_ref[hh, :lk, :])
        o_ref[q0:lk, hh * LANES:(hh + 1) * LANES] = (acc / l).astype(BF16)

    work = [(hh, qi) for hh in range(heads_per_step) for qi in range(seq // blk)]
    s_next = scores(*work[0])
    for idx, item in enumerate(work):
        s_cur = s_next
        if idx + 1 < len(work):
            s_next = scores(*work[idx + 1])
        finish(*item, s_cur)


def _fox_attn(qkv, crow, ccol, *, heads):
    batch, slots, seq, dh = qkv.shape
    assert slots == 3 * heads and dh == LANES
    hps = 4 if heads % 4 == 0 else 1
    groups = heads // hps
    blk = min(256, seq)
    kern = functools.partial(_fox_attn_kernel, heads_per_step=hps, blk=blk)
    head_spec = lambda off: pl.BlockSpec((None, hps, seq, dh), lambda b, g: (b, off + g, 0, 0))
    return pl.pallas_call(
        kern,
        grid=(batch, groups),
        in_specs=[
            head_spec(0), head_spec(groups), head_spec(2 * groups),
            pl.BlockSpec((None, heads, seq), lambda b, g: (b, 0, 0)),
            pl.BlockSpec((None, seq, LANES), lambda b, g: (b, 0, 0)),
        ],
        out_specs=pl.BlockSpec((None, seq, hps * dh), lambda b, g: (b, 0, g)),
        out_shape=jax.ShapeDtypeStruct((batch, seq, heads * dh), BF16),
        compiler_params=_params(("parallel", "parallel")),
        name="fox_attn",
    )(qkv, qkv, qkv, crow, ccol)


def _rope_table_kernel(pos_ref, inv_ref, o_ref):
    pos = pos_ref[...].astype(F32)
    ang = pos * inv_ref[...]
    cos = jnp.cos(ang)
    sin = jnp.sin(ang)
    idx = lax.broadcasted_iota(jnp.int32, ang.shape, 1) % SWA_HEAD_DIM
    o_ref[0] = jnp.where(idx < ROPE_DIM, cos, 1.0)
    o_ref[1] = jnp.where((idx >= ROPE_HALF) & (idx < ROPE_DIM), sin, 0.0)
    o_ref[2] = jnp.where(idx < ROPE_HALF, -sin, 0.0)


def _rope_tables(positions):
    batch, seq = positions.shape
    inv_freq = ROPE_THETA ** (-jnp.arange(0, ROPE_DIM, 2, dtype=F32) / ROPE_DIM)
    idx = jnp.arange(LANES) % SWA_HEAD_DIM
    inv_lane = jnp.where(idx < ROPE_DIM, inv_freq[idx % ROPE_HALF], 0.0).reshape(1, LANES)
    return pl.pallas_call(
        _rope_table_kernel,
        grid=(batch,),
        in_specs=[
            pl.BlockSpec((None, seq, 1), lambda b: (b, 0, 0)),
            pl.BlockSpec((1, LANES), lambda b: (0, 0)),
        ],
        out_specs=pl.BlockSpec((3, None, seq, LANES), lambda b: (0, b, 0, 0)),
        out_shape=jax.ShapeDtypeStruct((3, batch, seq, LANES), F32),
        compiler_params=_params(("parallel",)),
        name="rope_tables",
    )(positions.reshape(batch, seq, 1), inv_lane)


def _inproj_swa_kernel(h_ref, wq_ref, wkv_ref, rt_ref, q_ref, kv_ref, *, q_tiles, q_scale, k_width):
    j = pl.program_id(1)
    seq = h_ref.shape[0]
    chunk = min(512, seq)
    n_chunks = seq // chunk

    def project(w_ref, store):
        w = w_ref[...].astype(BF16)
        acc_next = _dot(h_ref[0:chunk, :], w)
        for c in range(n_chunks):
            rows = slice(c * chunk, (c + 1) * chunk)
            acc = acc_next
            if c + 1 < n_chunks:
                acc_next = _dot(h_ref[(c + 1) * chunk:(c + 2) * chunk, :], w)
            cos = rt_ref[0, rows, :]
            sin_lo = rt_ref[1, rows, :]
            sin_hi = rt_ref[2, rows, :]
            for cc in range(acc.shape[1] // LANES):
                a = acc[:, cc * LANES:(cc + 1) * LANES]
                rot = (a * cos + pltpu.roll(a, ROPE_HALF, 1) * sin_lo
                       + pltpu.roll(a, LANES - ROPE_HALF, 1) * sin_hi)
                store(rows, cc, a, rot)

    @pl.when(j < q_tiles)
    def _():
        def store(rows, cc, a, rot):
            q_ref[rows, cc * LANES:(cc + 1) * LANES] = (rot * q_scale).astype(BF16)
        project(wq_ref, store)

    @pl.when(j == q_tiles)
    def _():
        def store(rows, cc, a, rot):
            kv_ref[rows, cc * LANES:(cc + 1) * LANES] = (rot if cc * LANES < k_width else a).astype(BF16)
        project(wkv_ref, store)


def _inproj_swa(h, swa_w_in, swa_layer, rope_tables, *, q_heads):
    batch, seq, d = h.shape
    nq = q_heads * SWA_HEAD_DIM
    kvw = swa_w_in.shape[2] - nq
    assert (kvw // 2) % LANES == 0
    tn = min(512, nq)
    assert nq % tn == 0 and nq % kvw == 0
    q_tiles = nq // tn
    kern = functools.partial(_inproj_swa_kernel, q_tiles=q_tiles,
                             q_scale=LOG2E * float(SWA_HEAD_DIM) ** -0.5, k_width=kvw // 2)
    return pl.pallas_call(
        kern,
        grid=(batch, q_tiles + 1),
        in_specs=[
            pl.BlockSpec((None, seq, d), lambda b, j: (b, 0, 0)),
            pl.BlockSpec((None, d, tn), lambda b, j: (swa_layer, 0, jnp.minimum(j, q_tiles - 1))),
            _resident((None, d, kvw), lambda b, j: (swa_layer, 0, nq // kvw)),
            pl.BlockSpec((3, None, seq, LANES), lambda b, j: (0, b, 0, 0)),
        ],
        out_specs=[
            pl.BlockSpec((None, seq, tn), lambda b, j: (b, 0, jnp.minimum(j, q_tiles - 1))),
            pl.BlockSpec((None, seq, kvw), lambda b, j: (b, 0, 0)),
        ],
        out_shape=[
            jax.ShapeDtypeStruct((batch, seq, nq), BF16),
            jax.ShapeDtypeStruct((batch, seq, kvw), BF16),
        ],
        compiler_params=_params(("parallel", "arbitrary")),
        name="swa_inproj",
    )(h, swa_w_in, swa_w_in, rope_tables)


def _swa_attn_kernel(sink_ref, q_ref, k_ref, v_ref, o_ref, klo_scr, khi_scr, vlo_scr, vhi_scr, *,
                     group, layer):
    seq = q_ref.shape[0]
    blk = SWA_WINDOW
    g = pl.program_id(1)
    lane_s = lax.broadcasted_iota(jnp.int32, (seq, LANES), 1)
    hi_half = lane_s >= SWA_HEAD_DIM
    own_half = (lane_s // SWA_HEAD_DIM) == (g % 2)
    kp = k_ref[...].astype(F32)
    vp = v_ref[...].astype(F32)
    k2 = jnp.where(own_half, kp, pltpu.roll(kp, SWA_HEAD_DIM, 1))
    v2 = jnp.where(own_half, vp, pltpu.roll(vp, SWA_HEAD_DIM, 1))
    klo_scr[...] = jnp.where(hi_half, 0.0, k2).astype(BF16)
    khi_scr[...] = jnp.where(hi_half, k2, 0.0).astype(BF16)
    vlo_scr[...] = jnp.where(hi_half, 0.0, v2).astype(BF16)
    vhi_scr[...] = jnp.where(hi_half, v2, 0.0).astype(BF16)

    r = lax.broadcasted_iota(jnp.int32, (blk, 2 * blk), 0)
    c = lax.broadcasted_iota(jnp.int32, (blk, 2 * blk), 1)
    rel_first = r - c
    rel_rest = r + blk - c
    bias_first = jnp.where((rel_first >= 0) & (rel_first < SWA_WINDOW), 0.0, MASK_VALUE)
    bias_rest = jnp.where((rel_rest >= 0) & (rel_rest < SWA_WINDOW), 0.0, MASK_VALUE)

    def q_block(n, _):
        q0 = pl.multiple_of(n * blk, blk)
        ks = pl.multiple_of(jnp.maximum(n - 1, 0) * blk, blk)
        bias = jnp.where(n == 0, bias_first, bias_rest)
        keys = (klo_scr[pl.ds(ks, 2 * blk), :], khi_scr[pl.ds(ks, 2 * blk), :])
        vals = (vlo_scr[pl.ds(ks, 2 * blk), :], vhi_scr[pl.ds(ks, 2 * blk), :])
        for pair in range(group // 2):
            qp = q_ref[pl.ds(q0, blk), pair * LANES:(pair + 1) * LANES]
            out = jnp.zeros((blk, LANES), F32)
            for e in range(2):
                sink = sink_ref[layer, g * group + 2 * pair + e] * LOG2E
                s = _dot_nt(qp, keys[e]) + bias
                m = jnp.maximum(jnp.max(s, axis=-1, keepdims=True), sink)
                p = jnp.exp2(s - m)
                denom = jnp.sum(p, axis=-1, keepdims=True) + jnp.exp2(sink - m)
                out = out + _dot(p.astype(BF16), vals[e]) / denom
            o_ref[pl.ds(q0, blk), pair * LANES:(pair + 1) * LANES] = out.astype(BF16)
        return 0

    lax.fori_loop(0, seq // blk, q_block, 0, unroll=8)


def _swa_attn(q, kv, swa_sinks, swa_layer):
    batch, seq, nq = q.shape
    kvw = kv.shape[2]
    kv_heads = kvw // (2 * SWA_HEAD_DIM)
    q_heads = nq // SWA_HEAD_DIM
    group = q_heads // kv_heads
    gw = group * SWA_HEAD_DIM
    assert group % 2 == 0 and gw % LANES == 0 and kv_heads % 2 == 0
    v_off = kv_heads // 2
    return pl.pallas_call(
        functools.partial(_swa_attn_kernel, group=group, layer=swa_layer),
        grid=(batch, kv_heads),
        in_specs=[
            pl.BlockSpec(memory_space=pltpu.SMEM),
            pl.BlockSpec((None, seq, gw), lambda b, g: (b, 0, g)),
            pl.BlockSpec((None, seq, LANES), lambda b, g: (b, 0, g // 2)),
            pl.BlockSpec((None, seq, LANES), lambda b, g: (b, 0, v_off + g // 2)),
        ],
        out_specs=pl.BlockSpec((None, seq, gw), lambda b, g: (b, 0, g)),
        out_shape=jax.ShapeDtypeStruct((batch, seq, nq), BF16),
        scratch_shapes=[pltpu.VMEM((seq, LANES), BF16)] * 4,
        compiler_params=_params(("parallel", "arbitrary")),
        name="swa_attn",
    )(swa_sinks, q, kv, kv)


def _proj_ln_kernel(*refs, alpha, rows_per_batch, emit_next, cast_w, sub):
    refs = list(refs)
    w_scr = refs.pop() if cast_w else None
    if emit_next:
        a_ref, w_ref, x_ref, g_ref, lg_ref, lb_ref, sc_ref, sh_ref, xo_ref, ho_ref = refs
    else:
        a_ref, w_ref, x_ref, g_ref, lg_ref, lb_ref, xo_ref = refs
    if cast_w:
        @pl.when(pl.program_id(0) == 0)
        def _():
            w_scr[...] = w_ref[...].astype(BF16)
        w_ref = w_scr
    tm = a_ref.shape[0]
    b = (pl.program_id(0) * tm) // rows_per_batch
    gate = 1.0 + g_ref[pl.ds(b, 1), :]
    ln_g = lg_ref[...]
    ln_b = lb_ref[...]
    if emit_next:
        nsc = 1.0 + sc_ref[pl.ds(b, 1), :]
        nsh = sh_ref[pl.ds(b, 1), :]
    for c in range(tm // sub):
        sl = slice(c * sub, (c + 1) * sub)
        y = _dot(a_ref[sl, :], w_ref[...])
        z = alpha * x_ref[sl, :] + gate * y
        mu = jnp.mean(z, axis=-1, keepdims=True)
        zc = z - mu
        var = jnp.mean(zc * zc, axis=-1, keepdims=True)
        xn = zc * lax.rsqrt(var + LN_EPS) * ln_g + ln_b
        xo_ref[sl, :] = xn
        if emit_next:
            ho_ref[sl, :] = (xn * nsc + nsh).astype(BF16)


def _proj_ln(a, w, w_layer, x, mod, layer, gate_chunk, ln_g, ln_b, next_mod, *, alpha, tm):
    batch, seq, d = x.shape
    k = a.shape[-1]
    rows = batch * seq
    tm = min(tm, seq)
    emit_next = next_mod is not None
    cast_w = w.dtype != BF16
    kern = functools.partial(_proj_ln_kernel, alpha=alpha, rows_per_batch=seq, emit_next=emit_next,
                             cast_w=cast_w, sub=min(128, tm))
    row_spec = lambda width: pl.BlockSpec((tm, width), lambda i: (i, 0))
    vec_spec = _resident((None, 1, d), lambda i: (layer, 0, 0))
    in_specs = [row_spec(k), _resident((None, k, d), lambda i: (w_layer, 0, 0)), row_spec(d),
                _mod_spec(mod, layer, gate_chunk), vec_spec, vec_spec]
    depth = ln_g.shape[0]
    args = [a.reshape(rows, k), w, x.reshape(rows, d), mod, ln_g.reshape(depth, 1, d), ln_b.reshape(depth, 1, d)]
    out_specs = [row_spec(d)]
    out_shape = [jax.ShapeDtypeStruct((rows, d), F32)]
    if emit_next:
        nl, nsc, nsh = next_mod
        in_specs += [_mod_spec(mod, nl, nsc), _mod_spec(mod, nl, nsh)]
        args += [mod, mod]
        out_specs.append(row_spec(d))
        out_shape.append(jax.ShapeDtypeStruct((rows, d), BF16))
    outs = pl.pallas_call(
        kern,
        grid=(rows // tm,),
        in_specs=in_specs,
        out_specs=out_specs,
        out_shape=out_shape,
        scratch_shapes=[pltpu.VMEM((k, d), BF16)] if cast_w else [],
        compiler_params=_params(("arbitrary",)),
        name="proj_ln",
    )(*args)
    x_new = outs[0].reshape(batch, seq, d)
    return x_new, (outs[1].reshape(batch, seq, d) if emit_next else None)


def _ffn_up_kernel(h_ref, wg_ref, wv_ref, cwg_ref, cwv_ref, cbg_ref, cbv_ref, o_ref, *, chunk):
    seq = h_ref.shape[0]
    tn = wg_ref.shape[1]
    wg = wg_ref[...].astype(BF16)
    wv = wv_ref[...].astype(BF16)

    def conv(u, halo, cw_ref, cb_ref):
        ext = jnp.concatenate([halo, u], axis=0)
        u1 = pltpu.roll(ext, 1, 0)[SUBLANES:]
        u2 = pltpu.roll(ext, 2, 0)[SUBLANES:]
        return u2 * cw_ref[0:1, :] + u1 * cw_ref[1:2, :] + u * cw_ref[2:3, :] + cb_ref[...]

    def matmuls(c):
        hc = h_ref[c * chunk:(c + 1) * chunk, :]
        return _dot(hc, wg), _dot(hc, wv)

    n_chunks = seq // chunk
    halo_g = jnp.zeros((SUBLANES, tn), F32)
    halo_v = jnp.zeros((SUBLANES, tn), F32)
    u_next = matmuls(0)
    for c in range(n_chunks):
        sl = slice(c * chunk, (c + 1) * chunk)
        ug, uv = u_next
        if c + 1 < n_chunks:
            u_next = matmuls(c + 1)
        cg = conv(ug, halo_g, cwg_ref, cbg_ref)
        cv = conv(uv, halo_v, cwv_ref, cbv_ref)
        halo_g = ug[chunk - SUBLANES:]
        halo_v = uv[chunk - SUBLANES:]
        o_ref[sl, :] = (cg * jax.nn.sigmoid(cg) * cv).astype(BF16)


def _ffn_up(h, ffn_w_up, conv_w, conv_b, layer):
    batch, seq, d = h.shape
    depth, _, two_f = ffn_w_up.shape
    d_ff = two_f // 2
    tn = 512 if d_ff % 512 == 0 else LANES
    nt = d_ff // tn
    chunk = min(1024, seq)
    lo = lambda b, j: (layer, 0, j)
    hi = lambda b, j: (layer, 0, nt + j)
    conv_b = conv_b.reshape(depth, 1, two_f)
    return pl.pallas_call(
        functools.partial(_ffn_up_kernel, chunk=chunk),
        grid=(batch, nt),
        in_specs=[
            pl.BlockSpec((None, seq, d), lambda b, j: (b, 0, 0)),
            pl.BlockSpec((None, d, tn), lo), pl.BlockSpec((None, d, tn), hi),
            pl.BlockSpec((None, CONV_WIDTH, tn), lo), pl.BlockSpec((None, CONV_WIDTH, tn), hi),
            pl.BlockSpec((None, 1, tn), lo), pl.BlockSpec((None, 1, tn), hi),
        ],
        out_specs=pl.BlockSpec((None, seq, tn), lambda b, j: (b, 0, j)),
        out_shape=jax.ShapeDtypeStruct((batch, seq, d_ff), BF16),
        compiler_params=_params(("parallel", "arbitrary")),
        name="ffn_up",
    )(h, ffn_w_up, ffn_w_up, conv_w, conv_w, conv_b, conv_b)


def kernel(x, c, positions, fox_w_in, fox_b_f, fox_w_o, swa_w_in, swa_sinks, swa_w_o, ada_w, ada_b,
           ffn_w_up, ffn_conv_w, ffn_conv_b, ffn_w_down, ln_mix_g, ln_mix_b, ln_ffn_g, ln_ffn_b):
    batch, seq, d = x.shape
    depth = ada_w.shape[0]
    alpha = (2.0 * depth) ** 0.25
    fox_heads = fox_b_f.shape[1]
    fox_dh = d // fox_heads

    mod = _ada(c, ada_w, ada_b)
    rope_tables = _rope_tables(positions) if depth > 1 else None
    w_down = ffn_w_down.astype(BF16)
    fox_w = _fox_w_bf16(fox_w_in)

    h = None
    for i in range(depth):
        j = i // 2
        if i % 2 == 0:
            b_f = jnp.pad(fox_b_f[j], (0, LANES - fox_heads)).reshape(1, LANES)
            if h is None:
                qkv, fl = _inproj_fox(x, mod, i, fox_w, j, b_f, head_dim=fox_dh)
            else:
                qkv, fl = _inproj_fox(h, None, i, fox_w, j, b_f, head_dim=fox_dh)
            ccol, crow = _fox_cum(fl, fox_heads)
            o = _fox_attn(qkv, crow, ccol, heads=fox_heads)
            w_o = fox_w_o
        else:
            if h is None:
                raise NotImplementedError("SWA as the first layer")
            q, kv = _inproj_swa(h, swa_w_in, j, rope_tables, q_heads=swa_sinks.shape[1])
            o = _swa_attn(q, kv, swa_sinks, j)
            w_o = swa_w_o
        x, h2 = _proj_ln(o, w_o, j, x, mod, i, G1, ln_mix_g, ln_mix_b, (i, SC2, SH2), alpha=alpha, tm=512)
        hmid = _ffn_up(h2, ffn_w_up, ffn_conv_w, ffn_conv_b, i)
        next_mod = (i + 1, SC1, SH1) if i + 1 < depth else None
        x, h = _proj_ln(hmid, w_down, i, x, mod, i, G2, ln_ffn_g, ln_ffn_b, next_mod, alpha=alpha, tm=256)
    return x
```

```python
import functools

import jax
import jax.numpy as jnp
from jax import lax
from jax.experimental import pallas as pl
from jax.experimental.pallas import tpu as pltpu

F32 = jnp.float32
BF16 = jnp.bfloat16

LANES = 128
SUBLANES = 8
VMEM_LIMIT_BYTES = 56 * 1024 * 1024

LN_EPS = 1e-5
ROPE_THETA = 500000.0
ROPE_DIM = 16
ROPE_HALF = ROPE_DIM // 2
SWA_HEAD_DIM = 64
SWA_WINDOW = 128
CONV_WIDTH = 3
MASK_VALUE = -1e30
LOG2E = 1.4426950408889634

SH1, SC1, G1, SH2, SC2, G2 = range(6)


def _params(semantics):
    return pltpu.CompilerParams(dimension_semantics=semantics, vmem_limit_bytes=VMEM_LIMIT_BYTES)


def _dot(a, b):
    return jnp.dot(a, b, preferred_element_type=F32)


def _dot_nt(a, b):
    return lax.dot_general(a, b, (((1,), (1,)), ((), ())), preferred_element_type=F32)


def _resident(block_shape, index_map):
    return pl.BlockSpec(block_shape, index_map, pipeline_mode=pl.Buffered(1))


def _mod_spec(mod, layer, chunk):
    _, _, batch, d = mod.shape
    return _resident((None, None, batch, d), lambda *_: (layer, chunk, 0, 0))


def _ada_kernel(c_ref, w_ref, b_ref, o_ref):
    c = c_ref[...]
    c_act = (c * jax.nn.sigmoid(c)).astype(BF16)
    o_ref[...] = _dot(c_act, w_ref[...].astype(BF16)) + b_ref[...]


def _ada(c, ada_w, ada_b):
    depth, d, six_d = ada_w.shape
    batch = c.shape[0]
    tn = min(1024, d)
    nt = d // tn
    return pl.pallas_call(
        _ada_kernel,
        grid=(depth, 6, nt),
        in_specs=[
            pl.BlockSpec((batch, d), lambda i, k, j: (0, 0)),
            pl.BlockSpec((None, d, tn), lambda i, k, j: (i, 0, k * nt + j)),
            pl.BlockSpec((None, 1, tn), lambda i, k, j: (i, 0, k * nt + j)),
        ],
        out_specs=pl.BlockSpec((None, None, batch, tn), lambda i, k, j: (i, k, 0, j)),
        out_shape=jax.ShapeDtypeStruct((depth, 6, batch, d), F32),
        compiler_params=_params(("parallel", "parallel", "parallel")),
        name="ada_mod",
    )(c, ada_w, ada_b.reshape(depth, 1, six_d))


def _cast_cols_kernel(wt_ref, o_ref, *, valid_cols):
    tn = wt_ref.shape[0]
    w = wt_ref[...].T
    col = pl.program_id(1) * tn + lax.broadcasted_iota(jnp.int32, w.shape, 1)
    o_ref[...] = jnp.where(col < valid_cols, w, 0.0).astype(BF16)


def _fox_w_bf16(fox_w_in):
    n, d, cols = fox_w_in.shape
    tn = 512
    nt = pl.cdiv(cols, tn)
    return pl.pallas_call(
        functools.partial(_cast_cols_kernel, valid_cols=cols),
        grid=(n, nt),
        in_specs=[pl.BlockSpec((None, tn, d), lambda l, j: (l, j, 0))],
        out_specs=pl.BlockSpec((None, d, tn), lambda l, j: (l, 0, j)),
        out_shape=jax.ShapeDtypeStruct((n, d, nt * tn), BF16),
        compiler_params=_params(("parallel", "parallel")),
        name="fox_w_cast",
    )(jnp.swapaxes(fox_w_in, 1, 2))


def _inproj_fox_kernel(*refs, modulate, tiles_per_seq, q_tiles, q_scale):
    if modulate:
        x_ref, sc_ref, sh_ref, w_ref, wf_ref, bf_ref, qkv_ref, fl_ref, h_scr = refs
    else:
        x_ref, w_ref, wf_ref, bf_ref, qkv_ref, fl_ref = refs
    i = pl.program_id(0)
    j = pl.program_id(1)

    if modulate:
        @pl.when(j == 0)
        def _():
            b = i // tiles_per_seq
            sc = sc_ref[pl.ds(b, 1), :]
            sh = sh_ref[pl.ds(b, 1), :]
            h_scr[...] = (x_ref[...] * (1.0 + sc) + sh).astype(BF16)
        h_ref = h_scr
    else:
        h_ref = x_ref

    @pl.when(j == 0)
    def _():
        fl_ref[...] = _dot(h_ref[...], wf_ref[...]) + bf_ref[...]

    acc = _dot(h_ref[...], w_ref[...])
    acc = acc * jnp.where(j < q_tiles, q_scale, 1.0)
    for hh in range(acc.shape[1] // LANES):
        qkv_ref[hh] = acc[:, hh * LANES:(hh + 1) * LANES].astype(BF16)


def _inproj_fox(x_or_h, mod, layer, fox_w, fox_layer, b_f, *, head_dim):
    batch, seq, d = x_or_h.shape
    modulate = mod is not None
    n_out = 3 * d
    assert head_dim == LANES
    tn = min(1024, d)
    tm = min(1024 if modulate else 2048, seq)
    nm = seq // tm
    slots = n_out // LANES
    kern = functools.partial(_inproj_fox_kernel, modulate=modulate, tiles_per_seq=nm,
                             q_tiles=d // tn, q_scale=LOG2E * float(head_dim) ** -0.5)
    in_specs = [pl.BlockSpec((None, tm, d), lambda i, j: (i // nm, i % nm, 0))]
    args = [x_or_h]
    if modulate:
        in_specs += [_mod_spec(mod, layer, SC1), _mod_spec(mod, layer, SH1)]
        args += [mod, mod]
    in_specs += [
        pl.BlockSpec((None, d, tn), lambda i, j: (fox_layer, 0, j)),
        _resident((None, d, LANES), lambda i, j: (fox_layer, 0, n_out // LANES)),
        _resident((1, LANES), lambda i, j: (0, 0)),
    ]
    args += [fox_w, fox_w, b_f]
    return pl.pallas_call(
        kern,
        grid=(batch * nm, n_out // tn),
        in_specs=in_specs,
        out_specs=[
            pl.BlockSpec((None, tn // LANES, tm, LANES), lambda i, j: (i // nm, j, i % nm, 0)),
            pl.BlockSpec((None, tm, LANES), lambda i, j: (i // nm, i % nm, 0)),
        ],
        out_shape=[
            jax.ShapeDtypeStruct((batch, slots, seq, LANES), BF16),
            jax.ShapeDtypeStruct((batch, seq, LANES), F32),
        ],
        scratch_shapes=[pltpu.VMEM((tm, d), BF16)] if modulate else [],
        compiler_params=_params(("parallel", "arbitrary")),
        name="fox_inproj",
    )(*args)


def _fox_cum_kernel(fl_ref, col_ref, row_ref, *, heads):
    seq = fl_ref.shape[0]
    x = fl_ref[...]
    log_f = (jnp.minimum(x, 0.0) - jnp.log1p(jnp.exp(-jnp.abs(x)))) * LOG2E
    r = lax.broadcasted_iota(jnp.int32, (LANES, LANES), 0)
    c = lax.broadcasted_iota(jnp.int32, (LANES, LANES), 1)
    tri = (c <= r).astype(BF16)
    carry = jnp.zeros((1, LANES), F32)
    for blk in range(seq // LANES):
        xb = log_f[blk * LANES:(blk + 1) * LANES]
        hi = xb.astype(BF16)
        rem = xb - hi.astype(F32)
        mid = rem.astype(BF16)
        lo = (rem - mid.astype(F32)).astype(BF16)
        cb = _dot(tri, hi) + _dot(tri, mid) + _dot(tri, lo) + carry
        col_ref[blk * LANES:(blk + 1) * LANES, :] = cb
        carry = cb[LANES - 1:LANES, :]
    row_ref[...] = col_ref[...].T[:heads]


def _fox_cum(fl, heads):
    batch, seq, _ = fl.shape
    return pl.pallas_call(
        functools.partial(_fox_cum_kernel, heads=heads),
        grid=(batch,),
        in_specs=[pl.BlockSpec((None, seq, LANES), lambda b: (b, 0, 0))],
        out_specs=[
            pl.BlockSpec((None, seq, LANES), lambda b: (b, 0, 0)),
            pl.BlockSpec((None, heads, seq), lambda b: (b, 0, 0)),
        ],
        out_shape=[
            jax.ShapeDtypeStruct((batch, seq, LANES), F32),
            jax.ShapeDtypeStruct((batch, heads, seq), F32),
        ],
        compiler_params=_params(("parallel",)),
        name="fox_cum",
    )(fl)


def _fox_attn_kernel(q_ref, k_ref, v_ref, crow_ref, ccol_ref, o_ref, *, heads_per_step, blk):
    seq = q_ref.shape[1]
    hb = pl.program_id(1)
    lane = lax.broadcasted_iota(jnp.int32, (blk, LANES), 1)
    rows = lax.broadcasted_iota(jnp.int32, (blk, blk), 0)
    cols = lax.broadcasted_iota(jnp.int32, (blk, blk), 1)
    causal = cols <= rows

    def scores(hh, qi):
        h = hb * heads_per_step + hh
        q0 = qi * blk
        lk = q0 + blk
        s = _dot_nt(q_ref[hh, q0:lk, :], k_ref[hh, :lk, :]) - crow_ref[pl.ds(h, 1), :lk]
        diag = jnp.where(causal, s[:, q0:], MASK_VALUE)
        return diag if qi == 0 else jnp.concatenate([s[:, :q0], diag], axis=1)

    def finish(hh, qi, s):
        h = hb * heads_per_step + hh
        q0 = qi * blk
        lk = q0 + blk
        cq = jnp.sum(jnp.where(lane == h, ccol_ref[q0:lk, :], 0.0), axis=-1, keepdims=True)
        m = jnp.max(s, axis=-1, keepdims=True) + cq
        p = jnp.exp2(s + (cq - m))
        l = jnp.sum(p, axis=-1, keepdims=True)
        acc = _dot(p.astype(BF16), v_ref[hh, :lk, :])
        o_ref[q0:lk, hh * LANES:(hh + 1) * LANES] = (acc / l).astype(BF16)

    work = [(hh, qi) for hh in range(heads_per_step) for qi in range(seq // blk)]
    s_next = scores(*work[0])
    for idx, item in enumerate(work):
        s_cur = s_next
        if idx + 1 < len(work):
            s_next = scores(*work[idx + 1])
        finish(*item, s_cur)


def _fox_attn(qkv, crow, ccol, *, heads):
    batch, slots, seq, dh = qkv.shape
    assert slots == 3 * heads and dh == LANES
    hps = 4 if heads % 4 == 0 else 1
    groups = heads // hps
    blk = min(256, seq)
    kern = functools.partial(_fox_attn_kernel, heads_per_step=hps, blk=blk)
    head_spec = lambda off: pl.BlockSpec((None, hps, seq, dh), lambda b, g: (b, off + g, 0, 0))
    return pl.pallas_call(
        kern,
        grid=(batch, groups),
        in_specs=[
            head_spec(0), head_spec(groups), head_spec(2 * groups),
            pl.BlockSpec((None, heads, seq), lambda b, g: (b, 0, 0)),
            pl.BlockSpec((None, seq, LANES), lambda b, g: (b, 0, 0)),
        ],
        out_specs=pl.BlockSpec((None, seq, hps * dh), lambda b, g: (b, 0, g)),
        out_shape=jax.ShapeDtypeStruct((batch, seq, heads * dh), BF16),
        compiler_params=_params(("parallel", "parallel")),
        name="fox_attn",
    )(qkv, qkv, qkv, crow, ccol)


def _rope_table_kernel(pos_ref, inv_ref, o_ref):
    pos = pos_ref[...].astype(F32)
    ang = pos * inv_ref[...]
    cos = jnp.cos(ang)
    sin = jnp.sin(ang)
    idx = lax.broadcasted_iota(jnp.int32, ang.shape, 1) % SWA_HEAD_DIM
    o_ref[0] = jnp.where(idx < ROPE_DIM, cos, 1.0)
    o_ref[1] = jnp.where((idx >= ROPE_HALF) & (idx < ROPE_DIM), sin, 0.0)
    o_ref[2] = jnp.where(idx < ROPE_HALF, -sin, 0.0)


def _rope_tables(positions):
    batch, seq = positions.shape
    inv_freq = ROPE_THETA ** (-jnp.arange(0, ROPE_DIM, 2, dtype=F32) / ROPE_DIM)
    idx = jnp.arange(LANES) % SWA_HEAD_DIM
    inv_lane = jnp.where(idx < ROPE_DIM, inv_freq[idx % ROPE_HALF], 0.0).reshape(1, LANES)
    return pl.pallas_call(
        _rope_table_kernel,
        grid=(batch,),
        in_specs=[
            pl.BlockSpec((None, seq, 1), lambda b: (b, 0, 0)),
            pl.BlockSpec((1, LANES), lambda b: (0, 0)),
        ],
        out_specs=pl.BlockSpec((3, None, seq, LANES), lambda b: (0, b, 0, 0)),
        out_shape=jax.ShapeDtypeStruct((3, batch, seq, LANES), F32),
        compiler_params=_params(("parallel",)),
        name="rope_tables",
    )(positions.reshape(batch, seq, 1), inv_lane)


def _inproj_swa_kernel(h_ref, wq_ref, wkv_ref, rt_ref, q_ref, kv_ref, *, q_tiles, q_scale, k_width):
    j = pl.program_id(1)
    seq = h_ref.shape[0]
    chunk = min(512, seq)
    n_chunks = seq // chunk

    def project(w_ref, store):
        w = w_ref[...].astype(BF16)
        acc_next = _dot(h_ref[0:chunk, :], w)
        for c in range(n_chunks):
            rows = slice(c * chunk, (c + 1) * chunk)
            acc = acc_next
            if c + 1 < n_chunks:
                acc_next = _dot(h_ref[(c + 1) * chunk:(c + 2) * chunk, :], w)
            cos = rt_ref[0, rows, :]
            sin_lo = rt_ref[1, rows, :]
            sin_hi = rt_ref[2, rows, :]
            for cc in range(acc.shape[1] // LANES):
                a = acc[:, cc * LANES:(cc + 1) * LANES]
                rot = (a * cos + pltpu.roll(a, ROPE_HALF, 1) * sin_lo
                       + pltpu.roll(a, LANES - ROPE_HALF, 1) * sin_hi)
                store(rows, cc, a, rot)

    @pl.when(j < q_tiles)
    def _():
        def store(rows, cc, a, rot):
            q_ref[rows, cc * LANES:(cc + 1) * LANES] = (rot * q_scale).astype(BF16)
        project(wq_ref, store)

    @pl.when(j == q_tiles)
    def _():
        def store(rows, cc, a, rot):
            kv_ref[rows, cc * LANES:(cc + 1) * LANES] = (rot if cc * LANES < k_width else a).astype(BF16)
        project(wkv_ref, store)


def _inproj_swa(h, swa_w_in, swa_layer, rope_tables, *, q_heads):
    batch, seq, d = h.shape
    nq = q_heads * SWA_HEAD_DIM
    kvw = swa_w_in.shape[2] - nq
    assert (kvw // 2) % LANES == 0
    tn = min(512, nq)
    assert nq % tn == 0 and nq % kvw == 0
    q_tiles = nq // tn
    kern = functools.partial(_inproj_swa_kernel, q_tiles=q_tiles,
                             q_scale=LOG2E * float(SWA_HEAD_DIM) ** -0.5, k_width=kvw // 2)
    return pl.pallas_call(
        kern,
        grid=(batch, q_tiles + 1),
        in_specs=[
            pl.BlockSpec((None, seq, d), lambda b, j: (b, 0, 0)),
            pl.BlockSpec((None, d, tn), lambda b, j: (swa_layer, 0, jnp.minimum(j, q_tiles - 1))),
            _resident((None, d, kvw), lambda b, j: (swa_layer, 0, nq // kvw)),
            pl.BlockSpec((3, None, seq, LANES), lambda b, j: (0, b, 0, 0)),
        ],
        out_specs=[
            pl.BlockSpec((None, seq, tn), lambda b, j: (b, 0, jnp.minimum(j, q_tiles - 1))),
            pl.BlockSpec((None, seq, kvw), lambda b, j: (b, 0, 0)),
        ],
        out_shape=[
            jax.ShapeDtypeStruct((batch, seq, nq), BF16),
            jax.ShapeDtypeStruct((batch, seq, kvw), BF16),
        ],
        compiler_params=_params(("parallel", "arbitrary")),
        name="swa_inproj",
    )(h, swa_w_in, swa_w_in, rope_tables)


def _swa_attn_kernel(sink_ref, q_ref, k_ref, v_ref, o_ref, klo_scr, khi_scr, vlo_scr, vhi_scr, *,
                     group, layer):
    seq = q_ref.shape[0]
    blk = SWA_WINDOW
    g = pl.program_id(1)
    lane_s = lax.broadcasted_iota(jnp.int32, (seq, LANES), 1)
    hi_half = lane_s >= SWA_HEAD_DIM
    own_half = (lane_s // SWA_HEAD_DIM) == (g % 2)
    kp = k_ref[...].astype(F32)
    vp = v_ref[...].astype(F32)
    k2 = jnp.where(own_half, kp, pltpu.roll(kp, SWA_HEAD_DIM, 1))
    v2 = jnp.where(own_half, vp, pltpu.roll(vp, SWA_HEAD_DIM, 1))
    klo_scr[...] = jnp.where(hi_half, 0.0, k2).astype(BF16)
    khi_scr[...] = jnp.where(hi_half, k2, 0.0).astype(BF16)
    vlo_scr[...] = jnp.where(hi_half, 0.0, v2).astype(BF16)
    vhi_scr[...] = jnp.where(hi_half, v2, 0.0).astype(BF16)

    r = lax.broadcasted_iota(jnp.int32, (blk, 2 * blk), 0)
    c = lax.broadcasted_iota(jnp.int32, (blk, 2 * blk), 1)
    rel_first = r - c
    rel_rest = r + blk - c
    bias_first = jnp.where((rel_first >= 0) & (rel_first < SWA_WINDOW), 0.0, MASK_VALUE)
    bias_rest = jnp.where((rel_rest >= 0) & (rel_rest < SWA_WINDOW), 0.0, MASK_VALUE)

    def q_block(n, _):
        q0 = pl.multiple_of(n * blk, blk)
        ks = pl.multiple_of(jnp.maximum(n - 1, 0) * blk, blk)
        bias = jnp.where(n == 0, bias_first, bias_rest)
        keys = (klo_scr[pl.ds(ks, 2 * blk), :], khi_scr[pl.ds(ks, 2 * blk), :])
        vals = (vlo_scr[pl.ds(ks, 2 * blk), :], vhi_scr[pl.ds(ks, 2 * blk), :])
        for pair in range(group // 2):
            qp = q_ref[pl.ds(q0, blk), pair * LANES:(pair + 1) * LANES]
            out = jnp.zeros((blk, LANES), F32)
            for e in range(2):
                sink = sink_ref[layer, g * group + 2 * pair + e] * LOG2E
                s = _dot_nt(qp, keys[e]) + bias
                m = jnp.maximum(jnp.max(s, axis=-1, keepdims=True), sink)
                p = jnp.exp2(s - m)
                denom = jnp.sum(p, axis=-1, keepdims=True) + jnp.exp2(sink - m)
                out = out + _dot(p.astype(BF16), vals[e]) / denom
            o_ref[pl.ds(q0, blk), pair * LANES:(pair + 1) * LANES] = out.astype(BF16)
        return 0

    lax.fori_loop(0, seq // blk, q_block, 0, unroll=8)


def _swa_attn(q, kv, swa_sinks, swa_layer):
    batch, seq, nq = q.shape
    kvw = kv.shape[2]
    kv_heads = kvw // (2 * SWA_HEAD_DIM)
    q_heads = nq // SWA_HEAD_DIM
    group = q_heads // kv_heads
    gw = group * SWA_HEAD_DIM
    assert group % 2 == 0 and gw % LANES == 0 and kv_heads % 2 == 0
    v_off = kv_heads // 2
    return pl.pallas_call(
        functools.partial(_swa_attn_kernel, group=group, layer=swa_layer),
        grid=(batch, kv_heads),
        in_specs=[
            pl.BlockSpec(memory_space=pltpu.SMEM),
            pl.BlockSpec((None, seq, gw), lambda b, g: (b, 0, g)),
            pl.BlockSpec((None, seq, LANES), lambda b, g: (b, 0, g // 2)),
            pl.BlockSpec((None, seq, LANES), lambda b, g: (b, 0, v_off + g // 2)),
        ],
        out_specs=pl.BlockSpec((None, seq, gw), lambda b, g: (b, 0, g)),
        out_shape=jax.ShapeDtypeStruct((batch, seq, nq), BF16),
        scratch_shapes=[pltpu.VMEM((seq, LANES), BF16)] * 4,
        compiler_params=_params(("parallel", "arbitrary")),
        name="swa_attn",
    )(swa_sinks, q, kv, kv)


def _proj_ln_kernel(*refs, alpha, rows_per_batch, emit_next, cast_w, sub):
    refs = list(refs)
    w_scr = refs.pop() if cast_w else None
    if emit_next:
        a_ref, w_ref, x_ref, g_ref, lg_ref, lb_ref, sc_ref, sh_ref, xo_ref, ho_ref = refs
    else:
        a_ref, w_ref, x_ref, g_ref, lg_ref, lb_ref, xo_ref = refs
    if cast_w:
        @pl.when(pl.program_id(0) == 0)
        def _():
            w_scr[...] = w_ref[...].astype(BF16)
        w_ref = w_scr
    tm = a_ref.shape[0]
    b = (pl.program_id(0) * tm) // rows_per_batch
    gate = 1.0 + g_ref[pl.ds(b, 1), :]
    ln_g = lg_ref[...]
    ln_b = lb_ref[...]
    if emit_next:
        nsc = 1.0 + sc_ref[pl.ds(b, 1), :]
        nsh = sh_ref[pl.ds(b, 1), :]
    for c in range(tm // sub):
        sl = slice(c * sub, (c + 1) * sub)
        y = _dot(a_ref[sl, :], w_ref[...])
        z = alpha * x_ref[sl, :] + gate * y
        mu = jnp.mean(z, axis=-1, keepdims=True)
        zc = z - mu
        var = jnp.mean(zc * zc, axis=-1, keepdims=True)
        xn = zc * lax.rsqrt(var + LN_EPS) * ln_g + ln_b
        xo_ref[sl, :] = xn
        if emit_next:
            ho_ref[sl, :] = (xn * nsc + nsh).astype(BF16)


def _proj_ln(a, w, w_layer, x, mod, layer, gate_chunk, ln_g, ln_b, next_mod, *, alpha, tm):
    batch, seq, d = x.shape
    k = a.shape[-1]
    rows = batch * seq
    tm = min(tm, seq)
    emit_next = next_mod is not None
    cast_w = w.dtype != BF16
    kern = functools.partial(_proj_ln_kernel, alpha=alpha, rows_per_batch=seq, emit_next=emit_next,
                             cast_w=cast_w, sub=min(128, tm))
    row_spec = lambda width: pl.BlockSpec((tm, width), lambda i: (i, 0))
    vec_spec = _resident((None, 1, d), lambda i: (layer, 0, 0))
    in_specs = [row_spec(k), _resident((None, k, d), lambda i: (w_layer, 0, 0)), row_spec(d),
                _mod_spec(mod, layer, gate_chunk), vec_spec, vec_spec]
    depth = ln_g.shape[0]
    args = [a.reshape(rows, k), w, x.reshape(rows, d), mod, ln_g.reshape(depth, 1, d), ln_b.reshape(depth, 1, d)]
    out_specs = [row_spec(d)]
    out_shape = [jax.ShapeDtypeStruct((rows, d), F32)]
    if emit_next:
        nl, nsc, nsh = next_mod
        in_specs += [_mod_spec(mod, nl, nsc), _mod_spec(mod, nl, nsh)]
        args += [mod, mod]
        out_specs.append(row_spec(d))
        out_shape.append(jax.ShapeDtypeStruct((rows, d), BF16))
    outs = pl.pallas_call(
        kern,
        grid=(rows // tm,),
        in_specs=in_specs,
        out_specs=out_specs,
        out_shape=out_shape,
        scratch_shapes=[pltpu.VMEM((k, d), BF16)] if cast_w else [],
        compiler_params=_params(("arbitrary",)),
        name="proj_ln",
    )(*args)
    x_new = outs[0].reshape(batch, seq, d)
    return x_new, (outs[1].reshape(batch, seq, d) if emit_next else None)


def _ffn_up_kernel(h_ref, wg_ref, wv_ref, cwg_ref, cwv_ref, cbg_ref, cbv_ref, o_ref, *, chunk):
    seq = h_ref.shape[0]
    tn = wg_ref.shape[1]
    wg = wg_ref[...].astype(BF16)
    wv = wv_ref[...].astype(BF16)

    def conv(u, halo, cw_ref, cb_ref):
        ext = jnp.concatenate([halo, u], axis=0)
        u1 = pltpu.roll(ext, 1, 0)[SUBLANES:]
        u2 = pltpu.roll(ext, 2, 0)[SUBLANES:]
        return u2 * cw_ref[0:1, :] + u1 * cw_ref[1:2, :] + u * cw_ref[2:3, :] + cb_ref[...]

    bounds = list(range(0, seq - chunk + 1, chunk)) if seq > chunk else [0]
    bounds += [seq - chunk // 2, seq] if seq > chunk else [seq]
    spans = list(zip(bounds[:-1], bounds[1:]))

    def matmuls(span):
        hc = h_ref[span[0]:span[1], :]
        return _dot(hc, wg), _dot(hc, wv)

    halo_g = jnp.zeros((SUBLANES, tn), F32)
    halo_v = jnp.zeros((SUBLANES, tn), F32)
    u_next = matmuls(spans[0])
    for c, (r0, r1) in enumerate(spans):
        ug, uv = u_next
        if c + 1 < len(spans):
            u_next = matmuls(spans[c + 1])
        cg = conv(ug, halo_g, cwg_ref, cbg_ref)
        cv = conv(uv, halo_v, cwv_ref, cbv_ref)
        halo_g = ug[r1 - r0 - SUBLANES:]
        halo_v = uv[r1 - r0 - SUBLANES:]
        o_ref[r0:r1, :] = (cg * jax.nn.sigmoid(cg) * cv).astype(BF16)


def _ffn_up(h, ffn_w_up, conv_w, conv_b, layer):
    batch, seq, d = h.shape
    depth, _, two_f = ffn_w_up.shape
    d_ff = two_f // 2
    tn = 512 if d_ff % 512 == 0 else LANES
    nt = d_ff // tn
    chunk = min(1024, seq)
    lo = lambda b, j: (layer, 0, j)
    hi = lambda b, j: (layer, 0, nt + j)
    conv_b = conv_b.reshape(depth, 1, two_f)
    return pl.pallas_call(
        functools.partial(_ffn_up_kernel, chunk=chunk),
        grid=(batch, nt),
        in_specs=[
            pl.BlockSpec((None, seq, d), lambda b, j: (b, 0, 0)),
            pl.BlockSpec((None, d, tn), lo), pl.BlockSpec((None, d, tn), hi),
            pl.BlockSpec((None, CONV_WIDTH, tn), lo), pl.BlockSpec((None, CONV_WIDTH, tn), hi),
            pl.BlockSpec((None, 1, tn), lo), pl.BlockSpec((None, 1, tn), hi),
        ],
        out_specs=pl.BlockSpec((None, seq, tn), lambda b, j: (b, 0, j)),
        out_shape=jax.ShapeDtypeStruct((batch, seq, d_ff), BF16),
        compiler_params=_params(("parallel", "arbitrary")),
        name="ffn_up",
    )(h, ffn_w_up, ffn_w_up, conv_w, conv_w, conv_b, conv_b)


def kernel(x, c, positions, fox_w_in, fox_b_f, fox_w_o, swa_w_in, swa_sinks, swa_w_o, ada_w, ada_b,
           ffn_w_up, ffn_conv_w, ffn_conv_b, ffn_w_down, ln_mix_g, ln_mix_b, ln_ffn_g, ln_ffn_b):
    batch, seq, d = x.shape
    depth = ada_w.shape[0]
    alpha = (2.0 * depth) ** 0.25
    fox_heads = fox_b_f.shape[1]
    fox_dh = d // fox_heads

    mod = _ada(c, ada_w, ada_b)
    rope_tables = _rope_tables(positions) if depth > 1 else None
    w_down = ffn_w_down.astype(BF16)
    fox_w = _fox_w_bf16(fox_w_in)

    h = None
    for i in range(depth):
        j = i // 2
        if i % 2 == 0:
            b_f = jnp.pad(fox_b_f[j], (0, LANES - fox_heads)).reshape(1, LANES)
            if h is None:
                qkv, fl = _inproj_fox(x, mod, i, fox_w, j, b_f, head_dim=fox_dh)
            else:
                qkv, fl = _inproj_fox(h, None, i, fox_w, j, b_f, head_dim=fox_dh)
            ccol, crow = _fox_cum(fl, fox_heads)
            o = _fox_attn(qkv, crow, ccol, heads=fox_heads)
            w_o = fox_w_o
        else:
            if h is None:
                raise NotImplementedError("SWA as the first layer")
            q, kv = _inproj_swa(h, swa_w_in, j, rope_tables, q_heads=swa_sinks.shape[1])
            o = _swa_attn(q, kv, swa_sinks, j)
            w_o = swa_w_o
        x, h2 = _proj_ln(o, w_o, j, x, mod, i, G1, ln_mix_g, ln_mix_b, (i, SC2, SH2), alpha=alpha, tm=512)
        hmid = _ffn_up(h2, ffn_w_up, ffn_conv_w, ffn_conv_b, i)
        next_mod = (i + 1, SC1, SH1) if i + 1 < depth else None
        x, h = _proj_ln(hmid, w_down, i, x, mod, i, G2, ln_ffn_g, ln_ffn_b, next_mod, alpha=alpha, tm=256)
    return x
```

```python
import functools

import jax
import jax.numpy as jnp
from jax import lax
from jax.experimental import pallas as pl
from jax.experimental.pallas import tpu as pltpu

F32 = jnp.float32
BF16 = jnp.bfloat16

LANES = 128
SUBLANES = 8
VMEM_LIMIT_BYTES = 56 * 1024 * 1024

LN_EPS = 1e-5
ROPE_THETA = 500000.0
ROPE_DIM = 16
ROPE_HALF = ROPE_DIM // 2
SWA_HEAD_DIM = 64
SWA_WINDOW = 128
CONV_WIDTH = 3
MASK_VALUE = -1e30
LOG2E = 1.4426950408889634

SH1, SC1, G1, SH2, SC2, G2 = range(6)


def _params(semantics):
    return pltpu.CompilerParams(dimension_semantics=semantics, vmem_limit_bytes=VMEM_LIMIT_BYTES)


def _dot(a, b):
    return jnp.dot(a, b, preferred_element_type=F32)


def _dot_nt(a, b):
    return lax.dot_general(a, b, (((1,), (1,)), ((), ())), preferred_element_type=F32)


def _resident(block_shape, index_map):
    return pl.BlockSpec(block_shape, index_map, pipeline_mode=pl.Buffered(1))


def _mod_spec(mod, layer, chunk):
    _, _, batch, d = mod.shape
    return _resident((None, None, batch, d), lambda *_: (layer, chunk, 0, 0))


def _ada_kernel(c_ref, w_ref, b_ref, o_ref):
    c = c_ref[...]
    c_act = (c * jax.nn.sigmoid(c)).astype(BF16)
    o_ref[...] = _dot(c_act, w_ref[...].astype(BF16)) + b_ref[...]


def _ada(c, ada_w, ada_b):
    depth, d, six_d = ada_w.shape
    batch = c.shape[0]
    tn = min(1024, d)
    nt = d // tn
    return pl.pallas_call(
        _ada_kernel,
        grid=(depth, 6, nt),
        in_specs=[
            pl.BlockSpec((batch, d), lambda i, k, j: (0, 0)),
            pl.BlockSpec((None, d, tn), lambda i, k, j: (i, 0, k * nt + j)),
            pl.BlockSpec((None, 1, tn), lambda i, k, j: (i, 0, k * nt + j)),
        ],
        out_specs=pl.BlockSpec((None, None, batch, tn), lambda i, k, j: (i, k, 0, j)),
        out_shape=jax.ShapeDtypeStruct((depth, 6, batch, d), F32),
        compiler_params=_params(("parallel", "parallel", "parallel")),
        name="ada_mod",
    )(c, ada_w, ada_b.reshape(depth, 1, six_d))


def _cast_cols_kernel(wt_ref, o_ref, *, valid_cols):
    tn = wt_ref.shape[0]
    w = wt_ref[...].T
    col = pl.program_id(1) * tn + lax.broadcasted_iota(jnp.int32, w.shape, 1)
    o_ref[...] = jnp.where(col < valid_cols, w, 0.0).astype(BF16)


def _fox_w_bf16(fox_w_in):
    n, d, cols = fox_w_in.shape
    tn = 512
    nt = pl.cdiv(cols, tn)
    return pl.pallas_call(
        functools.partial(_cast_cols_kernel, valid_cols=cols),
        grid=(n, nt),
        in_specs=[pl.BlockSpec((None, tn, d), lambda l, j: (l, j, 0))],
        out_specs=pl.BlockSpec((None, d, tn), lambda l, j: (l, 0, j)),
        out_shape=jax.ShapeDtypeStruct((n, d, nt * tn), BF16),
        compiler_params=_params(("parallel", "parallel")),
        name="fox_w_cast",
    )(jnp.swapaxes(fox_w_in, 1, 2))


def _inproj_fox_kernel(*refs, modulate, tiles_per_seq, q_tiles, q_scale):
    if modulate:
        x_ref, sc_ref, sh_ref, w_ref, wf_ref, bf_ref, qkv_ref, fl_ref, h_scr = refs
    else:
        x_ref, w_ref, wf_ref, bf_ref, qkv_ref, fl_ref = refs
    i = pl.program_id(0)
    j = pl.program_id(1)

    if modulate:
        @pl.when(j == 0)
        def _():
            b = i // tiles_per_seq
            sc = sc_ref[pl.ds(b, 1), :]
            sh = sh_ref[pl.ds(b, 1), :]
            h_scr[...] = (x_ref[...] * (1.0 + sc) + sh).astype(BF16)
        h_ref = h_scr
    else:
        h_ref = x_ref

    @pl.when(j == 0)
    def _():
        fl_ref[...] = _dot(h_ref[...], wf_ref[...]) + bf_ref[...]

    acc = _dot(h_ref[...], w_ref[...])
    acc = acc * jnp.where(j < q_tiles, q_scale, 1.0)
    for hh in range(acc.shape[1] // LANES):
        qkv_ref[hh] = acc[:, hh * LANES:(hh + 1) * LANES].astype(BF16)


def _inproj_fox(x_or_h, mod, layer, fox_w, fox_layer, b_f, *, head_dim):
    batch, seq, d = x_or_h.shape
    modulate = mod is not None
    n_out = 3 * d
    assert head_dim == LANES
    tn = min(1024, d)
    tm = min(1024 if modulate else 2048, seq)
    nm = seq // tm
    slots = n_out // LANES
    kern = functools.partial(_inproj_fox_kernel, modulate=modulate, tiles_per_seq=nm,
                             q_tiles=d // tn, q_scale=LOG2E * float(head_dim) ** -0.5)
    in_specs = [pl.BlockSpec((None, tm, d), lambda i, j: (i // nm, i % nm, 0))]
    args = [x_or_h]
    if modulate:
        in_specs += [_mod_spec(mod, layer, SC1), _mod_spec(mod, layer, SH1)]
        args += [mod, mod]
    in_specs += [
        pl.BlockSpec((None, d, tn), lambda i, j: (fox_layer, 0, j)),
        _resident((None, d, LANES), lambda i, j: (fox_layer, 0, n_out // LANES)),
        _resident((1, LANES), lambda i, j: (0, 0)),
    ]
    args += [fox_w, fox_w, b_f]
    return pl.pallas_call(
        kern,
        grid=(batch * nm, n_out // tn),
        in_specs=in_specs,
        out_specs=[
            pl.BlockSpec((None, tn // LANES, tm, LANES), lambda i, j: (i // nm, j, i % nm, 0)),
            pl.BlockSpec((None, tm, LANES), lambda i, j: (i // nm, i % nm, 0)),
        ],
        out_shape=[
            jax.ShapeDtypeStruct((batch, slots, seq, LANES), BF16),
            jax.ShapeDtypeStruct((batch, seq, LANES), F32),
        ],
        scratch_shapes=[pltpu.VMEM((tm, d), BF16)] if modulate else [],
        compiler_params=_params(("parallel", "arbitrary")),
        name="fox_inproj",
    )(*args)


def _fox_cum_kernel(fl_ref, col_ref, row_ref, *, heads):
    seq = fl_ref.shape[0]
    x = fl_ref[...]
    log_f = (jnp.minimum(x, 0.0) - jnp.log1p(jnp.exp(-jnp.abs(x)))) * LOG2E
    r = lax.broadcasted_iota(jnp.int32, (LANES, LANES), 0)
    c = lax.broadcasted_iota(jnp.int32, (LANES, LANES), 1)
    tri = (c <= r).astype(BF16)
    carry = jnp.zeros((1, LANES), F32)
    for blk in range(seq // LANES):
        xb = log_f[blk * LANES:(blk + 1) * LANES]
        hi = xb.astype(BF16)
        rem = xb - hi.astype(F32)
        mid = rem.astype(BF16)
        lo = (rem - mid.astype(F32)).astype(BF16)
        cb = _dot(tri, hi) + _dot(tri, mid) + _dot(tri, lo) + carry
        col_ref[blk * LANES:(blk + 1) * LANES, :] = cb
        carry = cb[LANES - 1:LANES, :]
    row_ref[...] = col_ref[...].T[:heads]


def _fox_cum(fl, heads):
    batch, seq, _ = fl.shape
    return pl.pallas_call(
        functools.partial(_fox_cum_kernel, heads=heads),
        grid=(batch,),
        in_specs=[pl.BlockSpec((None, seq, LANES), lambda b: (b, 0, 0))],
        out_specs=[
            pl.BlockSpec((None, seq, LANES), lambda b: (b, 0, 0)),
            pl.BlockSpec((None, heads, seq), lambda b: (b, 0, 0)),
        ],
        out_shape=[
            jax.ShapeDtypeStruct((batch, seq, LANES), F32),
            jax.ShapeDtypeStruct((batch, heads, seq), F32),
        ],
        compiler_params=_params(("parallel",)),
        name="fox_cum",
    )(fl)


def _fox_attn_kernel(q_ref, k_ref, v_ref, crow_ref, ccol_ref, o_ref, *, heads_per_step, blk):
    seq = q_ref.shape[1]
    hb = pl.program_id(1)
    lane = lax.broadcasted_iota(jnp.int32, (blk, LANES), 1)
    rows = lax.broadcasted_iota(jnp.int32, (blk, blk), 0)
    cols = lax.broadcasted_iota(jnp.int32, (blk, blk), 1)
    causal = cols <= rows

    def scores(hh, qi):
        h = hb * heads_per_step + hh
        q0 = qi * blk
        lk = q0 + blk
        s = _dot_nt(q_ref[hh, q0:lk, :], k_ref[hh, :lk, :]) - crow_ref[pl.ds(h, 1), :lk]
        diag = jnp.where(causal, s[:, q0:], MASK_VALUE)
        return diag if qi == 0 else jnp.concatenate([s[:, :q0], diag], axis=1)

    def finish(hh, qi, s):
        h = hb * heads_per_step + hh
        q0 = qi * blk
        lk = q0 + blk
        cq = jnp.sum(jnp.where(lane == h, ccol_ref[q0:lk, :], 0.0), axis=-1, keepdims=True)
        m = jnp.max(s, axis=-1, keepdims=True) + cq
        p = jnp.exp2(s + (cq - m))
        l = jnp.sum(p, axis=-1, keepdims=True)
        acc = _dot(p.astype(BF16), v_ref[hh, :lk, :])
        o_ref[q0:lk, hh * LANES:(hh + 1) * LANES] = (acc / l).astype(BF16)

    work = [(hh, qi) for hh in range(heads_per_step) for qi in range(seq // blk)]
    s_next = scores(*work[0])
    for idx, item in enumerate(work):
        s_cur = s_next
        if idx + 1 < len(work):
            s_next = scores(*work[idx + 1])
        finish(*item, s_cur)


def _fox_attn(qkv, crow, ccol, *, heads):
    batch, slots, seq, dh = qkv.shape
    assert slots == 3 * heads and dh == LANES
    hps = 4 if heads % 4 == 0 else 1
    groups = heads // hps
    blk = min(256, seq)
    kern = functools.partial(_fox_attn_kernel, heads_per_step=hps, blk=blk)
    head_spec = lambda off: pl.BlockSpec((None, hps, seq, dh), lambda b, g: (b, off + g, 0, 0))
    return pl.pallas_call(
        kern,
        grid=(batch, groups),
        in_specs=[
            head_spec(0), head_spec(groups), head_spec(2 * groups),
            pl.BlockSpec((None, heads, seq), lambda b, g: (b, 0, 0)),
            pl.BlockSpec((None, seq, LANES), lambda b, g: (b, 0, 0)),
        ],
        out_specs=pl.BlockSpec((None, seq, hps * dh), lambda b, g: (b, 0, g)),
        out_shape=jax.ShapeDtypeStruct((batch, seq, heads * dh), BF16),
        compiler_params=_params(("parallel", "parallel")),
        name="fox_attn",
    )(qkv, qkv, qkv, crow, ccol)


def _rope_table_kernel(pos_ref, inv_ref, o_ref):
    pos = pos_ref[...].astype(F32)
    ang = pos * inv_ref[...]
    cos = jnp.cos(ang)
    sin = jnp.sin(ang)
    idx = lax.broadcasted_iota(jnp.int32, ang.shape, 1) % SWA_HEAD_DIM
    o_ref[0] = jnp.where(idx < ROPE_DIM, cos, 1.0)
    o_ref[1] = jnp.where((idx >= ROPE_HALF) & (idx < ROPE_DIM), sin, 0.0)
    o_ref[2] = jnp.where(idx < ROPE_HALF, -sin, 0.0)


def _rope_tables(positions):
    batch, seq = positions.shape
    inv_freq = ROPE_THETA ** (-jnp.arange(0, ROPE_DIM, 2, dtype=F32) / ROPE_DIM)
    idx = jnp.arange(LANES) % SWA_HEAD_DIM
    inv_lane = jnp.where(idx < ROPE_DIM, inv_freq[idx % ROPE_HALF], 0.0).reshape(1, LANES)
    return pl.pallas_call(
        _rope_table_kernel,
        grid=(batch,),
        in_specs=[
            pl.BlockSpec((None, seq, 1), lambda b: (b, 0, 0)),
            pl.BlockSpec((1, LANES), lambda b: (0, 0)),
        ],
        out_specs=pl.BlockSpec((3, None, seq, LANES), lambda b: (0, b, 0, 0)),
        out_shape=jax.ShapeDtypeStruct((3, batch, seq, LANES), F32),
        compiler_params=_params(("parallel",)),
        name="rope_tables",
    )(positions.reshape(batch, seq, 1), inv_lane)


def _inproj_swa_kernel(h_ref, wq_ref, wkv_ref, rt_ref, q_ref, kv_ref, *, q_tiles, q_scale, k_width):
    j = pl.program_id(1)
    seq = h_ref.shape[0]
    chunk = min(512, seq)
    n_chunks = seq // chunk

    def project(w_ref, store):
        w = w_ref[...].astype(BF16)
        acc_next = _dot(h_ref[0:chunk, :], w)
        for c in range(n_chunks):
            rows = slice(c * chunk, (c + 1) * chunk)
            acc = acc_next
            if c + 1 < n_chunks:
                acc_next = _dot(h_ref[(c + 1) * chunk:(c + 2) * chunk, :], w)
            cos = rt_ref[0, rows, :]
            sin_lo = rt_ref[1, rows, :]
            sin_hi = rt_ref[2, rows, :]
            for cc in range(acc.shape[1] // LANES):
                a = acc[:, cc * LANES:(cc + 1) * LANES]
                rot = (a * cos + pltpu.roll(a, ROPE_HALF, 1) * sin_lo
                       + pltpu.roll(a, LANES - ROPE_HALF, 1) * sin_hi)
                store(rows, cc, a, rot)

    @pl.when(j < q_tiles)
    def _():
        def store(rows, cc, a, rot):
            q_ref[rows, cc * LANES:(cc + 1) * LANES] = (rot * q_scale).astype(BF16)
        project(wq_ref, store)

    @pl.when(j == q_tiles)
    def _():
        def store(rows, cc, a, rot):
            kv_ref[rows, cc * LANES:(cc + 1) * LANES] = (rot if cc * LANES < k_width else a).astype(BF16)
        project(wkv_ref, store)


def _inproj_swa(h, swa_w_in, swa_layer, rope_tables, *, q_heads):
    batch, seq, d = h.shape
    nq = q_heads * SWA_HEAD_DIM
    kvw = swa_w_in.shape[2] - nq
    assert (kvw // 2) % LANES == 0
    tn = min(512, nq)
    assert nq % tn == 0 and nq % kvw == 0
    q_tiles = nq // tn
    kern = functools.partial(_inproj_swa_kernel, q_tiles=q_tiles,
                             q_scale=LOG2E * float(SWA_HEAD_DIM) ** -0.5, k_width=kvw // 2)
    return pl.pallas_call(
        kern,
        grid=(batch, q_tiles + 1),
        in_specs=[
            pl.BlockSpec((None, seq, d), lambda b, j: (b, 0, 0)),
            pl.BlockSpec((None, d, tn), lambda b, j: (swa_layer, 0, jnp.minimum(j, q_tiles - 1))),
            _resident((None, d, kvw), lambda b, j: (swa_layer, 0, nq // kvw)),
            pl.BlockSpec((3, None, seq, LANES), lambda b, j: (0, b, 0, 0)),
        ],
        out_specs=[
            pl.BlockSpec((None, seq, tn), lambda b, j: (b, 0, jnp.minimum(j, q_tiles - 1))),
            pl.BlockSpec((None, seq, kvw), lambda b, j: (b, 0, 0)),
        ],
        out_shape=[
            jax.ShapeDtypeStruct((batch, seq, nq), BF16),
            jax.ShapeDtypeStruct((batch, seq, kvw), BF16),
        ],
        compiler_params=_params(("parallel", "arbitrary")),
        name="swa_inproj",
    )(h, swa_w_in, swa_w_in, rope_tables)


def _swa_attn_kernel(sink_ref, q_ref, k_ref, v_ref, o_ref, klo_scr, khi_scr, vlo_scr, vhi_scr, *,
                     group, layer):
    seq = q_ref.shape[0]
    blk = SWA_WINDOW
    g = pl.program_id(1)
    lane_s = lax.broadcasted_iota(jnp.int32, (seq, LANES), 1)
    hi_half = lane_s >= SWA_HEAD_DIM
    own_half = (lane_s // SWA_HEAD_DIM) == (g % 2)
    kp = k_ref[...].astype(F32)
    vp = v_ref[...].astype(F32)
    k2 = jnp.where(own_half, kp, pltpu.roll(kp, SWA_HEAD_DIM, 1))
    v2 = jnp.where(own_half, vp, pltpu.roll(vp, SWA_HEAD_DIM, 1))
    klo_scr[...] = jnp.where(hi_half, 0.0, k2).astype(BF16)
    khi_scr[...] = jnp.where(hi_half, k2, 0.0).astype(BF16)
    vlo_scr[...] = jnp.where(hi_half, 0.0, v2).astype(BF16)
    vhi_scr[...] = jnp.where(hi_half, v2, 0.0).astype(BF16)

    r = lax.broadcasted_iota(jnp.int32, (blk, 2 * blk), 0)
    c = lax.broadcasted_iota(jnp.int32, (blk, 2 * blk), 1)
    rel_first = r - c
    rel_rest = r + blk - c
    bias_first = jnp.where((rel_first >= 0) & (rel_first < SWA_WINDOW), 0.0, MASK_VALUE)
    bias_rest = jnp.where((rel_rest >= 0) & (rel_rest < SWA_WINDOW), 0.0, MASK_VALUE)

    def q_block(n, _):
        q0 = pl.multiple_of(n * blk, blk)
        ks = pl.multiple_of(jnp.maximum(n - 1, 0) * blk, blk)
        bias = jnp.where(n == 0, bias_first, bias_rest)
        keys = (klo_scr[pl.ds(ks, 2 * blk), :], khi_scr[pl.ds(ks, 2 * blk), :])
        vals = (vlo_scr[pl.ds(ks, 2 * blk), :], vhi_scr[pl.ds(ks, 2 * blk), :])
        for pair in range(group // 2):
            qp = q_ref[pl.ds(q0, blk), pair * LANES:(pair + 1) * LANES]
            out = jnp.zeros((blk, LANES), F32)
            for e in range(2):
                sink = sink_ref[layer, g * group + 2 * pair + e] * LOG2E
                s = _dot_nt(qp, keys[e]) + bias
                m = jnp.maximum(jnp.max(s, axis=-1, keepdims=True), sink)
                p = jnp.exp2(s - m)
                denom = jnp.sum(p, axis=-1, keepdims=True) + jnp.exp2(sink - m)
                out = out + _dot(p.astype(BF16), vals[e]) / denom
            o_ref[pl.ds(q0, blk), pair * LANES:(pair + 1) * LANES] = out.astype(BF16)
        return 0

    lax.fori_loop(0, seq // blk, q_block, 0, unroll=8)


def _swa_attn(q, kv, swa_sinks, swa_layer):
    batch, seq, nq = q.shape
    kvw = kv.shape[2]
    kv_heads = kvw // (2 * SWA_HEAD_DIM)
    q_heads = nq // SWA_HEAD_DIM
    group = q_heads // kv_heads
    gw = group * SWA_HEAD_DIM
    assert group % 2 == 0 and gw % LANES == 0 and kv_heads % 2 == 0
    v_off = kv_heads // 2
    return pl.pallas_call(
        functools.partial(_swa_attn_kernel, group=group, layer=swa_layer),
        grid=(batch, kv_heads),
        in_specs=[
            pl.BlockSpec(memory_space=pltpu.SMEM),
            pl.BlockSpec((None, seq, gw), lambda b, g: (b, 0, g)),
            pl.BlockSpec((None, seq, LANES), lambda b, g: (b, 0, g // 2)),
            pl.BlockSpec((None, seq, LANES), lambda b, g: (b, 0, v_off + g // 2)),
        ],
        out_specs=pl.BlockSpec((None, seq, gw), lambda b, g: (b, 0, g)),
        out_shape=jax.ShapeDtypeStruct((batch, seq, nq), BF16),
        scratch_shapes=[pltpu.VMEM((seq, LANES), BF16)] * 4,
        compiler_params=_params(("parallel", "arbitrary")),
        name="swa_attn",
    )(swa_sinks, q, kv, kv)


def _proj_ln_kernel(*refs, alpha, rows_per_batch, emit_next, cast_w, sub):
    refs = list(refs)
    w_scr = refs.pop() if cast_w else None
    if emit_next:
        a_ref, w_ref, x_ref, g_ref, lg_ref, lb_ref, sc_ref, sh_ref, xo_ref, ho_ref = refs
    else:
        a_ref, w_ref, x_ref, g_ref, lg_ref, lb_ref, xo_ref = refs
    if cast_w:
        @pl.when(pl.program_id(0) == 0)
        def _():
            w_scr[...] = w_ref[...].astype(BF16)
        w_ref = w_scr
    tm = a_ref.shape[0]
    b = (pl.program_id(0) * tm) // rows_per_batch
    gate = 1.0 + g_ref[pl.ds(b, 1), :]
    ln_g = lg_ref[...]
    ln_b = lb_ref[...]
    if emit_next:
        nsc = 1.0 + sc_ref[pl.ds(b, 1), :]
        nsh = sh_ref[pl.ds(b, 1), :]
    for c in range(tm // sub):
        sl = slice(c * sub, (c + 1) * sub)
        y = _dot(a_ref[sl, :], w_ref[...])
        z = alpha * x_ref[sl, :] + gate * y
        mu = jnp.mean(z, axis=-1, keepdims=True)
        zc = z - mu
        var = jnp.mean(zc * zc, axis=-1, keepdims=True)
        xn = zc * lax.rsqrt(var + LN_EPS) * ln_g + ln_b
        xo_ref[sl, :] = xn
        if emit_next:
            ho_ref[sl, :] = (xn * nsc + nsh).astype(BF16)


def _proj_ln(a, w, w_layer, x, mod, layer, gate_chunk, ln_g, ln_b, next_mod, *, alpha, tm):
    batch, seq, d = x.shape
    k = a.shape[-1]
    rows = batch * seq
    tm = min(tm, seq)
    emit_next = next_mod is not None
    cast_w = w.dtype != BF16
    kern = functools.partial(_proj_ln_kernel, alpha=alpha, rows_per_batch=seq, emit_next=emit_next,
                             cast_w=cast_w, sub=min(256 if cast_w else 128, tm))
    row_spec = lambda width: pl.BlockSpec((tm, width), lambda i: (i, 0))
    vec_spec = _resident((None, 1, d), lambda i: (layer, 0, 0))
    in_specs = [row_spec(k), _resident((None, k, d), lambda i: (w_layer, 0, 0)), row_spec(d),
                _mod_spec(mod, layer, gate_chunk), vec_spec, vec_spec]
    depth = ln_g.shape[0]
    args = [a.reshape(rows, k), w, x.reshape(rows, d), mod, ln_g.reshape(depth, 1, d), ln_b.reshape(depth, 1, d)]
    out_specs = [row_spec(d)]
    out_shape = [jax.ShapeDtypeStruct((rows, d), F32)]
    if emit_next:
        nl, nsc, nsh = next_mod
        in_specs += [_mod_spec(mod, nl, nsc), _mod_spec(mod, nl, nsh)]
        args += [mod, mod]
        out_specs.append(row_spec(d))
        out_shape.append(jax.ShapeDtypeStruct((rows, d), BF16))
    outs = pl.pallas_call(
        kern,
        grid=(rows // tm,),
        in_specs=in_specs,
        out_specs=out_specs,
        out_shape=out_shape,
        scratch_shapes=[pltpu.VMEM((k, d), BF16)] if cast_w else [],
        compiler_params=_params(("arbitrary",)),
        name="proj_ln",
    )(*args)
    x_new = outs[0].reshape(batch, seq, d)
    return x_new, (outs[1].reshape(batch, seq, d) if emit_next else None)


def _ffn_up_kernel(h_ref, wg_ref, wv_ref, cwg_ref, cwv_ref, cbg_ref, cbv_ref, o_ref, *, chunk):
    seq = h_ref.shape[0]
    tn = wg_ref.shape[1]
    wg = wg_ref[...].astype(BF16)
    wv = wv_ref[...].astype(BF16)

    def conv(u, halo, cw_ref, cb_ref):
        ext = jnp.concatenate([halo, u], axis=0)
        u1 = pltpu.roll(ext, 1, 0)[SUBLANES:]
        u2 = pltpu.roll(ext, 2, 0)[SUBLANES:]
        return u2 * cw_ref[0:1, :] + u1 * cw_ref[1:2, :] + u * cw_ref[2:3, :] + cb_ref[...]

    def matmuls(c):
        hc = h_ref[c * chunk:(c + 1) * chunk, :]
        return _dot(hc, wg), _dot(hc, wv)

    n_chunks = seq // chunk
    halo_g = jnp.zeros((SUBLANES, tn), F32)
    halo_v = jnp.zeros((SUBLANES, tn), F32)
    u_next = matmuls(0)
    for c in range(n_chunks):
        sl = slice(c * chunk, (c + 1) * chunk)
        ug, uv = u_next
        if c + 1 < n_chunks:
            u_next = matmuls(c + 1)
        cg = conv(ug, halo_g, cwg_ref, cbg_ref)
        cv = conv(uv, halo_v, cwv_ref, cbv_ref)
        halo_g = ug[chunk - SUBLANES:]
        halo_v = uv[chunk - SUBLANES:]
        o_ref[sl, :] = (cg * jax.nn.sigmoid(cg) * cv).astype(BF16)


def _ffn_up(h, ffn_w_up, conv_w, conv_b, layer):
    batch, seq, d = h.shape
    depth, _, two_f = ffn_w_up.shape
    d_ff = two_f // 2
    tn = 512 if d_ff % 512 == 0 else LANES
    nt = d_ff // tn
    chunk = min(1024, seq)
    lo = lambda b, j: (layer, 0, j)
    hi = lambda b, j: (layer, 0, nt + j)
    conv_b = conv_b.reshape(depth, 1, two_f)
    return pl.pallas_call(
        functools.partial(_ffn_up_kernel, chunk=chunk),
        grid=(batch, nt),
        in_specs=[
            pl.BlockSpec((None, seq, d), lambda b, j: (b, 0, 0)),
            pl.BlockSpec((None, d, tn), lo), pl.BlockSpec((None, d, tn), hi),
            pl.BlockSpec((None, CONV_WIDTH, tn), lo), pl.BlockSpec((None, CONV_WIDTH, tn), hi),
            pl.BlockSpec((None, 1, tn), lo), pl.BlockSpec((None, 1, tn), hi),
        ],
        out_specs=pl.BlockSpec((None, seq, tn), lambda b, j: (b, 0, j)),
        out_shape=jax.ShapeDtypeStruct((batch, seq, d_ff), BF16),
        compiler_params=_params(("parallel", "arbitrary")),
        name="ffn_up",
    )(h, ffn_w_up, ffn_w_up, conv_w, conv_w, conv_b, conv_b)


def kernel(x, c, positions, fox_w_in, fox_b_f, fox_w_o, swa_w_in, swa_sinks, swa_w_o, ada_w, ada_b,
           ffn_w_up, ffn_conv_w, ffn_conv_b, ffn_w_down, ln_mix_g, ln_mix_b, ln_ffn_g, ln_ffn_b):
    batch, seq, d = x.shape
    depth = ada_w.shape[0]
    alpha = (2.0 * depth) ** 0.25
    fox_heads = fox_b_f.shape[1]
    fox_dh = d // fox_heads

    mod = _ada(c, ada_w, ada_b)
    rope_tables = _rope_tables(positions) if depth > 1 else None
    w_down = ffn_w_down.astype(BF16)
    fox_w = _fox_w_bf16(fox_w_in)

    h = None
    for i in range(depth):
        j = i // 2
        if i % 2 == 0:
            b_f = jnp.pad(fox_b_f[j], (0, LANES - fox_heads)).reshape(1, LANES)
            if h is None:
                qkv, fl = _inproj_fox(x, mod, i, fox_w, j, b_f, head_dim=fox_dh)
            else:
                qkv, fl = _inproj_fox(h, None, i, fox_w, j, b_f, head_dim=fox_dh)
            ccol, crow = _fox_cum(fl, fox_heads)
            o = _fox_attn(qkv, crow, ccol, heads=fox_heads)
            w_o = fox_w_o
        else:
            if h is None:
                raise NotImplementedError("SWA as the first layer")
            q, kv = _inproj_swa(h, swa_w_in, j, rope_tables, q_heads=swa_sinks.shape[1])
            o = _swa_attn(q, kv, swa_sinks, j)
            w_o = swa_w_o
        x, h2 = _proj_ln(o, w_o, j, x, mod, i, G1, ln_mix_g, ln_mix_b, (i, SC2, SH2), alpha=alpha, tm=512)
        hmid = _ffn_up(h2, ffn_w_up, ffn_conv_w, ffn_conv_b, i)
        next_mod = (i + 1, SC1, SH1) if i + 1 < depth else None
        x, h = _proj_ln(hmid, w_down, i, x, mod, i, G2, ln_ffn_g, ln_ffn_b, next_mod, alpha=alpha, tm=256)
    return x
```

```python
import functools

import jax
import jax.numpy as jnp
from jax import lax
from jax.experimental import pallas as pl
from jax.experimental.pallas import tpu as pltpu

F32 = jnp.float32
BF16 = jnp.bfloat16

LANES = 128
SUBLANES = 8
VMEM_LIMIT_BYTES = 56 * 1024 * 1024

LN_EPS = 1e-5
ROPE_THETA = 500000.0
ROPE_DIM = 16
ROPE_HALF = ROPE_DIM // 2
SWA_HEAD_DIM = 64
SWA_WINDOW = 128
CONV_WIDTH = 3
MASK_VALUE = -1e30
LOG2E = 1.4426950408889634

SH1, SC1, G1, SH2, SC2, G2 = range(6)


def _params(semantics):
    return pltpu.CompilerParams(dimension_semantics=semantics, vmem_limit_bytes=VMEM_LIMIT_BYTES)


def _dot(a, b):
    return jnp.dot(a, b, preferred_element_type=F32)


def _dot_nt(a, b):
    return lax.dot_general(a, b, (((1,), (1,)), ((), ())), preferred_element_type=F32)


def _resident(block_shape, index_map):
    return pl.BlockSpec(block_shape, index_map, pipeline_mode=pl.Buffered(1))


def _mod_spec(mod, layer, chunk):
    _, _, batch, d = mod.shape
    return _resident((None, None, batch, d), lambda *_: (layer, chunk, 0, 0))


def _ada_kernel(c_ref, w_ref, b_ref, o_ref):
    c = c_ref[...]
    c_act = (c * jax.nn.sigmoid(c)).astype(BF16)
    o_ref[...] = _dot(c_act, w_ref[...].astype(BF16)) + b_ref[...]


def _ada(c, ada_w, ada_b):
    depth, d, six_d = ada_w.shape
    batch = c.shape[0]
    tn = min(1024, d)
    nt = d // tn
    return pl.pallas_call(
        _ada_kernel,
        grid=(depth, 6, nt),
        in_specs=[
            pl.BlockSpec((batch, d), lambda i, k, j: (0, 0)),
            pl.BlockSpec((None, d, tn), lambda i, k, j: (i, 0, k * nt + j)),
            pl.BlockSpec((None, 1, tn), lambda i, k, j: (i, 0, k * nt + j)),
        ],
        out_specs=pl.BlockSpec((None, None, batch, tn), lambda i, k, j: (i, k, 0, j)),
        out_shape=jax.ShapeDtypeStruct((depth, 6, batch, d), F32),
        compiler_params=_params(("parallel", "parallel", "parallel")),
        name="ada_mod",
    )(c, ada_w, ada_b.reshape(depth, 1, six_d))


def _cast_cols_kernel(wt_ref, o_ref, *, valid_cols):
    tn = wt_ref.shape[0]
    w = wt_ref[...].T
    col = pl.program_id(1) * tn + lax.broadcasted_iota(jnp.int32, w.shape, 1)
    o_ref[...] = jnp.where(col < valid_cols, w, 0.0).astype(BF16)


def _fox_w_bf16(fox_w_in):
    n, d, cols = fox_w_in.shape
    tn = 512
    nt = pl.cdiv(cols, tn)
    return pl.pallas_call(
        functools.partial(_cast_cols_kernel, valid_cols=cols),
        grid=(n, nt),
        in_specs=[pl.BlockSpec((None, tn, d), lambda l, j: (l, j, 0))],
        out_specs=pl.BlockSpec((None, d, tn), lambda l, j: (l, 0, j)),
        out_shape=jax.ShapeDtypeStruct((n, d, nt * tn), BF16),
        compiler_params=_params(("parallel", "parallel")),
        name="fox_w_cast",
    )(jnp.swapaxes(fox_w_in, 1, 2))


def _inproj_fox_kernel(*refs, modulate, tiles_per_seq, q_tiles, q_scale):
    if modulate:
        x_ref, sc_ref, sh_ref, w_ref, wf_ref, bf_ref, qkv_ref, fl_ref, h_scr = refs
    else:
        x_ref, w_ref, wf_ref, bf_ref, qkv_ref, fl_ref = refs
    i = pl.program_id(0)
    j = pl.program_id(1)

    if modulate:
        @pl.when(j == 0)
        def _():
            b = i // tiles_per_seq
            sc = sc_ref[pl.ds(b, 1), :]
            sh = sh_ref[pl.ds(b, 1), :]
            h_scr[...] = (x_ref[...] * (1.0 + sc) + sh).astype(BF16)
        h_ref = h_scr
    else:
        h_ref = x_ref

    @pl.when(j == 0)
    def _():
        fl_ref[...] = _dot(h_ref[...], wf_ref[...]) + bf_ref[...]

    acc = _dot(h_ref[...], w_ref[...])
    acc = acc * jnp.where(j < q_tiles, q_scale, 1.0)
    for hh in range(acc.shape[1] // LANES):
        qkv_ref[hh] = acc[:, hh * LANES:(hh + 1) * LANES].astype(BF16)


def _inproj_fox(x_or_h, mod, layer, fox_w, fox_layer, b_f, *, head_dim):
    batch, seq, d = x_or_h.shape
    modulate = mod is not None
    n_out = 3 * d
    assert head_dim == LANES
    tn = min(1024, d)
    tm = min(1024 if modulate else 2048, seq)
    nm = seq // tm
    slots = n_out // LANES
    kern = functools.partial(_inproj_fox_kernel, modulate=modulate, tiles_per_seq=nm,
                             q_tiles=d // tn, q_scale=LOG2E * float(head_dim) ** -0.5)
    in_specs = [pl.BlockSpec((None, tm, d), lambda i, j: (i // nm, i % nm, 0))]
    args = [x_or_h]
    if modulate:
        in_specs += [_mod_spec(mod, layer, SC1), _mod_spec(mod, layer, SH1)]
        args += [mod, mod]
    in_specs += [
        pl.BlockSpec((None, d, tn), lambda i, j: (fox_layer, 0, j)),
        _resident((None, d, LANES), lambda i, j: (fox_layer, 0, n_out // LANES)),
        _resident((1, LANES), lambda i, j: (0, 0)),
    ]
    args += [fox_w, fox_w, b_f]
    return pl.pallas_call(
        kern,
        grid=(batch * nm, n_out // tn),
        in_specs=in_specs,
        out_specs=[
            pl.BlockSpec((None, tn // LANES, tm, LANES), lambda i, j: (i // nm, j, i % nm, 0)),
            pl.BlockSpec((None, tm, LANES), lambda i, j: (i // nm, i % nm, 0)),
        ],
        out_shape=[
            jax.ShapeDtypeStruct((batch, slots, seq, LANES), BF16),
            jax.ShapeDtypeStruct((batch, seq, LANES), F32),
        ],
        scratch_shapes=[pltpu.VMEM((tm, d), BF16)] if modulate else [],
        compiler_params=_params(("parallel", "arbitrary")),
        name="fox_inproj",
    )(*args)


def _fox_cum_kernel(fl_ref, col_ref, row_ref, *, heads):
    seq = fl_ref.shape[0]
    x = fl_ref[...]
    log_f = (jnp.minimum(x, 0.0) - jnp.log1p(jnp.exp(-jnp.abs(x)))) * LOG2E
    r = lax.broadcasted_iota(jnp.int32, (LANES, LANES), 0)
    c = lax.broadcasted_iota(jnp.int32, (LANES, LANES), 1)
    tri = (c <= r).astype(BF16)
    carry = jnp.zeros((1, LANES), F32)
    for blk in range(seq // LANES):
        xb = log_f[blk * LANES:(blk + 1) * LANES]
        hi = xb.astype(BF16)
        rem = xb - hi.astype(F32)
        mid = rem.astype(BF16)
        lo = (rem - mid.astype(F32)).astype(BF16)
        cb = _dot(tri, hi) + _dot(tri, mid) + _dot(tri, lo) + carry
        col_ref[blk * LANES:(blk + 1) * LANES, :] = cb
        carry = cb[LANES - 1:LANES, :]
    row_ref[...] = col_ref[...].T[:heads]


def _fox_cum(fl, heads):
    batch, seq, _ = fl.shape
    return pl.pallas_call(
        functools.partial(_fox_cum_kernel, heads=heads),
        grid=(batch,),
        in_specs=[pl.BlockSpec((None, seq, LANES), lambda b: (b, 0, 0))],
        out_specs=[
            pl.BlockSpec((None, seq, LANES), lambda b: (b, 0, 0)),
            pl.BlockSpec((None, heads, seq), lambda b: (b, 0, 0)),
        ],
        out_shape=[
            jax.ShapeDtypeStruct((batch, seq, LANES), F32),
            jax.ShapeDtypeStruct((batch, heads, seq), F32),
        ],
        compiler_params=_params(("parallel",)),
        name="fox_cum",
    )(fl)


def _fox_attn_kernel(q_ref, k_ref, v_ref, crow_ref, ccol_ref, o_ref, *, heads_per_step, blk):
    seq = q_ref.shape[1]
    hb = pl.program_id(1)
    lane = lax.broadcasted_iota(jnp.int32, (blk, LANES), 1)
    rows = lax.broadcasted_iota(jnp.int32, (blk, blk), 0)
    cols = lax.broadcasted_iota(jnp.int32, (blk, blk), 1)
    causal = cols <= rows

    def scores(hh, qi):
        h = hb * heads_per_step + hh
        q0 = qi * blk
        lk = q0 + blk
        s = _dot_nt(q_ref[hh, q0:lk, :], k_ref[hh, :lk, :]) - crow_ref[pl.ds(h, 1), :lk]
        diag = jnp.where(causal, s[:, q0:], MASK_VALUE)
        return diag if qi == 0 else jnp.concatenate([s[:, :q0], diag], axis=1)

    def finish(hh, qi, s):
        h = hb * heads_per_step + hh
        q0 = qi * blk
        lk = q0 + blk
        cq = jnp.sum(jnp.where(lane == h, ccol_ref[q0:lk, :], 0.0), axis=-1, keepdims=True)
        m = jnp.max(s, axis=-1, keepdims=True) + cq
        p = jnp.exp2(s + (cq - m))
        l = jnp.sum(p, axis=-1, keepdims=True)
        acc = _dot(p.astype(BF16), v_ref[hh, :lk, :])
        o_ref[q0:lk, hh * LANES:(hh + 1) * LANES] = (acc / l).astype(BF16)

    work = [(hh, qi) for hh in range(heads_per_step) for qi in range(seq // blk)]
    s_next = scores(*work[0])
    for idx, item in enumerate(work):
        s_cur = s_next
        if idx + 1 < len(work):
            s_next = scores(*work[idx + 1])
        finish(*item, s_cur)


def _fox_attn(qkv, crow, ccol, *, heads):
    batch, slots, seq, dh = qkv.shape
    assert slots == 3 * heads and dh == LANES
    hps = 4 if heads % 4 == 0 else 1
    groups = heads // hps
    blk = min(256, seq)
    kern = functools.partial(_fox_attn_kernel, heads_per_step=hps, blk=blk)
    head_spec = lambda off: pl.BlockSpec((None, hps, seq, dh), lambda b, g: (b, off + g, 0, 0))
    return pl.pallas_call(
        kern,
        grid=(batch, groups),
        in_specs=[
            head_spec(0), head_spec(groups), head_spec(2 * groups),
            pl.BlockSpec((None, heads, seq), lambda b, g: (b, 0, 0)),
            pl.BlockSpec((None, seq, LANES), lambda b, g: (b, 0, 0)),
        ],
        out_specs=pl.BlockSpec((None, seq, hps * dh), lambda b, g: (b, 0, g)),
        out_shape=jax.ShapeDtypeStruct((batch, seq, heads * dh), BF16),
        compiler_params=_params(("parallel", "parallel")),
        name="fox_attn",
    )(qkv, qkv, qkv, crow, ccol)


def _fill_rope_tables(pos_ref, inv_ref, o_ref):
    pos = pos_ref[...].astype(F32)
    ang = pos * inv_ref[...]
    cos = jnp.cos(ang)
    sin = jnp.sin(ang)
    idx = lax.broadcasted_iota(jnp.int32, ang.shape, 1) % SWA_HEAD_DIM
    o_ref[0] = jnp.where(idx < ROPE_DIM, cos, 1.0)
    o_ref[1] = jnp.where((idx >= ROPE_HALF) & (idx < ROPE_DIM), sin, 0.0)
    o_ref[2] = jnp.where(idx < ROPE_HALF, -sin, 0.0)


def _rope_tables(positions):
    batch, seq = positions.shape
    inv_freq = ROPE_THETA ** (-jnp.arange(0, ROPE_DIM, 2, dtype=F32) / ROPE_DIM)
    idx = jnp.arange(LANES) % SWA_HEAD_DIM
    inv_lane = jnp.where(idx < ROPE_DIM, inv_freq[idx % ROPE_HALF], 0.0).reshape(1, LANES)
    return positions.reshape(batch, seq, 1), inv_lane


def _inproj_swa_kernel(h_ref, wq_ref, wkv_ref, pos_ref, inv_ref, q_ref, kv_ref, rt_ref, *,
                       q_tiles, q_scale, k_width):
    j = pl.program_id(1)
    seq = h_ref.shape[0]

    @pl.when(j == 0)
    def _():
        _fill_rope_tables(pos_ref, inv_ref, rt_ref)
    chunk = min(512, seq)
    n_chunks = seq // chunk

    def project(w_ref, store):
        w = w_ref[...].astype(BF16)
        acc_next = _dot(h_ref[0:chunk, :], w)
        for c in range(n_chunks):
            rows = slice(c * chunk, (c + 1) * chunk)
            acc = acc_next
            if c + 1 < n_chunks:
                acc_next = _dot(h_ref[(c + 1) * chunk:(c + 2) * chunk, :], w)
            cos = rt_ref[0, rows, :]
            sin_lo = rt_ref[1, rows, :]
            sin_hi = rt_ref[2, rows, :]
            for cc in range(acc.shape[1] // LANES):
                a = acc[:, cc * LANES:(cc + 1) * LANES]
                rot = (a * cos + pltpu.roll(a, ROPE_HALF, 1) * sin_lo
                       + pltpu.roll(a, LANES - ROPE_HALF, 1) * sin_hi)
                store(rows, cc, a, rot)

    @pl.when(j < q_tiles)
    def _():
        def store(rows, cc, a, rot):
            q_ref[rows, cc * LANES:(cc + 1) * LANES] = (rot * q_scale).astype(BF16)
        project(wq_ref, store)

    @pl.when(j == q_tiles)
    def _():
        def store(rows, cc, a, rot):
            kv_ref[rows, cc * LANES:(cc + 1) * LANES] = (rot if cc * LANES < k_width else a).astype(BF16)
        project(wkv_ref, store)


def _inproj_swa(h, swa_w_in, swa_layer, rope_tables, *, q_heads):
    batch, seq, d = h.shape
    nq = q_heads * SWA_HEAD_DIM
    kvw = swa_w_in.shape[2] - nq
    assert (kvw // 2) % LANES == 0
    tn = min(512, nq)
    assert nq % tn == 0 and nq % kvw == 0
    q_tiles = nq // tn
    kern = functools.partial(_inproj_swa_kernel, q_tiles=q_tiles,
                             q_scale=LOG2E * float(SWA_HEAD_DIM) ** -0.5, k_width=kvw // 2)
    return pl.pallas_call(
        kern,
        grid=(batch, q_tiles + 1),
        in_specs=[
            pl.BlockSpec((None, seq, d), lambda b, j: (b, 0, 0)),
            pl.BlockSpec((None, d, tn), lambda b, j: (swa_layer, 0, jnp.minimum(j, q_tiles - 1))),
            _resident((None, d, kvw), lambda b, j: (swa_layer, 0, nq // kvw)),
            pl.BlockSpec((None, seq, 1), lambda b, j: (b, 0, 0)),
            _resident((1, LANES), lambda b, j: (0, 0)),
        ],
        out_specs=[
            pl.BlockSpec((None, seq, tn), lambda b, j: (b, 0, jnp.minimum(j, q_tiles - 1))),
            pl.BlockSpec((None, seq, kvw), lambda b, j: (b, 0, 0)),
        ],
        out_shape=[
            jax.ShapeDtypeStruct((batch, seq, nq), BF16),
            jax.ShapeDtypeStruct((batch, seq, kvw), BF16),
        ],
        scratch_shapes=[pltpu.VMEM((3, seq, LANES), F32)],
        compiler_params=_params(("parallel", "arbitrary")),
        name="swa_inproj",
    )(h, swa_w_in, swa_w_in, *rope_tables)


def _swa_attn_kernel(sink_ref, q_ref, k_ref, v_ref, o_ref, klo_scr, khi_scr, vlo_scr, vhi_scr, *,
                     group, layer):
    seq = q_ref.shape[0]
    blk = SWA_WINDOW
    g = pl.program_id(1)
    lane_s = lax.broadcasted_iota(jnp.int32, (seq, LANES), 1)
    hi_half = lane_s >= SWA_HEAD_DIM
    own_half = (lane_s // SWA_HEAD_DIM) == (g % 2)
    kp = k_ref[...].astype(F32)
    vp = v_ref[...].astype(F32)
    k2 = jnp.where(own_half, kp, pltpu.roll(kp, SWA_HEAD_DIM, 1))
    v2 = jnp.where(own_half, vp, pltpu.roll(vp, SWA_HEAD_DIM, 1))
    klo_scr[...] = jnp.where(hi_half, 0.0, k2).astype(BF16)
    khi_scr[...] = jnp.where(hi_half, k2, 0.0).astype(BF16)
    vlo_scr[...] = jnp.where(hi_half, 0.0, v2).astype(BF16)
    vhi_scr[...] = jnp.where(hi_half, v2, 0.0).astype(BF16)

    r = lax.broadcasted_iota(jnp.int32, (blk, 2 * blk), 0)
    c = lax.broadcasted_iota(jnp.int32, (blk, 2 * blk), 1)
    rel_first = r - c
    rel_rest = r + blk - c
    bias_first = jnp.where((rel_first >= 0) & (rel_first < SWA_WINDOW), 0.0, MASK_VALUE)
    bias_rest = jnp.where((rel_rest >= 0) & (rel_rest < SWA_WINDOW), 0.0, MASK_VALUE)

    def q_block(n, _):
        q0 = pl.multiple_of(n * blk, blk)
        ks = pl.multiple_of(jnp.maximum(n - 1, 0) * blk, blk)
        bias = jnp.where(n == 0, bias_first, bias_rest)
        keys = (klo_scr[pl.ds(ks, 2 * blk), :], khi_scr[pl.ds(ks, 2 * blk), :])
        vals = (vlo_scr[pl.ds(ks, 2 * blk), :], vhi_scr[pl.ds(ks, 2 * blk), :])
        for pair in range(group // 2):
            qp = q_ref[pl.ds(q0, blk), pair * LANES:(pair + 1) * LANES]
            out = jnp.zeros((blk, LANES), F32)
            for e in range(2):
                sink = sink_ref[layer, g * group + 2 * pair + e] * LOG2E
                s = _dot_nt(qp, keys[e]) + bias
                m = jnp.maximum(jnp.max(s, axis=-1, keepdims=True), sink)
                p = jnp.exp2(s - m)
                denom = jnp.sum(p, axis=-1, keepdims=True) + jnp.exp2(sink - m)
                out = out + _dot(p.astype(BF16), vals[e]) / denom
            o_ref[pl.ds(q0, blk), pair * LANES:(pair + 1) * LANES] = out.astype(BF16)
        return 0

    lax.fori_loop(0, seq // blk, q_block, 0, unroll=8)


def _swa_attn(q, kv, swa_sinks, swa_layer):
    batch, seq, nq = q.shape
    kvw = kv.shape[2]
    kv_heads = kvw // (2 * SWA_HEAD_DIM)
    q_heads = nq // SWA_HEAD_DIM
    group = q_heads // kv_heads
    gw = group * SWA_HEAD_DIM
    assert group % 2 == 0 and gw % LANES == 0 and kv_heads % 2 == 0
    v_off = kv_heads // 2
    return pl.pallas_call(
        functools.partial(_swa_attn_kernel, group=group, layer=swa_layer),
        grid=(batch, kv_heads),
        in_specs=[
            pl.BlockSpec(memory_space=pltpu.SMEM),
            pl.BlockSpec((None, seq, gw), lambda b, g: (b, 0, g)),
            pl.BlockSpec((None, seq, LANES), lambda b, g: (b, 0, g // 2)),
            pl.BlockSpec((None, seq, LANES), lambda b, g: (b, 0, v_off + g // 2)),
        ],
        out_specs=pl.BlockSpec((None, seq, gw), lambda b, g: (b, 0, g)),
        out_shape=jax.ShapeDtypeStruct((batch, seq, nq), BF16),
        scratch_shapes=[pltpu.VMEM((seq, LANES), BF16)] * 4,
        compiler_params=_params(("parallel", "arbitrary")),
        name="swa_attn",
    )(swa_sinks, q, kv, kv)


def _proj_ln_kernel(*refs, alpha, rows_per_batch, emit_next, cast_w, sub):
    refs = list(refs)
    w_scr = refs.pop() if cast_w else None
    if emit_next:
        a_ref, w_ref, x_ref, g_ref, lg_ref, lb_ref, sc_ref, sh_ref, xo_ref, ho_ref = refs
    else:
        a_ref, w_ref, x_ref, g_ref, lg_ref, lb_ref, xo_ref = refs
    if cast_w:
        @pl.when(pl.program_id(0) == 0)
        def _():
            w_scr[...] = w_ref[...].astype(BF16)
        w_ref = w_scr
    tm = a_ref.shape[0]
    b = (pl.program_id(0) * tm) // rows_per_batch
    gate = 1.0 + g_ref[pl.ds(b, 1), :]
    ln_g = lg_ref[...]
    ln_b = lb_ref[...]
    if emit_next:
        nsc = 1.0 + sc_ref[pl.ds(b, 1), :]
        nsh = sh_ref[pl.ds(b, 1), :]
    for c in range(tm // sub):
        sl = slice(c * sub, (c + 1) * sub)
        y = _dot(a_ref[sl, :], w_ref[...])
        z = alpha * x_ref[sl, :] + gate * y
        mu = jnp.mean(z, axis=-1, keepdims=True)
        zc = z - mu
        var = jnp.mean(zc * zc, axis=-1, keepdims=True)
        xn = zc * lax.rsqrt(var + LN_EPS) * ln_g + ln_b
        xo_ref[sl, :] = xn
        if emit_next:
            ho_ref[sl, :] = (xn * nsc + nsh).astype(BF16)


def _proj_ln(a, w, w_layer, x, mod, layer, gate_chunk, ln_g, ln_b, next_mod, *, alpha, tm):
    batch, seq, d = x.shape
    k = a.shape[-1]
    rows = batch * seq
    tm = min(tm, seq)
    emit_next = next_mod is not None
    cast_w = w.dtype != BF16
    kern = functools.partial(_proj_ln_kernel, alpha=alpha, rows_per_batch=seq, emit_next=emit_next,
                             cast_w=cast_w, sub=min(128, tm))
    row_spec = lambda width: pl.BlockSpec((tm, width), lambda i: (i, 0))
    vec_spec = _resident((None, 1, d), lambda i: (layer, 0, 0))
    in_specs = [row_spec(k), _resident((None, k, d), lambda i: (w_layer, 0, 0)), row_spec(d),
                _mod_spec(mod, layer, gate_chunk), vec_spec, vec_spec]
    depth = ln_g.shape[0]
    args = [a.reshape(rows, k), w, x.reshape(rows, d), mod, ln_g.reshape(depth, 1, d), ln_b.reshape(depth, 1, d)]
    out_specs = [row_spec(d)]
    out_shape = [jax.ShapeDtypeStruct((rows, d), F32)]
    if emit_next:
        nl, nsc, nsh = next_mod
        in_specs += [_mod_spec(mod, nl, nsc), _mod_spec(mod, nl, nsh)]
        args += [mod, mod]
        out_specs.append(row_spec(d))
        out_shape.append(jax.ShapeDtypeStruct((rows, d), BF16))
    outs = pl.pallas_call(
        kern,
        grid=(rows // tm,),
        in_specs=in_specs,
        out_specs=out_specs,
        out_shape=out_shape,
        scratch_shapes=[pltpu.VMEM((k, d), BF16)] if cast_w else [],
        compiler_params=_params(("arbitrary",)),
        name="proj_ln",
    )(*args)
    x_new = outs[0].reshape(batch, seq, d)
    return x_new, (outs[1].reshape(batch, seq, d) if emit_next else None)


def _ffn_up_kernel(h_ref, wg_ref, wv_ref, cwg_ref, cwv_ref, cbg_ref, cbv_ref, o_ref, *, chunk):
    seq = h_ref.shape[0]
    tn = wg_ref.shape[1]
    wg = wg_ref[...].astype(BF16)
    wv = wv_ref[...].astype(BF16)

    def conv(u, halo, cw_ref, cb_ref):
        ext = jnp.concatenate([halo, u], axis=0)
        u1 = pltpu.roll(ext, 1, 0)[SUBLANES:]
        u2 = pltpu.roll(ext, 2, 0)[SUBLANES:]
        return u2 * cw_ref[0:1, :] + u1 * cw_ref[1:2, :] + u * cw_ref[2:3, :] + cb_ref[...]

    def matmuls(c):
        hc = h_ref[c * chunk:(c + 1) * chunk, :]
        return _dot(hc, wg), _dot(hc, wv)

    n_chunks = seq // chunk
    halo_g = jnp.zeros((SUBLANES, tn), F32)
    halo_v = jnp.zeros((SUBLANES, tn), F32)
    u_next = matmuls(0)
    for c in range(n_chunks):
        sl = slice(c * chunk, (c + 1) * chunk)
        ug, uv = u_next
        if c + 1 < n_chunks:
            u_next = matmuls(c + 1)
        cg = conv(ug, halo_g, cwg_ref, cbg_ref)
        cv = conv(uv, halo_v, cwv_ref, cbv_ref)
        halo_g = ug[chunk - SUBLANES:]
        halo_v = uv[chunk - SUBLANES:]
        o_ref[sl, :] = (cg * jax.nn.sigmoid(cg) * cv).astype(BF16)


def _ffn_up(h, ffn_w_up, conv_w, conv_b, layer):
    batch, seq, d = h.shape
    depth, _, two_f = ffn_w_up.shape
    d_ff = two_f // 2
    tn = 512 if d_ff % 512 == 0 else LANES
    nt = d_ff // tn
    chunk = min(1024, seq)
    lo = lambda b, j: (layer, 0, j)
    hi = lambda b, j: (layer, 0, nt + j)
    conv_b = conv_b.reshape(depth, 1, two_f)
    return pl.pallas_call(
        functools.partial(_ffn_up_kernel, chunk=chunk),
        grid=(batch, nt),
        in_specs=[
            pl.BlockSpec((None, seq, d), lambda b, j: (b, 0, 0)),
            pl.BlockSpec((None, d, tn), lo), pl.BlockSpec((None, d, tn), hi),
            pl.BlockSpec((None, CONV_WIDTH, tn), lo), pl.BlockSpec((None, CONV_WIDTH, tn), hi),
            pl.BlockSpec((None, 1, tn), lo), pl.BlockSpec((None, 1, tn), hi),
        ],
        out_specs=pl.BlockSpec((None, seq, tn), lambda b, j: (b, 0, j)),
        out_shape=jax.ShapeDtypeStruct((batch, seq, d_ff), BF16),
        compiler_params=_params(("parallel", "arbitrary")),
        name="ffn_up",
    )(h, ffn_w_up, ffn_w_up, conv_w, conv_w, conv_b, conv_b)


def kernel(x, c, positions, fox_w_in, fox_b_f, fox_w_o, swa_w_in, swa_sinks, swa_w_o, ada_w, ada_b,
           ffn_w_up, ffn_conv_w, ffn_conv_b, ffn_w_down, ln_mix_g, ln_mix_b, ln_ffn_g, ln_ffn_b):
    batch, seq, d = x.shape
    depth = ada_w.shape[0]
    alpha = (2.0 * depth) ** 0.25
    fox_heads = fox_b_f.shape[1]
    fox_dh = d // fox_heads

    mod = _ada(c, ada_w, ada_b)
    rope_tables = _rope_tables(positions) if depth > 1 else None
    w_down = ffn_w_down.astype(BF16)
    fox_w = _fox_w_bf16(fox_w_in)

    h = None
    for i in range(depth):
        j = i // 2
        if i % 2 == 0:
            b_f = jnp.pad(fox_b_f[j], (0, LANES - fox_heads)).reshape(1, LANES)
            if h is None:
                qkv, fl = _inproj_fox(x, mod, i, fox_w, j, b_f, head_dim=fox_dh)
            else:
                qkv, fl = _inproj_fox(h, None, i, fox_w, j, b_f, head_dim=fox_dh)
            ccol, crow = _fox_cum(fl, fox_heads)
            o = _fox_attn(qkv, crow, ccol, heads=fox_heads)
            w_o = fox_w_o
        else:
            if h is None:
                raise NotImplementedError("SWA as the first layer")
            q, kv = _inproj_swa(h, swa_w_in, j, rope_tables, q_heads=swa_sinks.shape[1])
            o = _swa_attn(q, kv, swa_sinks, j)
            w_o = swa_w_o
        x, h2 = _proj_ln(o, w_o, j, x, mod, i, G1, ln_mix_g, ln_mix_b, (i, SC2, SH2), alpha=alpha, tm=512)
        hmid = _ffn_up(h2, ffn_w_up, ffn_conv_w, ffn_conv_b, i)
        next_mod = (i + 1, SC1, SH1) if i + 1 < depth else None
        x, h = _proj_ln(hmid, w_down, i, x, mod, i, G2, ln_ffn_g, ln_ffn_b, next_mod, alpha=alpha, tm=256)
    return x
```
